```python
import math
import jax
import jax.numpy as jnp
from jax import lax
import numpy as np

D_MODEL = 2048
BATCH = 4
SEQ = 2048
DEPTH = 2
DEC_BATCH = 128
DEC_SEQ = 1
PAST_LEN = 2048
PAGE_SIZE = 128

M_WIDTH = 3 * D_MODEL // 8
M_HEADS = 4
M_DV = M_WIDTH // M_HEADS
M_DK = M_DV // 2
M_QK = M_HEADS * M_DK
M_CHUNK = 64
CONV_WIDTH = D_MODEL // 4
CONV_K = 3
A_GROUPS = ((128, 1), (512, 4), (2048, 16))
A_HPG = 4
A_HEAD_DIM = 64
A_HEADS = A_HPG * len(A_GROUPS)
A_WIDTH = A_HEADS * A_HEAD_DIM
A_OUT = A_HPG * A_HEAD_DIM
A_BLOCK = 128
ROPE_THETA = 10000.0
N_BRANCH = 3
D_FF = ((8 * D_MODEL // 3 + 255) // 256) * 256
LN_EPS = 1e-5
ALPHA = (2 * DEPTH) ** 0.25
BETA = (8 * DEPTH) ** -0.25
IN_SIZES = (M_QK, M_QK, M_WIDTH, M_HEADS, M_HEADS, M_WIDTH,
            CONV_WIDTH, CONV_WIDTH, CONV_WIDTH,
            A_WIDTH, A_WIDTH, A_WIDTH,
            N_BRANCH * D_MODEL)
IN_WIDTH = sum(IN_SIZES)
IN_OFFSETS = tuple(int(o) for o in np.cumsum(IN_SIZES)[:-1])

kernel_name = 'hybrid_mlstm_shortconv_dilattn_decoder_step'

F32 = jnp.float32


def layer_norm(x, g, b):
    xf = x.astype(F32)
    mu = jnp.mean(xf, axis=-1, keepdims=True)
    var = jnp.mean(jnp.square(xf - mu), axis=-1, keepdims=True)
    return ((xf - mu) * lax.rsqrt(var + LN_EPS) * g + b).astype(x.dtype)


def head_norm(h, g):
    mu = jnp.mean(h, axis=-1, keepdims=True)
    var = jnp.mean(jnp.square(h - mu), axis=-1, keepdims=True)
    return (h - mu) * lax.rsqrt(var + LN_EPS) * g.reshape(h.shape[-2], h.shape[-1])


def swiglu(x, w_in, w_out):
    a, b = jnp.split(x @ w_in, 2, axis=-1)
    return (jax.nn.silu(a) * b) @ w_out


def rope(x, pos):
    dh = x.shape[-1]
    half = dh // 2
    inv = ROPE_THETA ** (-(2.0 * jnp.arange(half, dtype=F32)) / dh)
    ang = pos.astype(F32)[:, None] * inv[None, :]
    cos = jnp.cos(ang)[None, :, None, :]
    sin = jnp.sin(ang)[None, :, None, :]
    xf = x.astype(F32)
    x1, x2 = xf[..., :half], xf[..., half:]
    return jnp.concatenate([x1 * cos - x2 * sin, x2 * cos + x1 * sin], axis=-1)


def mlstm(q, k, v, i_pre, f_pre, C0, n0, m0):
    Bn, S, H, DK = q.shape
    L = math.gcd(S, M_CHUNK)
    nc = S // L
    k = k * DK ** -0.5
    log_f = jax.nn.log_sigmoid(f_pre)

    def chunks(a):
        a = a.reshape((Bn, nc, L, H) + a.shape[3:])
        return jnp.moveaxis(a, (1, 3), (0, 2))

    causal = jnp.tril(jnp.ones((L, L), dtype=bool))

    def step(carry, inp):
        C, n, m = carry
        qc, kc, vc, ic, fc = inp
        b = jnp.cumsum(fc, axis=-1)
        d_log = jnp.where(causal, b[..., :, None] - b[..., None, :] + ic[..., None, :], -jnp.inf)
        inter = b + m[..., None]
        m_t = jnp.maximum(jnp.max(d_log, axis=-1), inter)
        s = jnp.einsum('bhtd,bhsd->bhts', qc, kc) * jnp.exp(d_log - m_t[..., None])
        w_inter = jnp.exp(inter - m_t)
        num = jnp.einsum('bhts,bhsv->bhtv', s, vc) + w_inter[..., None] * jnp.einsum('bhtd,bhdv->bhtv', qc, C)
        den = jnp.sum(s, axis=-1) + w_inter * jnp.einsum('bhtd,bhd->bht', qc, n)
        h = num / jnp.maximum(jnp.abs(den), jnp.exp(-m_t))[..., None]
        b_last = b[..., -1]
        g = b_last[..., None] - b + ic
        m_new = jnp.maximum(b_last + m, jnp.max(g, axis=-1))
        w_k = jnp.exp(g - m_new[..., None])
        decay = jnp.exp(b_last + m - m_new)
        C_new = decay[..., None, None] * C + jnp.einsum('bhs,bhsd,bhsv->bhdv', w_k, kc, vc)
        n_new = decay[..., None] * n + jnp.einsum('bhs,bhsd->bhd', w_k, kc)
        return (C_new, n_new, m_new), h

    carry0 = (C0.astype(F32), n0.astype(F32), m0.astype(F32))
    (C1, n1, m1), h = lax.scan(step, carry0, (chunks(q), chunks(k), chunks(v), chunks(i_pre), chunks(log_f)))
    h = jnp.moveaxis(h, (0, 2), (1, 3)).reshape(Bn, S, H, v.shape[-1])
    return h, (C1, n1, m1)


def dilated_attn_prompt(q, kv, window, dil):
    Bn, S, H, Dh = q.shape
    L = S // dil
    w = window // dil
    Q = math.gcd(L, A_BLOCK)
    nb = L // Q

    def streams(a):
        return a.astype(F32).reshape(Bn, L, dil, H, Dh).transpose(0, 2, 3, 1, 4)

    qs = streams(q).reshape(Bn, dil, H, nb, Q, Dh)
    pad = ((0, 0), (0, 0), (0, 0), (w, 0), (0, 0))
    idx = jnp.arange(nb)[:, None] * Q + jnp.arange(Q + w)[None, :]
    kb = jnp.pad(streams(kv[:, :, 0]), pad)[:, :, :, idx]
    vb = jnp.pad(streams(kv[:, :, 1]), pad)[:, :, :, idx]
    s = jnp.einsum('brhnqd,brhnkd->brhnqk', qs, kb) * Dh ** -0.5
    qi = jnp.arange(Q)[:, None]
    kj = jnp.arange(Q + w)[None, :]
    kpos = jnp.arange(nb)[:, None, None] * Q + kj[None] - w
    valid = (kj >= qi)[None] & (kj <= qi + w)[None] & (kpos >= 0)
    s = jnp.where(valid, s, -jnp.inf)
    lse = jax.nn.logsumexp(s, axis=-1)
    o = jnp.einsum('brhnqk,brhnkd->brhnqd', jnp.exp(s - lse[..., None]), vb)
    o = o.reshape(Bn, dil, H, L, Dh).transpose(0, 3, 1, 2, 4).reshape(Bn, S, H, Dh)
    lse = lse.reshape(Bn, dil, H, L).transpose(0, 3, 1, 2).reshape(Bn, S, H)
    return o, lse


def dilated_attn_step(q, kv_new, kv_buf, window, dil):
    Bn, T, H, Dh = q.shape
    Wb = kv_buf.shape[1]
    kv = jnp.concatenate([kv_buf.astype(F32), kv_new.astype(F32)], axis=1)
    nk = window // dil + 1
    idx = Wb + jnp.arange(T)[:, None] - dil * jnp.arange(nk)[None, :]
    valid = idx >= 0
    kvg = kv[:, jnp.maximum(idx, 0)]
    s = jnp.einsum('bthd,btkhd->bthk', q, kvg[:, :, :, 0]) * Dh ** -0.5
    s = jnp.where(valid[None, :, None, :], s, -jnp.inf)
    lse = jax.nn.logsumexp(s, axis=-1)
    o = jnp.einsum('bthk,btkhd->bthd', jnp.exp(s - lse[..., None]), kvg[:, :, :, 1])
    return o, lse


def token_mix(h, pos, conv_prev, C0, n0, m0, kv_bufs, w_in, b_if, m_norm_g, conv_w,
              w_up_m, w_up_c, w_up_a, w_o):
    Bn, S, _ = h.shape
    (mq, mk, mv, mi, mf, mo, cb, cc, ch, aq, ak, av, gt) = jnp.split(h @ w_in, list(IN_OFFSETS), axis=-1)

    gif = jnp.concatenate([mi, mf], axis=-1).astype(F32) + b_if
    hm, (C1, n1, m1) = mlstm(mq.reshape(Bn, S, M_HEADS, M_DK).astype(F32),
                             mk.reshape(Bn, S, M_HEADS, M_DK).astype(F32),
                             mv.reshape(Bn, S, M_HEADS, M_DV).astype(F32),
                             gif[..., :M_HEADS], gif[..., M_HEADS:], C0, n0, m0)
    o_gate = jax.nn.sigmoid(mo.astype(F32)).reshape(Bn, S, M_HEADS, M_DV)
    hm = (o_gate * head_norm(hm, m_norm_g)).reshape(Bn, S, M_WIDTH).astype(h.dtype)

    u = cc * ch
    u_ext = jnp.concatenate([conv_prev.astype(u.dtype), u], axis=1)
    taps = jnp.stack([u_ext[:, j:j + S] for j in range(CONV_K)], axis=2)
    yc = cb * jnp.einsum('bskc,kc->bsc', taps, conv_w)
    conv_new = u_ext[:, S:]

    qa = rope(aq.reshape(Bn, S, A_HEADS, A_HEAD_DIM), pos)
    ka = rope(ak.reshape(Bn, S, A_HEADS, A_HEAD_DIM), pos)
    va = av.reshape(Bn, S, A_HEADS, A_HEAD_DIM).astype(F32)
    outs, lses, kv_out = [], [], []
    for g, (win, dil) in enumerate(A_GROUPS):
        sl = slice(g * A_HPG, (g + 1) * A_HPG)
        kv_g = jnp.stack([ka[:, :, sl], va[:, :, sl]], axis=2).astype(h.dtype)
        if kv_bufs is None:
            o, l = dilated_attn_prompt(qa[:, :, sl], kv_g, win, dil)
            kv_out.append(kv_g[:, -min(win, S):])
        else:
            o, l = dilated_attn_step(qa[:, :, sl], kv_g, kv_bufs[g], win, dil)
            kv_out.append(kv_g)
        outs.append(o)
        lses.append(l)
    alpha = jax.nn.softmax(jnp.stack(lses), axis=0)
    oa = jnp.einsum('gbsh,gbshd->bshd', alpha, jnp.stack(outs)).reshape(Bn, S, A_OUT).astype(h.dtype)

    gates = jax.nn.sigmoid(gt.astype(F32)).reshape(Bn, S, N_BRANCH, D_MODEL).astype(h.dtype)
    merged = (gates[:, :, 0] * (hm @ w_up_m) + gates[:, :, 1] * (yc @ w_up_c)
              + gates[:, :, 2] * (oa @ w_up_a))
    return merged @ w_o, (C1, n1, m1, conv_new, kv_out[0], kv_out[1], kv_out[2])


def decoder_layer(x, pos, conv_prev, C0, n0, m0, kv_bufs, w_in, b_if, m_norm_g, conv_w,
                  w_up_m, w_up_c, w_up_a, w_o, w_ffn_in, w_ffn_out, ln_g, ln_b):
    x = layer_norm(ALPHA * x + 0.5 * swiglu(x, w_ffn_in[0], w_ffn_out[0]), ln_g[0], ln_b[0])
    mix, st = token_mix(x, pos, conv_prev, C0, n0, m0, kv_bufs, w_in, b_if, m_norm_g, conv_w,
                        w_up_m, w_up_c, w_up_a, w_o)
    x = layer_norm(ALPHA * x + mix, ln_g[1], ln_b[1])
    x = layer_norm(ALPHA * x + 0.5 * swiglu(x, w_ffn_in[1], w_ffn_out[1]), ln_g[2], ln_b[2])
    return x, st


def setup_inputs(seed: int = 0) -> dict:
    key = jax.random.key(seed)
    ks = jax.random.split(key, 24)
    nrm = jax.random.normal

    def kv_cache(k, win):
        return nrm(k, (DEPTH, DEC_BATCH, min(win, PAST_LEN), 2, A_HPG, A_HEAD_DIM), F32)

    b_i = 0.1 * nrm(ks[9], (DEPTH, M_HEADS), F32)
    b_f = jnp.linspace(3.0, 6.0, M_HEADS, dtype=F32)[None, :] + 0.1 * nrm(ks[10], (DEPTH, M_HEADS), F32)
    return {
        'x_prompt': nrm(ks[0], (BATCH, SEQ, D_MODEL), F32),
        'x_sample': nrm(ks[1], (DEC_BATCH, DEC_SEQ, D_MODEL), F32),
        'state_mlstm_C': 0.2 * nrm(ks[2], (DEPTH, DEC_BATCH, M_HEADS, M_DK, M_DV), F32),
        'state_mlstm_n': 0.2 * nrm(ks[3], (DEPTH, DEC_BATCH, M_HEADS, M_DK), F32),
        'state_mlstm_m': 0.5 * nrm(ks[4], (DEPTH, DEC_BATCH, M_HEADS), F32),
        'state_conv': nrm(ks[5], (DEPTH, DEC_BATCH, CONV_K - 1, CONV_WIDTH), F32),
        'cache_attn_kv_w128': kv_cache(ks[6], A_GROUPS[0][0]),
        'cache_attn_kv_w512': kv_cache(ks[7], A_GROUPS[1][0]),
        'cache_attn_kv_w2048': kv_cache(ks[8], A_GROUPS[2][0]),
        'w_in': nrm(ks[11], (DEPTH, D_MODEL, IN_WIDTH), F32) * D_MODEL ** -0.5,
        'b_gate_if': jnp.concatenate([b_i, b_f], axis=-1),
        'mlstm_norm_g': 1.0 + 0.02 * nrm(ks[12], (DEPTH, M_WIDTH), F32),
        'conv_w': nrm(ks[13], (DEPTH, CONV_K, CONV_WIDTH), F32) * CONV_K ** -0.5,
        'w_up_mlstm': nrm(ks[14], (DEPTH, M_WIDTH, D_MODEL), F32) * M_WIDTH ** -0.5,
        'w_up_conv': nrm(ks[15], (DEPTH, CONV_WIDTH, D_MODEL), F32) * CONV_WIDTH ** -0.5,
        'w_up_attn': nrm(ks[16], (DEPTH, A_OUT, D_MODEL), F32) * A_OUT ** -0.5,
        'w_o': nrm(ks[17], (DEPTH, D_MODEL, D_MODEL), F32) * (D_MODEL ** -0.5 * BETA),
        'w_ffn_in': nrm(ks[18], (DEPTH, 2, D_MODEL, 2 * D_FF), F32) * D_MODEL ** -0.5,
        'w_ffn_out': nrm(ks[19], (DEPTH, 2, D_FF, D_MODEL), F32) * (D_FF ** -0.5 * BETA),
        'ln_g': 1.0 + 0.02 * nrm(ks[20], (DEPTH, 3, D_MODEL), F32),
        'ln_b': 0.02 * nrm(ks[21], (DEPTH, 3, D_MODEL), F32),
    }


def reference(x_prompt, x_sample, state_mlstm_C, state_mlstm_n, state_mlstm_m, state_conv,
              cache_attn_kv_w128, cache_attn_kv_w512, cache_attn_kv_w2048,
              w_in, b_gate_if, mlstm_norm_g, conv_w, w_up_mlstm, w_up_conv, w_up_attn, w_o,
              w_ffn_in, w_ffn_out, ln_g, ln_b):
    Bp, Sp, _ = x_prompt.shape
    Ts = x_sample.shape[1]
    pos_p = jnp.arange(Sp)
    pos_s = PAST_LEN + jnp.arange(Ts)
    y_prompt, y_sample = x_prompt, x_sample
    prompt_states, sample_states = [], []
    for l in range(DEPTH):
        weights = (w_in[l], b_gate_if[l], mlstm_norm_g[l], conv_w[l], w_up_mlstm[l], w_up_conv[l],
                   w_up_attn[l], w_o[l], w_ffn_in[l], w_ffn_out[l], ln_g[l], ln_b[l])
        conv0 = jnp.zeros((Bp, CONV_K - 1, CONV_WIDTH), x_prompt.dtype)
        C0 = jnp.zeros((Bp, M_HEADS, M_DK, M_DV), F32)
        n0 = jnp.zeros((Bp, M_HEADS, M_DK), F32)
        m0 = jnp.zeros((Bp, M_HEADS), F32)
        y_prompt, st_p = decoder_layer(y_prompt, pos_p, conv0, C0, n0, m0, None, *weights)
        bufs = (cache_attn_kv_w128[l], cache_attn_kv_w512[l], cache_attn_kv_w2048[l])
        y_sample, st_s = decoder_layer(y_sample, pos_s, state_conv[l], state_mlstm_C[l], state_mlstm_n[l],
                                       state_mlstm_m[l], bufs, *weights)
        prompt_states.append(st_p)
        sample_states.append(st_s)
    p_C, p_n, p_m, p_conv, p_kv128, p_kv512, p_kv2048 = [jnp.stack(z) for z in zip(*prompt_states)]
    s_C, s_n, s_m, s_conv, s_kv128, s_kv512, s_kv2048 = [jnp.stack(z) for z in zip(*sample_states)]
    return (y_prompt, y_sample, p_C, p_n, p_m, p_conv, p_kv128, p_kv512, p_kv2048,
            s_C, s_n, s_m, s_conv, s_kv128, s_kv512, s_kv2048)
```

```python
import functools
import math

import jax
import jax.numpy as jnp
import numpy as np
from jax import lax
from jax.experimental import pallas as pl
from jax.experimental.pallas import tpu as pltpu

F32 = jnp.float32
BF16 = jnp.bfloat16

D_MODEL = 2048
BATCH = 4
SEQ = 2048
DEPTH = 2
DEC_BATCH = 128
PAST_LEN = 2048
M_HEADS = 4
M_DV = 192
M_DK = 96
M_QK = M_HEADS * M_DK
M_WIDTH = M_HEADS * M_DV
CONV_WIDTH = 512
CONV_K = 3
A_GROUPS = ((128, 1), (512, 4), (2048, 16))
A_HPG = 4
A_HEAD_DIM = 64
A_GW = A_HPG * A_HEAD_DIM
A_WIDTH = 3 * A_GW
ROPE_THETA = 10000.0
N_BRANCH = 3
D_FF = 5632
LN_EPS = 1e-5
ALPHA = (2 * DEPTH) ** 0.25
IN_SIZES = (M_QK, M_QK, M_WIDTH, M_HEADS, M_HEADS, M_WIDTH,
            CONV_WIDTH, CONV_WIDTH, CONV_WIDTH,
            A_WIDTH, A_WIDTH, A_WIDTH, N_BRANCH * D_MODEL)
IN_OFFSETS = tuple(int(o) for o in np.cumsum((0,) + IN_SIZES))

M_PROMPT = BATCH * SEQ
M_ROWS = M_PROMPT + DEC_BATCH

LANES = 128
DKP = 128
DVP = 256
VMEM_LIMIT = 52 * 1024 * 1024

C_MV, C_MO = 0, 1024
C_MQ, C_MK = 2048, 2560
C_CB, C_CC, C_CH = 3072, 3584, 4096
C_AQ, C_AK, C_AV = 4608, 5376, 6144
C_IF = 6912
PROJ_W = 7168

NEG = -1e30


def _sigmoid(x):
    return 1.0 / (1.0 + jnp.exp(-x))


def _layer_norm(z, g, b):
    mu = jnp.mean(z, axis=-1, keepdims=True)
    zc = z - mu
    var = jnp.mean(zc * zc, axis=-1, keepdims=True)
    return zc * lax.rsqrt(var + LN_EPS) * g + b


def _cparams(sem):
    return pltpu.CompilerParams(dimension_semantics=sem, vmem_limit_bytes=VMEM_LIMIT)


FFN_TM = 640
FFN_TF = 512


def _ffn_kernel(x_ref, wa_ref, wb_ref, wo_ref, g_ref, b_ref, y_ref, yb_ref, xb_scr):
    f = pl.program_id(1)

    @pl.when(f == 0)
    def _():
        xb_scr[...] = x_ref[...].astype(BF16)
        y_ref[...] = jnp.zeros_like(y_ref)

    xb = xb_scr[...]
    a = jnp.dot(xb, wa_ref[...], preferred_element_type=F32)
    b = jnp.dot(xb, wb_ref[...], preferred_element_type=F32)
    h = (a * _sigmoid(a)) * b
    y_ref[...] += jnp.dot(h.astype(BF16), wo_ref[...], preferred_element_type=F32)

    @pl.when(f == pl.num_programs(1) - 1)
    def _():
        z = ALPHA * x_ref[...] + 0.5 * y_ref[...]
        out = _layer_norm(z, g_ref[...], b_ref[...])
        y_ref[...] = out
        yb_ref[...] = out.astype(BF16)


def ffn_ln(x, w_in, w_out, g, b):
    m = x.shape[0]
    nf = D_FF // FFN_TF
    return pl.pallas_call(
        _ffn_kernel,
        grid=(m // FFN_TM, nf),
        in_specs=[
            pl.BlockSpec((FFN_TM, D_MODEL), lambda i, f: (i, 0)),
            pl.BlockSpec((D_MODEL, FFN_TF), lambda i, f: (0, f)),
            pl.BlockSpec((D_MODEL, FFN_TF), lambda i, f: (0, f + nf)),
            pl.BlockSpec((FFN_TF, D_MODEL), lambda i, f: (f, 0)),
            pl.BlockSpec((1, D_MODEL), lambda i, f: (0, 0)),
            pl.BlockSpec((1, D_MODEL), lambda i, f: (0, 0)),
        ],
        out_specs=[
            pl.BlockSpec((FFN_TM, D_MODEL), lambda i, f: (i, 0)),
            pl.BlockSpec((FFN_TM, D_MODEL), lambda i, f: (i, 0)),
        ],
        out_shape=[jax.ShapeDtypeStruct((m, D_MODEL), F32),
                   jax.ShapeDtypeStruct((m, D_MODEL), BF16)],
        scratch_shapes=[pltpu.VMEM((FFN_TM, D_MODEL), BF16)],
        compiler_params=_cparams(("parallel", "arbitrary")),
        name="ffn_ln",
    )(x, w_in, w_in, w_out, g.reshape(1, D_MODEL), b.reshape(1, D_MODEL))


PROJ_TM = 1040
PROJ_TN = 1024


def _proj_kernel(xb_ref, w_ref, o_ref):
    o_ref[...] = jnp.dot(xb_ref[...], w_ref[...], preferred_element_type=F32)


def branch_proj(xb, w):
    m = xb.shape[0]
    return pl.pallas_call(
        _proj_kernel,
        grid=(m // PROJ_TM, PROJ_W // PROJ_TN),
        in_specs=[pl.BlockSpec((PROJ_TM, D_MODEL), lambda i, j: (i, 0)),
                  pl.BlockSpec((D_MODEL, PROJ_TN), lambda i, j: (0, j))],
        out_specs=pl.BlockSpec((PROJ_TM, PROJ_TN), lambda i, j: (i, j)),
        out_shape=jax.ShapeDtypeStruct((m, PROJ_W), F32),
        compiler_params=_cparams(("parallel", "parallel")),
        name="branch_proj",
    )(xb, w)


MRG_TM = 640
MRG_TN = 256


def _merge_kernel(x_ref, xb_ref, hm_ref, yc_ref, oa_ref, wg0_ref, wg1_ref, wg2_ref,
                  wum_ref, wuc_ref, wua_ref, wo_ref, g_ref, b_ref, y_ref):
    n = pl.program_id(1)

    @pl.when(n == 0)
    def _():
        y_ref[...] = jnp.zeros_like(y_ref)

    xb = xb_ref[...]

    def gated(wg_ref, br_ref, wu_ref):
        gate = _sigmoid(jnp.dot(xb, wg_ref[...], preferred_element_type=F32))
        return gate * jnp.dot(br_ref[...], wu_ref[...], preferred_element_type=F32)

    merged = (gated(wg0_ref, hm_ref, wum_ref) + gated(wg1_ref, yc_ref, wuc_ref)
              + gated(wg2_ref, oa_ref, wua_ref))
    y_ref[...] += jnp.dot(merged.astype(BF16), wo_ref[...], preferred_element_type=F32)

    @pl.when(n == pl.num_programs(1) - 1)
    def _():
        z = ALPHA * x_ref[...] + y_ref[...]
        y_ref[...] = _layer_norm(z, g_ref[...], b_ref[...])


def merge_ln(x, xb, hm, yc, oa, wg, wum, wuc, wua, wo, g, b):
    m = x.shape[0]
    nn = D_MODEL // MRG_TN
    row = lambda w: pl.BlockSpec((MRG_TM, w), lambda i, n: (i, 0))
    return pl.pallas_call(
        _merge_kernel,
        grid=(m // MRG_TM, nn),
        in_specs=[
            row(D_MODEL), row(D_MODEL), row(M_HEADS * DVP), row(CONV_WIDTH), row(A_GW),
            pl.BlockSpec((D_MODEL, MRG_TN), lambda i, n: (0, n)),
            pl.BlockSpec((D_MODEL, MRG_TN), lambda i, n: (0, n + nn)),
            pl.BlockSpec((D_MODEL, MRG_TN), lambda i, n: (0, n + 2 * nn)),
            pl.BlockSpec((M_HEADS * DVP, MRG_TN), lambda i, n: (0, n)),
            pl.BlockSpec((CONV_WIDTH, MRG_TN), lambda i, n: (0, n)),
            pl.BlockSpec((A_GW, MRG_TN), lambda i, n: (0, n)),
            pl.BlockSpec((MRG_TN, D_MODEL), lambda i, n: (n, 0)),
            pl.BlockSpec((1, D_MODEL), lambda i, n: (0, 0)),
            pl.BlockSpec((1, D_MODEL), lambda i, n: (0, 0)),
        ],
        out_specs=pl.BlockSpec((MRG_TM, D_MODEL), lambda i, n: (i, 0)),
        out_shape=jax.ShapeDtypeStruct((m, D_MODEL), F32),
        compiler_params=_cparams(("parallel", "arbitrary")),
        name="merge_ln",
    )(x, xb, hm, yc, oa, wg, wg, wg, wum, wuc, wua, wo,
      g.reshape(1, D_MODEL), b.reshape(1, D_MODEL))


def _conv_prompt_kernel(cb_ref, cc_ref, ch_ref, w_ref, y_ref, st_ref, u_scr):
    u = cc_ref[...] * ch_ref[...]
    u_scr[pl.ds(0, 8), :] = jnp.zeros((8, CONV_WIDTH), F32)
    u_scr[pl.ds(8, SEQ), :] = u
    w = w_ref[...]
    acc = (w[0:1, :] * u_scr[pl.ds(6, SEQ), :] + w[1:2, :] * u_scr[pl.ds(7, SEQ), :]
           + w[2:3, :] * u)
    y_ref[...] = (cb_ref[...] * acc).astype(BF16)
    st_ref[...] = u_scr[pl.ds(8 + SEQ - (CONV_K - 1), CONV_K - 1), :]


def conv_prompt(proj, conv_w):
    blk = lambda c: pl.BlockSpec((SEQ, CONV_WIDTH), lambda b, c=c: (b, c // CONV_WIDTH))
    return pl.pallas_call(
        _conv_prompt_kernel,
        grid=(BATCH,),
        in_specs=[blk(C_CB), blk(C_CC), blk(C_CH),
                  pl.BlockSpec((CONV_K, CONV_WIDTH), lambda b: (0, 0))],
        out_specs=[pl.BlockSpec((SEQ, CONV_WIDTH), lambda b: (b, 0)),
                   pl.BlockSpec((None, CONV_K - 1, CONV_WIDTH), lambda b: (b, 0, 0))],
        out_shape=[jax.ShapeDtypeStruct((M_PROMPT, CONV_WIDTH), BF16),
                   jax.ShapeDtypeStruct((BATCH, CONV_K - 1, CONV_WIDTH), F32)],
        scratch_shapes=[pltpu.VMEM((SEQ + 8, CONV_WIDTH), F32)],
        compiler_params=_cparams(("parallel",)),
        name="conv_prompt",
    )(proj, proj, proj, conv_w)


def _conv_sample_kernel(cb_ref, cc_ref, ch_ref, prev_ref, w_ref, y_ref, st_ref):
    u = cc_ref[...] * ch_ref[...]
    w = w_ref[...]
    p0 = prev_ref[:, 0, :]
    p1 = prev_ref[:, 1, :]
    acc = w[0:1, :] * p0 + w[1:2, :] * p1 + w[2:3, :] * u
    y_ref[...] = (cb_ref[...] * acc).astype(BF16)
    st_ref[:, 0, :] = p1
    st_ref[:, 1, :] = u


def conv_sample(proj, prev, conv_w):
    rb = M_PROMPT // DEC_BATCH
    blk = lambda c: pl.BlockSpec((DEC_BATCH, CONV_WIDTH), lambda i, c=c: (rb, c // CONV_WIDTH))
    full3 = pl.BlockSpec((DEC_BATCH, CONV_K - 1, CONV_WIDTH), lambda i: (0, 0, 0))
    return pl.pallas_call(
        _conv_sample_kernel,
        grid=(1,),
        in_specs=[blk(C_CB), blk(C_CC), blk(C_CH), full3,
                  pl.BlockSpec((CONV_K, CONV_WIDTH), lambda i: (0, 0))],
        out_specs=[pl.BlockSpec((DEC_BATCH, CONV_WIDTH), lambda i: (0, 0)), full3],
        out_shape=[jax.ShapeDtypeStruct((DEC_BATCH, CONV_WIDTH), BF16),
                   jax.ShapeDtypeStruct((DEC_BATCH, CONV_K - 1, CONV_WIDTH), F32)],
        compiler_params=_cparams(("arbitrary",)),
        name="conv_sample",
    )(proj, proj, proj, prev, conv_w)


M_L = 128


def _log_sigmoid(x):
    return jnp.minimum(x, 0.0) - jnp.log1p(jnp.exp(-jnp.abs(x)))


def _head_norm_gate(h, o_pre, gain):
    lane = lax.broadcasted_iota(jnp.int32, h.shape, 1)
    real = lane < M_DV
    mu = jnp.sum(h, axis=-1, keepdims=True) * (1.0 / M_DV)
    hc = jnp.where(real, h - mu, 0.0)
    var = jnp.sum(hc * hc, axis=-1, keepdims=True) * (1.0 / M_DV)
    return _sigmoid(o_pre) * (hc * lax.rsqrt(var + LN_EPS) * gain)


def _mlstm_prompt_kernel(q_ref, k_ref, v_ref, o_ref, if_ref, bias_ref, gain_ref,
                         hm_ref, c_out_ref, n_out_ref, m_out_ref, c_scr, n_scr, m_scr):
    hd = pl.program_id(1)
    c_scr[...] = jnp.zeros_like(c_scr)
    n_scr[...] = jnp.zeros_like(n_scr)
    m_scr[...] = jnp.zeros_like(m_scr)

    row = lax.broadcasted_iota(jnp.int32, (M_L, M_L), 0)
    col = lax.broadcasted_iota(jnp.int32, (M_L, M_L), 1)
    causal = col <= row
    tri = causal.astype(F32)
    bias = bias_ref[...]
    gain = gain_ref[...]

    def chunk(c, carry):
        r0 = pl.multiple_of(c * M_L, M_L)
        x_if = if_ref[pl.ds(r0, M_L), :] + bias
        log_f = _log_sigmoid(x_if)
        cs = jnp.dot(tri, log_f, preferred_element_type=F32, precision=lax.Precision.HIGHEST)
        b_col = jnp.sum(jnp.where(col == hd + M_HEADS, cs, 0.0), axis=1, keepdims=True)
        i_col = jnp.sum(jnp.where(col == hd, x_if, 0.0), axis=1, keepdims=True)
        zt = jnp.where(col == 0, b_col, jnp.where(col == 1, i_col, 0.0)).T
        b_row = zt[0:1, :]
        i_row = zt[1:2, :]
        m_prev = m_scr[...]

        d = jnp.where(causal, b_col - b_row + i_row, NEG)
        inter = b_col + m_prev
        m_t = jnp.maximum(jnp.max(d, axis=1, keepdims=True), inter)
        dmat = jnp.exp(d - m_t)
        q = q_ref[pl.ds(r0, M_L), :]
        k = k_ref[pl.ds(r0, M_L), :] * (M_DK ** -0.5)
        vb = v_ref[pl.ds(r0, M_L), :].astype(BF16)
        qb = q.astype(BF16)
        s = lax.dot_general(qb, k.astype(BF16), (((1,), (1,)), ((), ())),
                            preferred_element_type=F32) * dmat
        w_inter = jnp.exp(inter - m_t)
        c_prev = c_scr[...]
        n_prev = n_scr[...]
        num = (jnp.dot(s.astype(BF16), vb, preferred_element_type=F32)
               + w_inter * jnp.dot(qb, c_prev.astype(BF16), preferred_element_type=F32))
        den = (jnp.sum(s, axis=1, keepdims=True)
               + w_inter * jnp.sum(q * n_prev, axis=1, keepdims=True))
        h = num / jnp.maximum(jnp.abs(den), jnp.exp(-m_t))
        hm_ref[pl.ds(r0, M_L), :] = _head_norm_gate(h, o_ref[pl.ds(r0, M_L), :], gain).astype(BF16)

        b_last = b_row[:, M_L - 1:M_L]
        g_row = b_last - b_row + i_row
        m_new = jnp.maximum(b_last + m_prev, jnp.max(g_row, axis=1, keepdims=True))
        w_k = jnp.exp(b_last - b_col + i_col - m_new)
        decay = jnp.exp(b_last + m_prev - m_new)
        kw = k * w_k
        c_scr[...] = decay * c_prev + lax.dot_general(
            kw.astype(BF16), vb, (((0,), (0,)), ((), ())), preferred_element_type=F32)
        n_scr[...] = decay * n_prev + jnp.sum(kw, axis=0, keepdims=True)
        m_scr[...] = m_new
        return carry

    lax.fori_loop(0, SEQ // M_L, chunk, 0)
    c_out_ref[...] = c_scr[pl.ds(0, M_DK), pl.ds(0, M_DV)]
    n_out_ref[...] = n_scr[:, pl.ds(0, M_DK)]
    m_out_ref[...] = m_scr[...]


def mlstm_prompt(proj, bias, gain):
    def cblk(c0, w):
        return pl.BlockSpec((SEQ, w), lambda b, h: (b, c0 // w + h))
    return pl.pallas_call(
        _mlstm_prompt_kernel,
        grid=(BATCH, M_HEADS),
        in_specs=[cblk(C_MQ, DKP), cblk(C_MK, DKP), cblk(C_MV, DVP), cblk(C_MO, DVP),
                  pl.BlockSpec((SEQ, LANES), lambda b, h: (b, C_IF // LANES)),
                  pl.BlockSpec((1, LANES), lambda b, h: (0, 0)),
                  pl.BlockSpec((None, 1, DVP), lambda b, h: (h, 0, 0))],
        out_specs=[pl.BlockSpec((SEQ, DVP), lambda b, h: (b, h)),
                   pl.BlockSpec((None, None, M_DK, M_DV), lambda b, h: (b, h, 0, 0)),
                   pl.BlockSpec((None, None, 1, M_DK), lambda b, h: (b, h, 0, 0)),
                   pl.BlockSpec((None, None, 1, 1), lambda b, h: (b, h, 0, 0))],
        out_shape=[jax.ShapeDtypeStruct((M_PROMPT, M_HEADS * DVP), BF16),
                   jax.ShapeDtypeStruct((BATCH, M_HEADS, M_DK, M_DV), F32),
                   jax.ShapeDtypeStruct((BATCH, M_HEADS, 1, M_DK), F32),
                   jax.ShapeDtypeStruct((BATCH, M_HEADS, 1, 1), F32)],
        scratch_shapes=[pltpu.VMEM((DKP, DVP), F32), pltpu.VMEM((1, DKP), F32),
                        pltpu.VMEM((1, 1), F32)],
        compiler_params=_cparams(("parallel", "parallel")),
        name="mlstm_prompt",
    )(proj, proj, proj, proj, proj, bias, gain)


MS_BB = 8


def _mlstm_sample_kernel(q_ref, k_ref, v_ref, o_ref, if_ref, bias_ref, gain_ref,
                         c0_ref, n0_ref, m0_ref,
                         hm_ref, c_out_ref, n_out_ref, m_out_ref):
    i = pl.program_id(0)
    lane = lax.broadcasted_iota(jnp.int32, (DKP, LANES), 1)
    sub = lax.broadcasted_iota(jnp.int32, (MS_BB, M_DV), 0)
    gain = gain_ref[...]
    bias = bias_ref[...]
    r0 = pl.multiple_of(i * MS_BB, MS_BB)
    hm_ref[...] = jnp.zeros_like(hm_ref)

    for hd in range(M_HEADS):
        qt = q_ref[:, pl.ds(hd * DKP, DKP)].T
        kt = (k_ref[:, pl.ds(hd * DKP, DKP)] * (M_DK ** -0.5)).T

        if8 = if_ref[pl.ds(r0, MS_BB), :] + bias
        v8 = v_ref[pl.ds(r0, MS_BB), pl.ds(hd * DVP, DVP)]
        q8 = q_ref[pl.ds(r0, MS_BB), pl.ds(hd * DKP, DKP)]
        k8 = k_ref[pl.ds(r0, MS_BB), pl.ds(hd * DKP, DKP)] * (M_DK ** -0.5)

        def pick_row(x8, j):
            rows = lax.broadcasted_iota(jnp.int32, x8.shape, 0)
            return jnp.sum(jnp.where(rows == j, x8, 0.0), axis=0, keepdims=True)

        def per_row(j, h_acc, hd=hd, qt=qt, kt=kt, if8=if8, v8=v8, q8=q8, k8=k8):
            bg = i * MS_BB + j
            pick = lane == bg
            q_col = jnp.sum(jnp.where(pick, qt, 0.0), axis=1, keepdims=True)[:M_DK]
            k_col = jnp.sum(jnp.where(pick, kt, 0.0), axis=1, keepdims=True)[:M_DK]
            x_if = pick_row(if8, j)
            i_pre = x_if[:, hd:hd + 1]
            log_f = _log_sigmoid(x_if[:, M_HEADS + hd:M_HEADS + hd + 1])
            m_prev = m0_ref[j, hd:hd + 1, :]
            c_prev = c0_ref[j, hd]
            n_prev = n0_ref[j, hd:hd + 1, :]
            v_row = pick_row(v8, j)[:, :M_DV]
            q_row = pick_row(q8, j)[:, :M_DK]
            k_row = pick_row(k8, j)[:, :M_DK]

            inter = log_f + m_prev
            m_t = jnp.maximum(i_pre, inter)
            qk = jnp.sum(q_col * k_col, axis=0, keepdims=True)
            s = qk * jnp.exp(i_pre - m_t)
            w_inter = jnp.exp(inter - m_t)
            q_c = jnp.sum(q_col * c_prev, axis=0, keepdims=True)
            q_n = jnp.sum(q_row * n_prev, axis=1, keepdims=True)
            num = s * v_row + w_inter * q_c
            den = s + w_inter * q_n
            h = num / jnp.maximum(jnp.abs(den), jnp.exp(-m_t))

            m_new = jnp.maximum(inter, i_pre)
            w_k = jnp.exp(i_pre - m_new)
            decay = jnp.exp(inter - m_new)
            c_out_ref[j, hd] = decay * c_prev + (w_k * k_col) * v_row
            n_out_ref[j, hd:hd + 1, :] = decay * n_prev + w_k * k_row
            m_out_ref[j, hd:hd + 1, :] = m_new
            return jnp.where(sub == j, h, h_acc)

        h_all = lax.fori_loop(0, MS_BB, per_row, jnp.zeros((MS_BB, M_DV), F32))
        mu = jnp.mean(h_all, axis=-1, keepdims=True)
        hc = h_all - mu
        var = jnp.mean(hc * hc, axis=-1, keepdims=True)
        o_pre = o_ref[pl.ds(r0, MS_BB), pl.ds(hd * DVP, DVP)][:, :M_DV]
        hm_ref[:, pl.ds(hd * DVP, M_DV)] = (
            _sigmoid(o_pre) * (hc * lax.rsqrt(var + LN_EPS) * gain[hd:hd + 1, :M_DV]))


def mlstm_sample(proj, bias, gain, c0, n0, m0, layer):
    rb = M_PROMPT // DEC_BATCH
    full = lambda c0_, w: pl.BlockSpec((DEC_BATCH, w), lambda i: (rb, c0_ // w))
    return pl.pallas_call(
        _mlstm_sample_kernel,
        grid=(DEC_BATCH // MS_BB,),
        in_specs=[full(C_MQ, M_HEADS * DKP), full(C_MK, M_HEADS * DKP),
                  full(C_MV, M_HEADS * DVP), full(C_MO, M_HEADS * DVP),
                  pl.BlockSpec((DEC_BATCH, LANES), lambda i: (rb, C_IF // LANES)),
                  pl.BlockSpec((1, LANES), lambda i: (0, 0)),
                  pl.BlockSpec((M_HEADS, DVP), lambda i: (0, 0)),
                  pl.BlockSpec((None, MS_BB, M_HEADS, M_DK, M_DV), lambda i: (layer, i, 0, 0, 0)),
                  pl.BlockSpec((None, MS_BB, M_HEADS, M_DK), lambda i: (layer, i, 0, 0)),
                  pl.BlockSpec((None, MS_BB, M_HEADS, 1), lambda i: (layer, i, 0, 0))],
        out_specs=[pl.BlockSpec((MS_BB, M_HEADS * DVP), lambda i: (i, 0)),
                   pl.BlockSpec((MS_BB, M_HEADS, M_DK, M_DV), lambda i: (i, 0, 0, 0)),
                   pl.BlockSpec((MS_BB, M_HEADS, M_DK), lambda i: (i, 0, 0)),
                   pl.BlockSpec((MS_BB, M_HEADS, 1), lambda i: (i, 0, 0))],
        out_shape=[jax.ShapeDtypeStruct((DEC_BATCH, M_HEADS * DVP), F32),
                   jax.ShapeDtypeStruct((DEC_BATCH, M_HEADS, M_DK, M_DV), F32),
                   jax.ShapeDtypeStruct((DEC_BATCH, M_HEADS, M_DK), F32),
                   jax.ShapeDtypeStruct((DEC_BATCH, M_HEADS, 1), F32)],
        compiler_params=_cparams(("arbitrary",)),
        name="mlstm_sample",
    )(proj, proj, proj, proj, proj, bias, gain, c0, n0, m0.reshape(DEPTH, DEC_BATCH, M_HEADS, 1))


A_Q = 128
A_LT = A_GW // LANES


def _rope(x, cos, sin_signed):
    lane = lax.broadcasted_iota(jnp.int32, x.shape, 1)
    first_half = (lane % A_HEAD_DIM) < (A_HEAD_DIM // 2)
    partner = jnp.where(first_half, pltpu.roll(x, A_GW - A_HEAD_DIM // 2, 1),
                        pltpu.roll(x, A_HEAD_DIM // 2, 1))
    return x * cos + partner * sin_signed


def _head_masks(shape):
    lane = lax.broadcasted_iota(jnp.int32, shape, 1)
    return [(lane // A_HEAD_DIM) == h for h in range(A_HPG)]


def _attn_group_prompt(dil, gi, qs_scr, ks_scr, vs_scr, o_scr, l_scr):
    length = SEQ // dil
    nb = length // A_Q
    row = lax.broadcasted_iota(jnp.int32, (A_Q, A_Q), 0)
    col = lax.broadcasted_iota(jnp.int32, (A_Q, A_Q), 1)
    cur_ok = col <= row
    prev_ok = col >= row
    masks = _head_masks((A_Q, A_GW))
    nt = (((1,), (1,)), ((), ()))

    def window(start):
        if dil == 1:
            return pl.ds(pl.multiple_of(start, A_Q), A_Q)
        return pl.ds(start, A_Q, stride=dil)

    def rows(scr, start):
        w = window(start)
        return jnp.concatenate([scr[t, w, :] for t in range(A_LT)], axis=1).astype(BF16)

    def block(idx, carry):
        r = idx % dil
        n = idx // dil
        base = r + (dil * A_Q) * n
        qb = rows(qs_scr, base)
        kc = rows(ks_scr, base)
        vc = rows(vs_scr, base)
        if nb > 1:
            pbase = jnp.maximum(base - dil * A_Q, r)
            kp = rows(ks_scr, pbase)
            vp = rows(vs_scr, pbase)
            has_prev = n > 0
        o_acc = jnp.zeros((A_Q, A_GW), F32)
        l_acc = jnp.zeros((A_Q, A_GW), F32)
        for h in range(A_HPG):
            qh = jnp.where(masks[h], qb, jnp.zeros_like(qb))
            s_c = jnp.where(cur_ok, lax.dot_general(qh, kc, nt, preferred_element_type=F32), NEG)
            m = jnp.max(s_c, axis=1, keepdims=True)
            if nb > 1:
                s_p = jnp.where(jnp.logical_and(prev_ok, has_prev),
                                lax.dot_general(qh, kp, nt, preferred_element_type=F32), NEG)
                m = jnp.maximum(m, jnp.max(s_p, axis=1, keepdims=True))
            p_c = jnp.exp(s_c - m)
            l = jnp.sum(p_c, axis=1, keepdims=True)
            o_h = jnp.dot(p_c.astype(BF16), vc, preferred_element_type=F32)
            if nb > 1:
                p_p = jnp.exp(s_p - m)
                l = l + jnp.sum(p_p, axis=1, keepdims=True)
                o_h = o_h + jnp.dot(p_p.astype(BF16), vp, preferred_element_type=F32)
            o_acc = o_acc + jnp.where(masks[h], o_h / l, 0.0)
            l_acc = l_acc + jnp.where(masks[h], m + jnp.log(l), 0.0)
        w = window(base)
        for t in range(A_LT):
            o_scr[gi, t, w, :] = o_acc[:, t * LANES:(t + 1) * LANES]
            l_scr[gi, t, w, :] = l_acc[:, t * LANES:(t + 1) * LANES]
        return carry

    lax.fori_loop(0, dil * nb, block, 0)


def _attn_prompt_kernel(q_ref, k_ref, v_ref, cos_ref, sin_ref, oa_ref, kr_ref,
                        qs_scr, ks_scr, vs_scr, o_scr, l_scr):
    g = pl.program_id(1)
    rc = 256

    def rope_rows(c, carry):
        sl = pl.ds(pl.multiple_of(c * rc, rc), rc)
        cos = cos_ref[sl, :]
        sin = sin_ref[sl, :]
        qr = _rope(q_ref[sl, :], cos, sin) * (A_HEAD_DIM ** -0.5)
        kr = _rope(k_ref[sl, :], cos, sin)
        v = v_ref[sl, :]
        kr_ref[sl, :] = kr
        for t in range(A_LT):
            lanes = slice(t * LANES, (t + 1) * LANES)
            qs_scr[t, sl, :] = qr[:, lanes]
            ks_scr[t, sl, :] = kr[:, lanes]
            vs_scr[t, sl, :] = v[:, lanes]
        return carry

    lax.fori_loop(0, SEQ // rc, rope_rows, 0)

    for gi, (_, dil) in enumerate(A_GROUPS):
        @pl.when(g == gi)
        def _(gi=gi, dil=dil):
            _attn_group_prompt(dil, gi, qs_scr, ks_scr, vs_scr, o_scr, l_scr)

    @pl.when(g == len(A_GROUPS) - 1)
    def _():
        def comb(c, carry):
            sl = pl.ds(pl.multiple_of(c * rc, rc), rc)
            for t in range(A_LT):
                l0, l1, l2 = l_scr[0, t, sl, :], l_scr[1, t, sl, :], l_scr[2, t, sl, :]
                mx = jnp.maximum(jnp.maximum(l0, l1), l2)
                e0, e1, e2 = jnp.exp(l0 - mx), jnp.exp(l1 - mx), jnp.exp(l2 - mx)
                tot = e0 * o_scr[0, t, sl, :] + e1 * o_scr[1, t, sl, :] + e2 * o_scr[2, t, sl, :]
                oa_ref[sl, pl.ds(t * LANES, LANES)] = (tot / (e0 + e1 + e2)).astype(BF16)
            return carry
        lax.fori_loop(0, SEQ // rc, comb, 0)


def attn_prompt(proj, cos, sin):
    def gblk(c0):
        return pl.BlockSpec((SEQ, A_GW), lambda b, g: (b, c0 // A_GW + g))
    tab = pl.BlockSpec((SEQ, A_GW), lambda b, g: (0, 0))
    return pl.pallas_call(
        _attn_prompt_kernel,
        grid=(BATCH, len(A_GROUPS)),
        in_specs=[gblk(C_AQ), gblk(C_AK), gblk(C_AV), tab, tab],
        out_specs=[pl.BlockSpec((SEQ, A_GW), lambda b, g: (b, 0)),
                   pl.BlockSpec((SEQ, A_GW), lambda b, g: (b, g))],
        out_shape=[jax.ShapeDtypeStruct((M_PROMPT, A_GW), BF16),
                   jax.ShapeDtypeStruct((M_PROMPT, A_WIDTH), F32)],
        scratch_shapes=[pltpu.VMEM((A_LT, SEQ, LANES), F32),
                        pltpu.VMEM((A_LT, SEQ, LANES), F32),
                        pltpu.VMEM((A_LT, SEQ, LANES), F32),
                        pltpu.VMEM((len(A_GROUPS), A_LT, SEQ, LANES), F32),
                        pltpu.VMEM((len(A_GROUPS), A_LT, SEQ, LANES), F32)],
        compiler_params=_cparams(("parallel", "arbitrary")),
        name="attn_prompt",
    )(proj, proj, proj, cos, sin)


AS_BB = 16


def _attn_sample_kernel(q_ref, k_ref, v_ref, cos_ref, sin_ref, c0_ref, c1_ref, c2_ref,
                        oa_ref, kr_ref, o_scr, l_scr):
    cos = cos_ref[...]
    sin = sin_ref[...]
    sub = lax.broadcasted_iota(jnp.int32, (8, A_GW), 0)
    lane_head = lax.broadcasted_iota(jnp.int32, (8, A_GW), 1) // A_HEAD_DIM
    diag = sub == lane_head
    nt = (((1,), (1,)), ((), ()))

    for gi, cache_ref in enumerate((c0_ref, c1_ref, c2_ref)):
        gs = pl.ds(gi * A_GW, A_GW)
        qr = _rope(q_ref[:, gs], cos, sin) * (A_HEAD_DIM ** -0.5)
        kr = _rope(k_ref[:, gs], cos, sin)
        kr_ref[:, gs] = kr
        o_scr[gi] = qr
        l_scr[gi] = kr

        def per_row(j, carry, gi=gi, cache_ref=cache_ref, gs=gs):
            q_row = o_scr[gi, pl.ds(j, 1), :]
            k_new = l_scr[gi, pl.ds(j, 1), :]
            v_new = v_ref[pl.ds(j, 1), gs]
            qx = jnp.where(diag, q_row, 0.0)
            kc = cache_ref[j, :, pl.ds(0, A_GW)].astype(BF16)
            vc = cache_ref[j, :, pl.ds(A_GW, A_GW)].astype(BF16)
            s = lax.dot_general(qx.astype(BF16), kc, nt, preferred_element_type=F32)
            s0 = jnp.sum(qx * k_new, axis=1, keepdims=True)
            m = jnp.maximum(jnp.max(s, axis=1, keepdims=True), s0)
            p = jnp.exp(s - m)
            p0 = jnp.exp(s0 - m)
            l = jnp.sum(p, axis=1, keepdims=True) + p0
            o = (jnp.dot(p.astype(BF16), vc, preferred_element_type=F32) + p0 * v_new) / l
            lse = m + jnp.log(l)
            o_scr[gi, pl.ds(j, 1), :] = jnp.sum(jnp.where(diag, o, 0.0), axis=0, keepdims=True)
            l_scr[gi, pl.ds(j, 1), :] = jnp.sum(jnp.where(diag, lse, 0.0), axis=0, keepdims=True)
            return carry

        lax.fori_loop(0, AS_BB, per_row, 0)

    l0, l1, l2 = l_scr[0], l_scr[1], l_scr[2]
    mx = jnp.maximum(jnp.maximum(l0, l1), l2)
    e0, e1, e2 = jnp.exp(l0 - mx), jnp.exp(l1 - mx), jnp.exp(l2 - mx)
    tot = e0 * o_scr[0] + e1 * o_scr[1] + e2 * o_scr[2]
    oa_ref[...] = (tot / (e0 + e1 + e2)).astype(BF16)


def attn_sample(proj, cos, sin, caches, layer):
    rb = M_PROMPT // AS_BB
    def pblk(c0):
        return pl.BlockSpec((AS_BB, A_WIDTH), lambda i: (rb + i, c0 // A_WIDTH))
    tab = pl.BlockSpec((1, A_GW), lambda i: (0, 0))
    cache_specs = [pl.BlockSpec((None, AS_BB, A_Q, 2 * A_GW), lambda i: (layer, i, 0, 0))
                   for _ in caches]
    return pl.pallas_call(
        _attn_sample_kernel,
        grid=(DEC_BATCH // AS_BB,),
        in_specs=[pblk(C_AQ), pblk(C_AK), pblk(C_AV), tab, tab] + cache_specs,
        out_specs=[pl.BlockSpec((AS_BB, A_GW), lambda i: (i, 0)),
                   pl.BlockSpec((AS_BB, A_WIDTH), lambda i: (i, 0))],
        out_shape=[jax.ShapeDtypeStruct((DEC_BATCH, A_GW), BF16),
                   jax.ShapeDtypeStruct((DEC_BATCH, A_WIDTH), F32)],
        scratch_shapes=[pltpu.VMEM((len(A_GROUPS), AS_BB, A_GW), F32),
                        pltpu.VMEM((len(A_GROUPS), AS_BB, A_GW), F32)],
        compiler_params=_cparams(("parallel",)),
        name="attn_sample",
    )(proj, proj, proj, cos, sin, *caches)


def _pad_heads(w, d, dp):
    r = w.shape[0]
    w = w.reshape(r, M_HEADS, d)
    return jnp.pad(w, ((0, 0), (0, 0), (0, dp - d))).reshape(r, M_HEADS * dp)


def _layer_weights(w_in_l, w_up_m, w_up_c, w_up_a, w_o_l):
    o = IN_OFFSETS
    piece = lambda i: w_in_l[:, o[i]:o[i + 1]]
    mq, mk, mv, mi, mf, mo, cb, cc, ch, aq, ak, av, gt = [piece(i) for i in range(13)]
    w_br = jnp.concatenate(
        [_pad_heads(mv, M_DV, DVP), _pad_heads(mo, M_DV, DVP),
         _pad_heads(mq, M_DK, DKP), _pad_heads(mk, M_DK, DKP),
         cb, cc, ch, aq, ak, av,
         mi, mf,jnp.zeros((D_MODEL, PROJ_W - C_IF - 2 * M_HEADS), F32)], axis=1).astype(BF16)
    w_um = jnp.pad(w_up_m.reshape(M_HEADS, M_DV, D_MODEL),
                   ((0, 0), (0, DVP - M_DV), (0, 0))).reshape(M_HEADS * DVP, D_MODEL)
    return (w_br, gt.astype(BF16), w_um.astype(BF16), w_up_c.astype(BF16),
            w_up_a.astype(BF16), w_o_l.astype(BF16))


def _rope_tables(pos):
    half = A_HEAD_DIM // 2
    inv = ROPE_THETA ** (-(2.0 * jnp.arange(half, dtype=F32)) / A_HEAD_DIM)
    ang = pos.astype(F32)[:, None] * inv[None, :]
    cos = jnp.cos(ang)
    sin = jnp.sin(ang)
    cos = jnp.tile(jnp.concatenate([cos, cos], axis=-1), (1, A_HPG))
    sin = jnp.tile(jnp.concatenate([-sin, sin], axis=-1), (1, A_HPG))
    return cos, sin


def kernel(x_prompt, x_sample, state_mlstm_C, state_mlstm_n, state_mlstm_m, state_conv,
           cache_attn_kv_w128, cache_attn_kv_w512, cache_attn_kv_w2048,
           w_in, b_gate_if, mlstm_norm_g, conv_w, w_up_mlstm, w_up_conv, w_up_attn, w_o,
           w_ffn_in, w_ffn_out, ln_g, ln_b):
    x = jnp.concatenate([x_prompt.reshape(M_PROMPT, D_MODEL),
                         x_sample.reshape(DEC_BATCH, D_MODEL)], axis=0)
    cos_p, sin_p = _rope_tables(jnp.arange(SEQ))
    cos_s, sin_s = _rope_tables(PAST_LEN + jnp.arange(1))

    p_states, s_states = [], []
    for l in range(DEPTH):
        w_br, w_g, w_um, w_uc, w_ua, w_ol = _layer_weights(
            w_in[l], w_up_mlstm[l], w_up_conv[l], w_up_attn[l], w_o[l])
        bias = jnp.pad(b_gate_if[l], (0, LANES - 2 * M_HEADS)).reshape(1, LANES)
        gain = jnp.pad(mlstm_norm_g[l].reshape(M_HEADS, M_DV), ((0, 0), (0, DVP - M_DV)))

        x, xb = ffn_ln(x, w_ffn_in[l, 0].astype(BF16), w_ffn_out[l, 0].astype(BF16),
                       ln_g[l, 0], ln_b[l, 0])
        proj = branch_proj(xb, w_br)

        hm_p, pc, pn, pm = mlstm_prompt(proj, bias, gain.reshape(M_HEADS, 1, DVP))
        hm_s, sc, sn, sm = mlstm_sample(proj, bias, gain, state_mlstm_C, state_mlstm_n,
                                        state_mlstm_m, l)
        yc_p, pconv = conv_prompt(proj, conv_w[l])
        yc_s, sconv = conv_sample(proj, state_conv[l], conv_w[l])
        oa_p, kr_p = attn_prompt(proj, cos_p, sin_p)
        caches = []
        for cache, (win, dil) in zip((cache_attn_kv_w128, cache_attn_kv_w512, cache_attn_kv_w2048),
                                     A_GROUPS):
            wb = cache.shape[2]
            caches.append(cache.reshape(DEPTH, DEC_BATCH, wb // dil, dil * 2 * A_GW))
        oa_s, kr_s = attn_sample(proj, cos_s, sin_s, caches, l)

        hm = jnp.concatenate([hm_p, hm_s.astype(BF16)], axis=0)
        yc = jnp.concatenate([yc_p, yc_s], axis=0)
        oa = jnp.concatenate([oa_p, oa_s], axis=0)
        x = merge_ln(x, xb, hm, yc, oa, w_g, w_um, w_uc, w_ua, w_ol, ln_g[l, 1], ln_b[l, 1])
        x, _ = ffn_ln(x, w_ffn_in[l, 1].astype(BF16), w_ffn_out[l, 1].astype(BF16),
                      ln_g[l, 2], ln_b[l, 2])

        v_p = proj[:M_PROMPT, C_AV:C_AV + A_WIDTH].reshape(BATCH, SEQ, 3, A_HPG, A_HEAD_DIM)
        k_p = kr_p.reshape(BATCH, SEQ, 3, A_HPG, A_HEAD_DIM)
        v_s = proj[M_PROMPT:, C_AV:C_AV + A_WIDTH].reshape(DEC_BATCH, 1, 3, A_HPG, A_HEAD_DIM)
        k_s = kr_s.reshape(DEC_BATCH, 1, 3, A_HPG, A_HEAD_DIM)
        kv_p, kv_s = [], []
        for gi, (win, _) in enumerate(A_GROUPS):
            keep = min(win, SEQ)
            kv_p.append(jnp.stack([k_p[:, SEQ - keep:, gi], v_p[:, SEQ - keep:, gi]], axis=2))
            kv_s.append(jnp.stack([k_s[:, :, gi], v_s[:, :, gi]], axis=2))
        p_states.append((pc, pn.reshape(BATCH, M_HEADS, M_DK), pm.reshape(BATCH, M_HEADS), pconv,
                         kv_p[0], kv_p[1], kv_p[2]))
        s_states.append((sc, sn, sm.reshape(DEC_BATCH, M_HEADS), sconv, kv_s[0], kv_s[1], kv_s[2]))

    y_prompt = x[:M_PROMPT].reshape(BATCH, SEQ, D_MODEL)
    y_sample = x[M_PROMPT:].reshape(DEC_BATCH, 1, D_MODEL)
    p_out = [jnp.stack(z) for z in zip(*p_states)]
    s_out = [jnp.stack(z) for z in zip(*s_states)]
    return (y_prompt, y_sample, *p_out, *s_out)
```

```python
import functools
import math

import jax
import jax.numpy as jnp
import numpy as np
from jax import lax
from jax.experimental import pallas as pl
from jax.experimental.pallas import tpu as pltpu

F32 = jnp.float32
BF16 = jnp.bfloat16

D_MODEL = 2048
BATCH = 4
SEQ = 2048
DEPTH = 2
DEC_BATCH = 128
PAST_LEN = 2048
M_HEADS = 4
M_DV = 192
M_DK = 96
M_QK = M_HEADS * M_DK
M_WIDTH = M_HEADS * M_DV
CONV_WIDTH = 512
CONV_K = 3
A_GROUPS = ((128, 1), (512, 4), (2048, 16))
A_HPG = 4
A_HEAD_DIM = 64
A_GW = A_HPG * A_HEAD_DIM
A_WIDTH = 3 * A_GW
ROPE_THETA = 10000.0
N_BRANCH = 3
D_FF = 5632
LN_EPS = 1e-5
ALPHA = (2 * DEPTH) ** 0.25
IN_SIZES = (M_QK, M_QK, M_WIDTH, M_HEADS, M_HEADS, M_WIDTH,
            CONV_WIDTH, CONV_WIDTH, CONV_WIDTH,
            A_WIDTH, A_WIDTH, A_WIDTH, N_BRANCH * D_MODEL)
IN_OFFSETS = tuple(int(o) for o in np.cumsum((0,) + IN_SIZES))

M_PROMPT = BATCH * SEQ
M_ROWS = M_PROMPT + DEC_BATCH

LANES = 128
DKP = 128
DVP = 256
VMEM_LIMIT = 52 * 1024 * 1024

C_MV, C_MO = 0, 1024
C_MQ, C_MK = 2048, 2560
C_CB, C_CC, C_CH = 3072, 3584, 4096
C_AQ, C_AK, C_AV = 4608, 5376, 6144
C_IF = 6912
PROJ_W = 7168

NEG = -1e30


def _sigmoid(x):
    return 1.0 / (1.0 + jnp.exp(-x))


def _layer_norm(z, g, b):
    mu = jnp.mean(z, axis=-1, keepdims=True)
    zc = z - mu
    var = jnp.mean(zc * zc, axis=-1, keepdims=True)
    return zc * lax.rsqrt(var + LN_EPS) * g + b


def _cparams(sem):
    return pltpu.CompilerParams(dimension_semantics=sem, vmem_limit_bytes=VMEM_LIMIT)


FFN_TM = 640
FFN_TF = 512


def _ffn_kernel(x_ref, wa_ref, wb_ref, wo_ref, g_ref, b_ref, y_ref, yb_ref, xb_scr):
    f = pl.program_id(1)

    @pl.when(f == 0)
    def _():
        xb_scr[...] = x_ref[...].astype(BF16)
        y_ref[...] = jnp.zeros_like(y_ref)

    xb = xb_scr[...]
    a = jnp.dot(xb, wa_ref[...], preferred_element_type=F32)
    b = jnp.dot(xb, wb_ref[...], preferred_element_type=F32)
    h = (a * _sigmoid(a)) * b
    y_ref[...] += jnp.dot(h.astype(BF16), wo_ref[...], preferred_element_type=F32)

    @pl.when(f == pl.num_programs(1) - 1)
    def _():
        z = ALPHA * x_ref[...] + 0.5 * y_ref[...]
        out = _layer_norm(z, g_ref[...], b_ref[...])
        y_ref[...] = out
        yb_ref[...] = out.astype(BF16)


def ffn_ln(x, w_in, w_out, g, b, layer, which):
    m = x.shape[0]
    nf = D_FF // FFN_TF
    return pl.pallas_call(
        _ffn_kernel,
        grid=(m // FFN_TM, nf),
        in_specs=[
            pl.BlockSpec((FFN_TM, D_MODEL), lambda i, f: (i, 0)),
            pl.BlockSpec((None, None, D_MODEL, FFN_TF), lambda i, f: (layer, which, 0, f)),
            pl.BlockSpec((None, None, D_MODEL, FFN_TF), lambda i, f: (layer, which, 0, f + nf)),
            pl.BlockSpec((None, None, FFN_TF, D_MODEL), lambda i, f: (layer, which, f, 0)),
            pl.BlockSpec((1, D_MODEL), lambda i, f: (0, 0)),
            pl.BlockSpec((1, D_MODEL), lambda i, f: (0, 0)),
        ],
        out_specs=[
            pl.BlockSpec((FFN_TM, D_MODEL), lambda i, f: (i, 0)),
            pl.BlockSpec((FFN_TM, D_MODEL), lambda i, f: (i, 0)),
        ],
        out_shape=[jax.ShapeDtypeStruct((m, D_MODEL), F32),
                   jax.ShapeDtypeStruct((m, D_MODEL), BF16)],
        scratch_shapes=[pltpu.VMEM((FFN_TM, D_MODEL), BF16)],
        compiler_params=_cparams(("parallel", "arbitrary")),
        name="ffn_ln",
    )(x, w_in, w_in, w_out, g.reshape(1, D_MODEL), b.reshape(1, D_MODEL))


PROJ_TM = 1040
PROJ_TN = 1024


def _proj_kernel(xb_ref, w_ref, o_ref):
    o_ref[...] = jnp.dot(xb_ref[...], w_ref[...], preferred_element_type=F32)


def branch_proj(xb, w):
    m = xb.shape[0]
    return pl.pallas_call(
        _proj_kernel,
        grid=(m // PROJ_TM, PROJ_W // PROJ_TN),
        in_specs=[pl.BlockSpec((PROJ_TM, D_MODEL), lambda i, j: (i, 0)),
                  pl.BlockSpec((D_MODEL, PROJ_TN), lambda i, j: (0, j))],
        out_specs=pl.BlockSpec((PROJ_TM, PROJ_TN), lambda i, j: (i, j)),
        out_shape=jax.ShapeDtypeStruct((m, PROJ_W), F32),
        compiler_params=_cparams(("parallel", "parallel")),
        name="branch_proj",
    )(xb, w)


MRG_TM = 640
MRG_TN = 256


def _merge_kernel(x_ref, xb_ref, hm_ref, yc_ref, oa_ref, wg0_ref, wg1_ref, wg2_ref,
                  wum_ref, wuc_ref, wua_ref, wo_ref, g_ref, b_ref, y_ref):
    n = pl.program_id(1)

    @pl.when(n == 0)
    def _():
        y_ref[...] = jnp.zeros_like(y_ref)

    xb = xb_ref[...]

    def gated(wg_ref, br_ref, wu_ref):
        gate = _sigmoid(jnp.dot(xb, wg_ref[...], preferred_element_type=F32))
        return gate * jnp.dot(br_ref[...], wu_ref[...], preferred_element_type=F32)

    merged = (gated(wg0_ref, hm_ref, wum_ref) + gated(wg1_ref, yc_ref, wuc_ref)
              + gated(wg2_ref, oa_ref, wua_ref))
    y_ref[...] += jnp.dot(merged.astype(BF16), wo_ref[...], preferred_element_type=F32)

    @pl.when(n == pl.num_programs(1) - 1)
    def _():
        z = ALPHA * x_ref[...] + y_ref[...]
        y_ref[...] = _layer_norm(z, g_ref[...], b_ref[...])


def merge_ln(x, xb, hm, yc, oa, wg, wum, wuc, wua, wo, g, b):
    m = x.shape[0]
    nn = D_MODEL // MRG_TN
    row = lambda w: pl.BlockSpec((MRG_TM, w), lambda i, n: (i, 0))
    return pl.pallas_call(
        _merge_kernel,
        grid=(m // MRG_TM, nn),
        in_specs=[
            row(D_MODEL), row(D_MODEL), row(M_HEADS * DVP), row(CONV_WIDTH), row(A_GW),
            pl.BlockSpec((D_MODEL, MRG_TN), lambda i, n: (0, n)),
            pl.BlockSpec((D_MODEL, MRG_TN), lambda i, n: (0, n + nn)),
            pl.BlockSpec((D_MODEL, MRG_TN), lambda i, n: (0, n + 2 * nn)),
            pl.BlockSpec((M_HEADS * DVP, MRG_TN), lambda i, n: (0, n)),
            pl.BlockSpec((CONV_WIDTH, MRG_TN), lambda i, n: (0, n)),
            pl.BlockSpec((A_GW, MRG_TN), lambda i, n: (0, n)),
            pl.BlockSpec((MRG_TN, D_MODEL), lambda i, n: (n, 0)),
            pl.BlockSpec((1, D_MODEL), lambda i, n: (0, 0)),
            pl.BlockSpec((1, D_MODEL), lambda i, n: (0, 0)),
        ],
        out_specs=pl.BlockSpec((MRG_TM, D_MODEL), lambda i, n: (i, 0)),
        out_shape=jax.ShapeDtypeStruct((m, D_MODEL), F32),
        compiler_params=_cparams(("parallel", "arbitrary")),
        name="merge_ln",
    )(x, xb, hm, yc, oa, wg, wg, wg, wum, wuc, wua, wo,
      g.reshape(1, D_MODEL), b.reshape(1, D_MODEL))


def _conv_prompt_kernel(cb_ref, cc_ref, ch_ref, w_ref, y_ref, st_ref, u_scr):
    u = cc_ref[...] * ch_ref[...]
    u_scr[pl.ds(0, 8), :] = jnp.zeros((8, CONV_WIDTH), F32)
    u_scr[pl.ds(8, SEQ), :] = u
    w = w_ref[...]
    acc = (w[0:1, :] * u_scr[pl.ds(6, SEQ), :] + w[1:2, :] * u_scr[pl.ds(7, SEQ), :]
           + w[2:3, :] * u)
    y_ref[...] = (cb_ref[...] * acc).astype(BF16)
    st_ref[...] = u_scr[pl.ds(8 + SEQ - (CONV_K - 1), CONV_K - 1), :]


def conv_prompt(proj, conv_w):
    blk = lambda c: pl.BlockSpec((SEQ, CONV_WIDTH), lambda b, c=c: (b, c // CONV_WIDTH))
    return pl.pallas_call(
        _conv_prompt_kernel,
        grid=(BATCH,),
        in_specs=[blk(C_CB), blk(C_CC), blk(C_CH),
                  pl.BlockSpec((CONV_K, CONV_WIDTH), lambda b: (0, 0))],
        out_specs=[pl.BlockSpec((SEQ, CONV_WIDTH), lambda b: (b, 0)),
                   pl.BlockSpec((None, CONV_K - 1, CONV_WIDTH), lambda b: (b, 0, 0))],
        out_shape=[jax.ShapeDtypeStruct((M_PROMPT, CONV_WIDTH), BF16),
                   jax.ShapeDtypeStruct((BATCH, CONV_K - 1, CONV_WIDTH), F32)],
        scratch_shapes=[pltpu.VMEM((SEQ + 8, CONV_WIDTH), F32)],
        compiler_params=_cparams(("parallel",)),
        name="conv_prompt",
    )(proj, proj, proj, conv_w)


def _conv_sample_kernel(cb_ref, cc_ref, ch_ref, prev_ref, w_ref, y_ref, st_ref):
    u = cc_ref[...] * ch_ref[...]
    w = w_ref[...]
    p0 = prev_ref[:, 0, :]
    p1 = prev_ref[:, 1, :]
    acc = w[0:1, :] * p0 + w[1:2, :] * p1 + w[2:3, :] * u
    y_ref[...] = (cb_ref[...] * acc).astype(BF16)
    st_ref[:, 0, :] = p1
    st_ref[:, 1, :] = u


def conv_sample(proj, prev, conv_w):
    rb = M_PROMPT // DEC_BATCH
    blk = lambda c: pl.BlockSpec((DEC_BATCH, CONV_WIDTH), lambda i, c=c: (rb, c // CONV_WIDTH))
    full3 = pl.BlockSpec((DEC_BATCH, CONV_K - 1, CONV_WIDTH), lambda i: (0, 0, 0))
    return pl.pallas_call(
        _conv_sample_kernel,
        grid=(1,),
        in_specs=[blk(C_CB), blk(C_CC), blk(C_CH), full3,
                  pl.BlockSpec((CONV_K, CONV_WIDTH), lambda i: (0, 0))],
        out_specs=[pl.BlockSpec((DEC_BATCH, CONV_WIDTH), lambda i: (0, 0)), full3],
        out_shape=[jax.ShapeDtypeStruct((DEC_BATCH, CONV_WIDTH), BF16),
                   jax.ShapeDtypeStruct((DEC_BATCH, CONV_K - 1, CONV_WIDTH), F32)],
        compiler_params=_cparams(("arbitrary",)),
        name="conv_sample",
    )(proj, proj, proj, prev, conv_w)


M_L = 128


def _log_sigmoid(x):
    return jnp.minimum(x, 0.0) - jnp.log1p(jnp.exp(-jnp.abs(x)))


def _head_norm_gate(h, o_pre, gain):
    lane = lax.broadcasted_iota(jnp.int32, h.shape, 1)
    real = lane < M_DV
    mu = jnp.sum(h, axis=-1, keepdims=True) * (1.0 / M_DV)
    hc = jnp.where(real, h - mu, 0.0)
    var = jnp.sum(hc * hc, axis=-1, keepdims=True) * (1.0 / M_DV)
    return _sigmoid(o_pre) * (hc * lax.rsqrt(var + LN_EPS) * gain)


def _mlstm_prompt_kernel(q_ref, k_ref, v_ref, o_ref, if_ref, bias_ref, gain_ref,
                         hm_ref, c_out_ref, n_out_ref, m_out_ref, c_scr, n_scr, m_scr):
    hd = pl.program_id(1)
    c_scr[...] = jnp.zeros_like(c_scr)
    n_scr[...] = jnp.zeros_like(n_scr)
    m_scr[...] = jnp.zeros_like(m_scr)

    row = lax.broadcasted_iota(jnp.int32, (M_L, M_L), 0)
    col = lax.broadcasted_iota(jnp.int32, (M_L, M_L), 1)
    causal = col <= row
    tri = causal.astype(F32)
    bias = bias_ref[...]
    gain = gain_ref[...]

    def chunk(c, carry):
        r0 = pl.multiple_of(c * M_L, M_L)
        x_if = if_ref[pl.ds(r0, M_L), :] + bias
        log_f = _log_sigmoid(x_if)
        cs = jnp.dot(tri, log_f, preferred_element_type=F32, precision=lax.Precision.HIGHEST)
        b_col = jnp.sum(jnp.where(col == hd + M_HEADS, cs, 0.0), axis=1, keepdims=True)
        i_col = jnp.sum(jnp.where(col == hd, x_if, 0.0), axis=1, keepdims=True)
        zt = jnp.where(col == 0, b_col, jnp.where(col == 1, i_col, 0.0)).T
        b_row = zt[0:1, :]
        i_row = zt[1:2, :]
        m_prev = m_scr[...]

        d = jnp.where(causal, b_col - b_row + i_row, NEG)
        inter = b_col + m_prev
        m_t = jnp.maximum(jnp.max(d, axis=1, keepdims=True), inter)
        dmat = jnp.exp(d - m_t)
        q = q_ref[pl.ds(r0, M_L), :]
        k = k_ref[pl.ds(r0, M_L), :] * (M_DK ** -0.5)
        vb = v_ref[pl.ds(r0, M_L), :].astype(BF16)
        qb = q.astype(BF16)
        s = lax.dot_general(qb, k.astype(BF16), (((1,), (1,)), ((), ())),
                            preferred_element_type=F32) * dmat
        w_inter = jnp.exp(inter - m_t)
        c_prev = c_scr[...]
        n_prev = n_scr[...]
        num = (jnp.dot(s.astype(BF16), vb, preferred_element_type=F32)
               + w_inter * jnp.dot(qb, c_prev.astype(BF16), preferred_element_type=F32))
        den = (jnp.sum(s, axis=1, keepdims=True)
               + w_inter * jnp.sum(q * n_prev, axis=1, keepdims=True))
        h = num / jnp.maximum(jnp.abs(den), jnp.exp(-m_t))
        hm_ref[pl.ds(r0, M_L), :] = _head_norm_gate(h, o_ref[pl.ds(r0, M_L), :], gain).astype(BF16)

        b_last = b_row[:, M_L - 1:M_L]
        g_row = b_last - b_row + i_row
        m_new = jnp.maximum(b_last + m_prev, jnp.max(g_row, axis=1, keepdims=True))
        w_k = jnp.exp(b_last - b_col + i_col - m_new)
        decay = jnp.exp(b_last + m_prev - m_new)
        kw = k * w_k
        c_scr[...] = decay * c_prev + lax.dot_general(
            kw.astype(BF16), vb, (((0,), (0,)), ((), ())), preferred_element_type=F32)
        n_scr[...] = decay * n_prev + jnp.sum(kw, axis=0, keepdims=True)
        m_scr[...] = m_new
        return carry

    lax.fori_loop(0, SEQ // M_L, chunk, 0)
    c_out_ref[...] = c_scr[pl.ds(0, M_DK), pl.ds(0, M_DV)]
    n_out_ref[...] = n_scr[:, pl.ds(0, M_DK)]
    m_out_ref[...] = m_scr[...]


def mlstm_prompt(proj, bias, gain):
    def cblk(c0, w):
        return pl.BlockSpec((SEQ, w), lambda b, h: (b, c0 // w + h))
    return pl.pallas_call(
        _mlstm_prompt_kernel,
        grid=(BATCH, M_HEADS),
        in_specs=[cblk(C_MQ, DKP), cblk(C_MK, DKP), cblk(C_MV, DVP), cblk(C_MO, DVP),
                  pl.BlockSpec((SEQ, LANES), lambda b, h: (b, C_IF // LANES)),
                  pl.BlockSpec((1, LANES), lambda b, h: (0, 0)),
                  pl.BlockSpec((None, 1, DVP), lambda b, h: (h, 0, 0))],
        out_specs=[pl.BlockSpec((SEQ, DVP), lambda b, h: (b, h)),
                   pl.BlockSpec((None, None, M_DK, M_DV), lambda b, h: (b, h, 0, 0)),
                   pl.BlockSpec((None, None, 1, M_DK), lambda b, h: (b, h, 0, 0)),
                   pl.BlockSpec((None, None, 1, 1), lambda b, h: (b, h, 0, 0))],
        out_shape=[jax.ShapeDtypeStruct((M_PROMPT, M_HEADS * DVP), BF16),
                   jax.ShapeDtypeStruct((BATCH, M_HEADS, M_DK, M_DV), F32),
                   jax.ShapeDtypeStruct((BATCH, M_HEADS, 1, M_DK), F32),
                   jax.ShapeDtypeStruct((BATCH, M_HEADS, 1, 1), F32)],
        scratch_shapes=[pltpu.VMEM((DKP, DVP), F32), pltpu.VMEM((1, DKP), F32),
                        pltpu.VMEM((1, 1), F32)],
        compiler_params=_cparams(("parallel", "parallel")),
        name="mlstm_prompt",
    )(proj, proj, proj, proj, proj, bias, gain)


MS_DC = 48


def _pick_row(x8, j):
    rows = lax.broadcasted_iota(jnp.int32, x8.shape, 0)
    return jnp.sum(jnp.where(rows == j, x8, 0.0), axis=0, keepdims=True)


def _mlstm_sample_kernel(q_ref, k_ref, v_ref, o_ref, if_ref, bias_ref, gain_ref,
                         c0_ref, n0_ref, m0_ref,
                         hm_ref, c_out_ref, n_out_ref, m_out_ref,
                         qt_scr, kw_scr, vt_scr, acc_scr, st_scr):
    hd = pl.program_id(0)
    c = pl.program_id(1)

    @pl.when(c == 0)
    def _():
        qt = q_ref[...].T
        kt = (k_ref[...] * (M_DK ** -0.5)).T
        vt_scr[...] = v_ref[...].T
        x_if = if_ref[...].T[0:2 * M_HEADS, :] + bias_ref[...]
        i_pre = _pick_row(x_if, hd)
        log_f = _log_sigmoid(_pick_row(x_if, hd + M_HEADS))
        inter = log_f + m0_ref[...]
        m_new = jnp.maximum(i_pre, inter)
        w_k = jnp.exp(i_pre - m_new)
        decay = jnp.exp(inter - m_new)
        n_prev = n0_ref[...]
        s = jnp.sum(qt * kt, axis=0, keepdims=True) * w_k
        den = s + decay * jnp.sum(qt[:M_DK] * n_prev, axis=0, keepdims=True)
        kw = kt * w_k
        qt_scr[...] = qt
        kw_scr[...] = kw
        n_out_ref[...] = decay * n_prev + kw[:M_DK]
        m_out_ref[...] = m_new
        st_scr[0:1, :] = s
        st_scr[1:2, :] = decay
        st_scr[2:3, :] = den
        st_scr[3:4, :] = m_new
        acc_scr[...] = jnp.zeros_like(acc_scr)

    decay = st_scr[1:2, :]
    vt = vt_scr[pl.ds(0, M_DV), :]

    def tile(t, acc):
        r8 = pl.multiple_of(c * MS_DC + t * 8, 8)
        q8 = qt_scr[pl.ds(r8, 8), :]
        kw8 = kw_scr[pl.ds(r8, 8), :]
        for r in range(8):
            c_row = c0_ref[t * 8 + r]
            c_out_ref[t * 8 + r] = decay * c_row + kw8[r:r + 1, :] * vt
            acc = acc + q8[r:r + 1, :] * c_row
        return acc

    acc = lax.fori_loop(0, MS_DC // 8, tile, acc_scr[...])
    acc_scr[...] = acc

    @pl.when(c == pl.num_programs(1) - 1)
    def _():
        s = st_scr[0:1, :]
        den = st_scr[2:3, :]
        m_t = st_scr[3:4, :]
        h = (s * vt + decay * acc) / jnp.maximum(jnp.abs(den), jnp.exp(-m_t))
        mu = jnp.mean(h, axis=0, keepdims=True)
        hc = h - mu
        var = jnp.mean(hc * hc, axis=0, keepdims=True)
        o_pre = o_ref[...].T[:M_DV, :]
        out = _sigmoid(o_pre) * (hc * lax.rsqrt(var + LN_EPS) * gain_ref[...])
        out = jnp.concatenate([out, jnp.zeros((DVP - M_DV, DEC_BATCH), F32)], axis=0)
        hm_ref[...] = out.T.astype(BF16)


def mlstm_sample(proj, bias_col, gain_col, c0t, n0t, m0t, layer):
    rb = M_PROMPT // DEC_BATCH
    nc = M_DK // MS_DC
    def cblk(c0_, w):
        return pl.BlockSpec((DEC_BATCH, w), lambda h, c: (rb, c0_ // w + h))
    return pl.pallas_call(
        _mlstm_sample_kernel,
        grid=(M_HEADS, nc),
        in_specs=[cblk(C_MQ, DKP), cblk(C_MK, DKP), cblk(C_MV, DVP), cblk(C_MO, DVP),
                  pl.BlockSpec((DEC_BATCH, LANES), lambda h, c: (rb, C_IF // LANES)),
                  pl.BlockSpec((2 * M_HEADS, DEC_BATCH), lambda h, c: (0, 0)),
                  pl.BlockSpec((None, M_DV, 1), lambda h, c: (h, 0, 0)),
                  pl.BlockSpec((None, None, MS_DC, M_DV, DEC_BATCH), lambda h, c: (layer, h, c, 0, 0)),
                  pl.BlockSpec((None, None, M_DK, DEC_BATCH), lambda h, c: (layer, h, 0, 0)),
                  pl.BlockSpec((None, None, 1, DEC_BATCH), lambda h, c: (layer, h, 0, 0))],
        out_specs=[pl.BlockSpec((DEC_BATCH, DVP), lambda h, c: (0, h)),
                   pl.BlockSpec((None, MS_DC, M_DV, DEC_BATCH), lambda h, c: (h, c, 0, 0)),
                   pl.BlockSpec((None, M_DK, DEC_BATCH), lambda h, c: (h, 0, 0)),
                   pl.BlockSpec((None, 1, DEC_BATCH), lambda h, c: (h, 0, 0))],
        out_shape=[jax.ShapeDtypeStruct((DEC_BATCH, M_HEADS * DVP), BF16),
                   jax.ShapeDtypeStruct((M_HEADS, M_DK, M_DV, DEC_BATCH), F32),
                   jax.ShapeDtypeStruct((M_HEADS, M_DK, DEC_BATCH), F32),
                   jax.ShapeDtypeStruct((M_HEADS, 1, DEC_BATCH), F32)],
        scratch_shapes=[pltpu.VMEM((DKP, DEC_BATCH), F32), pltpu.VMEM((DKP, DEC_BATCH), F32),
                        pltpu.VMEM((DVP, DEC_BATCH), F32), pltpu.VMEM((M_DV, DEC_BATCH), F32),
                        pltpu.VMEM((8, DEC_BATCH), F32)],
        compiler_params=_cparams(("arbitrary", "arbitrary")),
        name="mlstm_sample",
    )(proj, proj, proj, proj, proj, bias_col, gain_col, c0t, n0t, m0t)


A_Q = 128
A_LT = A_GW // LANES


def _rope(x, cos, sin_signed):
    lane = lax.broadcasted_iota(jnp.int32, x.shape, 1)
    first_half = (lane % A_HEAD_DIM) < (A_HEAD_DIM // 2)
    partner = jnp.where(first_half, pltpu.roll(x, A_GW - A_HEAD_DIM // 2, 1),
                        pltpu.roll(x, A_HEAD_DIM // 2, 1))
    return x * cos + partner * sin_signed


def _head_masks(shape):
    lane = lax.broadcasted_iota(jnp.int32, shape, 1)
    return [(lane // A_HEAD_DIM) == h for h in range(A_HPG)]


def _attn_group_prompt(dil, gi, qs_scr, ks_scr, vs_scr, o_scr, l_scr):
    length = SEQ // dil
    nb = length // A_Q
    row = lax.broadcasted_iota(jnp.int32, (A_Q, A_Q), 0)
    col = lax.broadcasted_iota(jnp.int32, (A_Q, A_Q), 1)
    cur_ok = col <= row
    prev_ok = col >= row
    masks = _head_masks((A_Q, A_GW))
    nt = (((1,), (1,)), ((), ()))

    def window(start):
        if dil == 1:
            return pl.ds(pl.multiple_of(start, A_Q), A_Q)
        return pl.ds(start, A_Q, stride=dil)

    def rows(scr, start):
        w = window(start)
        return jnp.concatenate([scr[t, w, :] for t in range(A_LT)], axis=1).astype(BF16)

    def block(idx, carry):
        r = idx % dil
        n = idx // dil
        base = r + (dil * A_Q) * n
        qb = rows(qs_scr, base)
        kc = rows(ks_scr, base)
        vc = rows(vs_scr, base)
        if nb > 1:
            pbase = jnp.maximum(base - dil * A_Q, r)
            kp = rows(ks_scr, pbase)
            vp = rows(vs_scr, pbase)
            has_prev = n > 0
        o_acc = jnp.zeros((A_Q, A_GW), F32)
        l_acc = jnp.zeros((A_Q, A_GW), F32)
        for h in range(A_HPG):
            qh = jnp.where(masks[h], qb, jnp.zeros_like(qb))
            s_c = jnp.where(cur_ok, lax.dot_general(qh, kc, nt, preferred_element_type=F32), NEG)
            m = jnp.max(s_c, axis=1, keepdims=True)
            if nb > 1:
                s_p = jnp.where(jnp.logical_and(prev_ok, has_prev),
                                lax.dot_general(qh, kp, nt, preferred_element_type=F32), NEG)
                m = jnp.maximum(m, jnp.max(s_p, axis=1, keepdims=True))
            p_c = jnp.exp(s_c - m)
            l = jnp.sum(p_c, axis=1, keepdims=True)
            o_h = jnp.dot(p_c.astype(BF16), vc, preferred_element_type=F32)
            if nb > 1:
                p_p = jnp.exp(s_p - m)
                l = l + jnp.sum(p_p, axis=1, keepdims=True)
                o_h = o_h + jnp.dot(p_p.astype(BF16), vp, preferred_element_type=F32)
            o_acc = o_acc + jnp.where(masks[h], o_h / l, 0.0)
            l_acc = l_acc + jnp.where(masks[h], m + jnp.log(l), 0.0)
        w = window(base)
        for t in range(A_LT):
            o_scr[gi, t, w, :] = o_acc[:, t * LANES:(t + 1) * LANES]
            l_scr[gi, t, w, :] = l_acc[:, t * LANES:(t + 1) * LANES]
        return carry

    lax.fori_loop(0, dil * nb, block, 0)


def _attn_prompt_kernel(q_ref, k_ref, v_ref, cos_ref, sin_ref, oa_ref, kr_ref,
                        qs_scr, ks_scr, vs_scr, o_scr, l_scr):
    g = pl.program_id(1)
    rc = 256

    def rope_rows(c, carry):
        sl = pl.ds(pl.multiple_of(c * rc, rc), rc)
        cos = cos_ref[sl, :]
        sin = sin_ref[sl, :]
        qr = _rope(q_ref[sl, :], cos, sin) * (A_HEAD_DIM ** -0.5)
        kr = _rope(k_ref[sl, :], cos, sin)
        v = v_ref[sl, :]
        kr_ref[sl, :] = kr
        for t in range(A_LT):
            lanes = slice(t * LANES, (t + 1) * LANES)
            qs_scr[t, sl, :] = qr[:, lanes]
            ks_scr[t, sl, :] = kr[:, lanes]
            vs_scr[t, sl, :] = v[:, lanes]
        return carry

    lax.fori_loop(0, SEQ // rc, rope_rows, 0)

    for gi, (_, dil) in enumerate(A_GROUPS):
        @pl.when(g == gi)
        def _(gi=gi, dil=dil):
            _attn_group_prompt(dil, gi, qs_scr, ks_scr, vs_scr, o_scr, l_scr)

    @pl.when(g == len(A_GROUPS) - 1)
    def _():
        def comb(c, carry):
            sl = pl.ds(pl.multiple_of(c * rc, rc), rc)
            for t in range(A_LT):
                l0, l1, l2 = l_scr[0, t, sl, :], l_scr[1, t, sl, :], l_scr[2, t, sl, :]
                mx = jnp.maximum(jnp.maximum(l0, l1), l2)
                e0, e1, e2 = jnp.exp(l0 - mx), jnp.exp(l1 - mx), jnp.exp(l2 - mx)
                tot = e0 * o_scr[0, t, sl, :] + e1 * o_scr[1, t, sl, :] + e2 * o_scr[2, t, sl, :]
                oa_ref[sl, pl.ds(t * LANES, LANES)] = (tot / (e0 + e1 + e2)).astype(BF16)
            return carry
        lax.fori_loop(0, SEQ // rc, comb, 0)


def attn_prompt(proj, cos, sin):
    def gblk(c0):
        return pl.BlockSpec((SEQ, A_GW), lambda b, g: (b, c0 // A_GW + g))
    tab = pl.BlockSpec((SEQ, A_GW), lambda b, g: (0, 0))
    return pl.pallas_call(
        _attn_prompt_kernel,
        grid=(BATCH, len(A_GROUPS)),
        in_specs=[gblk(C_AQ), gblk(C_AK), gblk(C_AV), tab, tab],
        out_specs=[pl.BlockSpec((SEQ, A_GW), lambda b, g: (b, 0)),
                   pl.BlockSpec((SEQ, A_GW), lambda b, g: (b, g))],
        out_shape=[jax.ShapeDtypeStruct((M_PROMPT, A_GW), BF16),
                   jax.ShapeDtypeStruct((M_PROMPT, A_WIDTH), F32)],
        scratch_shapes=[pltpu.VMEM((A_LT, SEQ, LANES), F32),
                        pltpu.VMEM((A_LT, SEQ, LANES), F32),
                        pltpu.VMEM((A_LT, SEQ, LANES), F32),
                        pltpu.VMEM((len(A_GROUPS), A_LT, SEQ, LANES), F32),
                        pltpu.VMEM((len(A_GROUPS), A_LT, SEQ, LANES), F32)],
        compiler_params=_cparams(("parallel", "arbitrary")),
        name="attn_prompt",
    )(proj, proj, proj, cos, sin)


AS_BB = 2


def _attn_sample_kernel(q_ref, k_ref, v_ref, cos_ref, sin_ref, c0_ref, c1_ref, c2_ref,
                        oa_ref, kt_ref, vt_ref, qt_scr, s0_scr, ot_scr, lt_scr):
    i = pl.program_id(0)
    lane_b = lax.broadcasted_iota(jnp.int32, (1, DEC_BATCH), 1)
    sub8 = lax.broadcasted_iota(jnp.int32, (8, DEC_BATCH), 0)

    @pl.when(i == 0)
    def _():
        cos = cos_ref[...]
        sin = sin_ref[...]
        for gi in range(len(A_GROUPS)):
            gs = pl.ds(gi * A_GW, A_GW)
            qt = (_rope(q_ref[:, gs], cos, sin) * (A_HEAD_DIM ** -0.5)).T
            kt = _rope(k_ref[:, gs], cos, sin).T
            qt_scr[gi] = qt
            kt_ref[gi] = kt
            vt_ref[gi] = v_ref[:, gs].T
            prod = qt * kt
            s0 = jnp.zeros((8, DEC_BATCH), F32)
            for h in range(A_HPG):
                part = jnp.sum(prod[h * A_HEAD_DIM:(h + 1) * A_HEAD_DIM], axis=0, keepdims=True)
                s0 = jnp.where(sub8 == h, part, s0)
            s0_scr[gi] = s0
        ot_scr[...] = jnp.zeros_like(ot_scr)
        lt_scr[...] = jnp.zeros_like(lt_scr)

    for bl in range(AS_BB):
        pick = lane_b == i * AS_BB + bl
        for gi, (cache_ref, (_, dil)) in enumerate(zip((c0_ref, c1_ref, c2_ref), A_GROUPS)):
            wb = cache_ref.shape[-1]
            if dil > 1:
                keep = (lax.broadcasted_iota(jnp.int32, (1, wb), 1) & (dil - 1)) == 0
            s0_all = s0_scr[gi]
            for h in range(A_HPG):
                hs = pl.ds(h * A_HEAD_DIM, A_HEAD_DIM)
                q_col = jnp.sum(jnp.where(pick, qt_scr[gi, hs, :], 0.0), axis=1, keepdims=True)
                v_col = jnp.sum(jnp.where(pick, vt_ref[gi, hs, :], 0.0), axis=1, keepdims=True)
                s0 = jnp.sum(jnp.where(pick, s0_all[h:h + 1, :], 0.0), axis=1, keepdims=True)
                s = jnp.sum(q_col * cache_ref[bl, 0, h], axis=0, keepdims=True)
                if dil > 1:
                    s = jnp.where(keep, s, NEG)
                m = jnp.maximum(jnp.max(s, axis=1, keepdims=True), s0)
                p = jnp.exp(s - m)
                p0 = jnp.exp(s0 - m)
                l = jnp.sum(p, axis=1, keepdims=True) + p0
                o = (jnp.sum(p * cache_ref[bl, 1, h], axis=1, keepdims=True) + p0 * v_col) / l
                ot_scr[gi, hs, :] = jnp.where(pick, o, ot_scr[gi, hs, :])
                lt_scr[gi, pl.ds(h, 1), :] = jnp.where(pick, m + jnp.log(l), lt_scr[gi, pl.ds(h, 1), :])

    @pl.when(i == pl.num_programs(0) - 1)
    def _():
        for h in range(A_HPG):
            hs = pl.ds(h * A_HEAD_DIM, A_HEAD_DIM)
            l0, l1, l2 = (lt_scr[gi, pl.ds(h, 1), :] for gi in range(3))
            mx = jnp.maximum(jnp.maximum(l0, l1), l2)
            e0, e1, e2 = jnp.exp(l0 - mx), jnp.exp(l1 - mx), jnp.exp(l2 - mx)
            tot = e0 * ot_scr[0, hs, :] + e1 * ot_scr[1, hs, :] + e2 * ot_scr[2, hs, :]
            ot_scr[0, hs, :] = tot / (e0 + e1 + e2)
        oa_ref[...] = ot_scr[0].T.astype(BF16)


def attn_sample(proj, cos, sin, caches_t, layer):
    rb = M_PROMPT // DEC_BATCH
    def pblk(c0):
        return pl.BlockSpec((DEC_BATCH, A_WIDTH), lambda i: (rb, c0 // A_WIDTH))
    tab = pl.BlockSpec((1, A_GW), lambda i: (0, 0))
    cache_specs = [pl.BlockSpec((None, AS_BB, 2, A_HPG, A_HEAD_DIM, ct.shape[-1]),
                                lambda i: (layer, i, 0, 0, 0, 0)) for ct in caches_t]
    ng = len(A_GROUPS)
    full3 = pl.BlockSpec((ng, A_GW, DEC_BATCH), lambda i: (0, 0, 0))
    return pl.pallas_call(
        _attn_sample_kernel,
        grid=(DEC_BATCH // AS_BB,),
        in_specs=[pblk(C_AQ), pblk(C_AK), pblk(C_AV), tab, tab] + cache_specs,
        out_specs=[pl.BlockSpec((DEC_BATCH, A_GW), lambda i: (0, 0)), full3, full3],
        out_shape=[jax.ShapeDtypeStruct((DEC_BATCH, A_GW), BF16),
                   jax.ShapeDtypeStruct((ng, A_GW, DEC_BATCH), F32),
                   jax.ShapeDtypeStruct((ng, A_GW, DEC_BATCH), F32)],
        scratch_shapes=[pltpu.VMEM((ng, A_GW, DEC_BATCH), F32), pltpu.VMEM((ng, 8, DEC_BATCH), F32),
                        pltpu.VMEM((ng, A_GW, DEC_BATCH), F32), pltpu.VMEM((ng, 8, DEC_BATCH), F32)],
        compiler_params=_cparams(("arbitrary",)),
        name="attn_sample",
    )(proj, proj, proj, cos, sin, *caches_t)


def _pad_heads(w, d, dp):
    r = w.shape[0]
    w = w.reshape(r, M_HEADS, d)
    return jnp.pad(w, ((0, 0), (0, 0), (0, dp - d))).reshape(r, M_HEADS * dp)


def _layer_weights(w_in_l, w_up_m, w_up_c, w_up_a, w_o_l):
    o = IN_OFFSETS
    piece = lambda i: w_in_l[:, o[i]:o[i + 1]]
    mq, mk, mv, mi, mf, mo, cb, cc, ch, aq, ak, av, gt = [piece(i) for i in range(13)]
    w_br = jnp.concatenate(
        [_pad_heads(mv, M_DV, DVP), _pad_heads(mo, M_DV, DVP),
         _pad_heads(mq, M_DK, DKP), _pad_heads(mk, M_DK, DKP),
         cb, cc, ch, aq, ak, av,
         mi, mf,jnp.zeros((D_MODEL, PROJ_W - C_IF - 2 * M_HEADS), F32)], axis=1).astype(BF16)
    w_um = jnp.pad(w_up_m.reshape(M_HEADS, M_DV, D_MODEL),
                   ((0, 0), (0, DVP - M_DV), (0, 0))).reshape(M_HEADS * DVP, D_MODEL)
    return (w_br, gt.astype(BF16), w_um.astype(BF16), w_up_c.astype(BF16),
            w_up_a.astype(BF16), w_o_l.astype(BF16))


def _rope_tables(pos):
    half = A_HEAD_DIM // 2
    inv = ROPE_THETA ** (-(2.0 * jnp.arange(half, dtype=F32)) / A_HEAD_DIM)
    ang = pos.astype(F32)[:, None] * inv[None, :]
    cos = jnp.cos(ang)
    sin = jnp.sin(ang)
    cos = jnp.tile(jnp.concatenate([cos, cos], axis=-1), (1, A_HPG))
    sin = jnp.tile(jnp.concatenate([-sin, sin], axis=-1), (1, A_HPG))
    return cos, sin


def kernel(x_prompt, x_sample, state_mlstm_C, state_mlstm_n, state_mlstm_m, state_conv,
           cache_attn_kv_w128, cache_attn_kv_w512, cache_attn_kv_w2048,
           w_in, b_gate_if, mlstm_norm_g, conv_w, w_up_mlstm, w_up_conv, w_up_attn, w_o,
           w_ffn_in, w_ffn_out, ln_g, ln_b):
    x = jnp.concatenate([x_prompt.reshape(M_PROMPT, D_MODEL),
                         x_sample.reshape(DEC_BATCH, D_MODEL)], axis=0)
    cos_p, sin_p = _rope_tables(jnp.arange(SEQ))
    cos_s, sin_s = _rope_tables(PAST_LEN + jnp.arange(1))
    w_ffn_in_b = w_ffn_in.astype(BF16)
    w_ffn_out_b = w_ffn_out.astype(BF16)

    c0t = jnp.transpose(state_mlstm_C, (0, 2, 3, 4, 1))
    n0t = jnp.transpose(state_mlstm_n, (0, 2, 3, 1))
    m0t = jnp.transpose(state_mlstm_m, (0, 2, 1)).reshape(DEPTH, M_HEADS, 1, DEC_BATCH)
    caches_t = [jnp.transpose(c, (0, 1, 3, 4, 5, 2))
                for c in (cache_attn_kv_w128, cache_attn_kv_w512, cache_attn_kv_w2048)]

    p_states, s_states = [], []
    for l in range(DEPTH):
        w_br, w_g, w_um, w_uc, w_ua, w_ol = _layer_weights(
            w_in[l], w_up_mlstm[l], w_up_conv[l], w_up_attn[l], w_o[l])
        bias = jnp.pad(b_gate_if[l], (0, LANES - 2 * M_HEADS)).reshape(1, LANES)
        bias_col = jnp.broadcast_to(b_gate_if[l][:, None], (2 * M_HEADS, DEC_BATCH))
        gain = jnp.pad(mlstm_norm_g[l].reshape(M_HEADS, M_DV), ((0, 0), (0, DVP - M_DV)))
        gain_col = mlstm_norm_g[l].reshape(M_HEADS, M_DV, 1)

        x, xb = ffn_ln(x, w_ffn_in_b, w_ffn_out_b, ln_g[l, 0], ln_b[l, 0], l, 0)
        proj = branch_proj(xb, w_br)

        hm_p, pc, pn, pm = mlstm_prompt(proj, bias, gain.reshape(M_HEADS, 1, DVP))
        hm_s, sct, snt, smt = mlstm_sample(proj, bias_col, gain_col, c0t, n0t, m0t, l)
        yc_p, pconv = conv_prompt(proj, conv_w[l])
        yc_s, sconv = conv_sample(proj, state_conv[l], conv_w[l])
        oa_p, kr_p = attn_prompt(proj, cos_p, sin_p)
        oa_s, kt_s, vt_s = attn_sample(proj, cos_s, sin_s, caches_t, l)

        hm = jnp.concatenate([hm_p, hm_s], axis=0)
        yc = jnp.concatenate([yc_p, yc_s], axis=0)
        oa = jnp.concatenate([oa_p, oa_s], axis=0)
        x = merge_ln(x, xb, hm, yc, oa, w_g, w_um, w_uc, w_ua, w_ol, ln_g[l, 1], ln_b[l, 1])
        x, _ = ffn_ln(x, w_ffn_in_b, w_ffn_out_b, ln_g[l, 2], ln_b[l, 2], l, 1)

        v_p = proj[:M_PROMPT, C_AV:C_AV + A_WIDTH].reshape(BATCH, SEQ, 3, A_HPG, A_HEAD_DIM)
        k_p = kr_p.reshape(BATCH, SEQ, 3, A_HPG, A_HEAD_DIM)
        kt_s = kt_s.reshape(3, A_HPG, A_HEAD_DIM, DEC_BATCH)
        vt_s = vt_s.reshape(3, A_HPG, A_HEAD_DIM, DEC_BATCH)
        kv_p, kv_s = [], []
        for gi, (win, _) in enumerate(A_GROUPS):
            keep = min(win, SEQ)
            kv_p.append(jnp.stack([k_p[:, SEQ - keep:, gi], v_p[:, SEQ - keep:, gi]], axis=2))
            kv_s.append(jnp.stack([kt_s[gi], vt_s[gi]], axis=0))
        p_states.append((pc, pn.reshape(BATCH, M_HEADS, M_DK), pm.reshape(BATCH, M_HEADS), pconv,
                         kv_p[0], kv_p[1], kv_p[2]))
        s_states.append((sct, snt, smt, sconv, kv_s[0], kv_s[1], kv_s[2]))

    y_prompt = x[:M_PROMPT].reshape(BATCH, SEQ, D_MODEL)
    y_sample = x[M_PROMPT:].reshape(DEC_BATCH, 1, D_MODEL)
    p_out = [jnp.stack(z) for z in zip(*p_states)]
    sct, snt, smt, sconv, kv0, kv1, kv2 = [jnp.stack(z) for z in zip(*s_states)]
    s_out = [jnp.transpose(sct, (0, 4, 1, 2, 3)),
             jnp.transpose(snt, (0, 3, 1, 2)),
             jnp.transpose(smt.reshape(DEPTH, M_HEADS, DEC_BATCH), (0, 2, 1)),
             sconv]
    s_out += [jnp.transpose(kv, (0, 4, 1, 2, 3)).reshape(DEPTH, DEC_BATCH, 1, 2, A_HPG, A_HEAD_DIM)
              for kv in (kv0, kv1, kv2)]
    return (y_prompt, y_sample, *p_out, *s_out)
```

```python
import functools
import math

import jax
import jax.numpy as jnp
import numpy as np
from jax import lax
from jax.experimental import pallas as pl
from jax.experimental.pallas import tpu as pltpu

F32 = jnp.float32
BF16 = jnp.bfloat16

D_MODEL = 2048
BATCH = 4
SEQ = 2048
DEPTH = 2
DEC_BATCH = 128
PAST_LEN = 2048
M_HEADS = 4
M_DV = 192
M_DK = 96
M_QK = M_HEADS * M_DK
M_WIDTH = M_HEADS * M_DV
CONV_WIDTH = 512
CONV_K = 3
A_GROUPS = ((128, 1), (512, 4), (2048, 16))
A_HPG = 4
A_HEAD_DIM = 64
A_GW = A_HPG * A_HEAD_DIM
A_WIDTH = 3 * A_GW
ROPE_THETA = 10000.0
N_BRANCH = 3
D_FF = 5632
LN_EPS = 1e-5
ALPHA = (2 * DEPTH) ** 0.25
IN_SIZES = (M_QK, M_QK, M_WIDTH, M_HEADS, M_HEADS, M_WIDTH,
            CONV_WIDTH, CONV_WIDTH, CONV_WIDTH,
            A_WIDTH, A_WIDTH, A_WIDTH, N_BRANCH * D_MODEL)
IN_OFFSETS = tuple(int(o) for o in np.cumsum((0,) + IN_SIZES))

M_PROMPT = BATCH * SEQ
M_ROWS = M_PROMPT + DEC_BATCH

LANES = 128
DKP = 128
DVP = 256
VMEM_LIMIT = 52 * 1024 * 1024

C_MV, C_MO = 0, 1024
C_MQ, C_MK = 2048, 2560
C_CB, C_CC, C_CH = 3072, 3584, 4096
C_AQ, C_AK, C_AV = 4608, 5376, 6144
C_IF = 6912
PROJ_W = 7168

NEG = -1e30


def _sigmoid(x):
    return 1.0 / (1.0 + jnp.exp(-x))


def _layer_norm(z, g, b):
    mu = jnp.mean(z, axis=-1, keepdims=True)
    zc = z - mu
    var = jnp.mean(zc * zc, axis=-1, keepdims=True)
    return zc * lax.rsqrt(var + LN_EPS) * g + b


def _cparams(sem):
    return pltpu.CompilerParams(dimension_semantics=sem, vmem_limit_bytes=VMEM_LIMIT)


FFN_TM = 640
FFN_TF = 512


def _ffn_kernel(x_ref, wa_ref, wb_ref, wo_ref, g_ref, b_ref, y_ref, yb_ref, xb_scr):
    f = pl.program_id(1)

    @pl.when(f == 0)
    def _():
        xb_scr[...] = x_ref[...].astype(BF16)
        y_ref[...] = jnp.zeros_like(y_ref)

    xb = xb_scr[...]
    a = jnp.dot(xb, wa_ref[...], preferred_element_type=F32)
    b = jnp.dot(xb, wb_ref[...], preferred_element_type=F32)
    h = (a * _sigmoid(a)) * b
    y_ref[...] += jnp.dot(h.astype(BF16), wo_ref[...], preferred_element_type=F32)

    @pl.when(f == pl.num_programs(1) - 1)
    def _():
        z = ALPHA * x_ref[...] + 0.5 * y_ref[...]
        out = _layer_norm(z, g_ref[...], b_ref[...])
        y_ref[...] = out
        yb_ref[...] = out.astype(BF16)


def ffn_ln(x, w_in, w_out, g, b, layer, which):
    m = x.shape[0]
    nf = D_FF // FFN_TF
    return pl.pallas_call(
        _ffn_kernel,
        grid=(m // FFN_TM, nf),
        in_specs=[
            pl.BlockSpec((FFN_TM, D_MODEL), lambda i, f: (i, 0)),
            pl.BlockSpec((None, None, D_MODEL, FFN_TF), lambda i, f: (layer, which, 0, f)),
            pl.BlockSpec((None, None, D_MODEL, FFN_TF), lambda i, f: (layer, which, 0, f + nf)),
            pl.BlockSpec((None, None, FFN_TF, D_MODEL), lambda i, f: (layer, which, f, 0)),
            pl.BlockSpec((1, D_MODEL), lambda i, f: (0, 0)),
            pl.BlockSpec((1, D_MODEL), lambda i, f: (0, 0)),
        ],
        out_specs=[
            pl.BlockSpec((FFN_TM, D_MODEL), lambda i, f: (i, 0)),
            pl.BlockSpec((FFN_TM, D_MODEL), lambda i, f: (i, 0)),
        ],
        out_shape=[jax.ShapeDtypeStruct((m, D_MODEL), F32),
                   jax.ShapeDtypeStruct((m, D_MODEL), BF16)],
        scratch_shapes=[pltpu.VMEM((FFN_TM, D_MODEL), BF16)],
        compiler_params=_cparams(("parallel", "arbitrary")),
        name="ffn_ln",
    )(x, w_in, w_in, w_out, g.reshape(1, D_MODEL), b.reshape(1, D_MODEL))


PROJ_TM = 1040
PROJ_TN = 1024


def _proj_kernel(xb_ref, w_ref, o_ref):
    o_ref[...] = jnp.dot(xb_ref[...], w_ref[...], preferred_element_type=F32)


def branch_proj(xb, w):
    m = xb.shape[0]
    return pl.pallas_call(
        _proj_kernel,
        grid=(m // PROJ_TM, PROJ_W // PROJ_TN),
        in_specs=[pl.BlockSpec((PROJ_TM, D_MODEL), lambda i, j: (i, 0)),
                  pl.BlockSpec((D_MODEL, PROJ_TN), lambda i, j: (0, j))],
        out_specs=pl.BlockSpec((PROJ_TM, PROJ_TN), lambda i, j: (i, j)),
        out_shape=jax.ShapeDtypeStruct((m, PROJ_W), F32),
        compiler_params=_cparams(("parallel", "parallel")),
        name="branch_proj",
    )(xb, w)


MRG_TM = 640
MRG_TN = 512


def _gate_up_kernel(xb_ref, hm_ref, yc_ref, oa_ref, wg0_ref, wg1_ref, wg2_ref,
                    wum_ref, wuc_ref, wua_ref, o_ref):
    xb = xb_ref[...]

    def gated(wg_ref, br_ref, wu_ref):
        gate = _sigmoid(jnp.dot(xb, wg_ref[...], preferred_element_type=F32))
        return gate * jnp.dot(br_ref[...], wu_ref[...], preferred_element_type=F32)

    merged = (gated(wg0_ref, hm_ref, wum_ref) + gated(wg1_ref, yc_ref, wuc_ref)
              + gated(wg2_ref, oa_ref, wua_ref))
    o_ref[...] = merged.astype(BF16)


def _out_ln_kernel(x_ref, mg_ref, wo_ref, g_ref, b_ref, y_ref):
    z = ALPHA * x_ref[...] + jnp.dot(mg_ref[...], wo_ref[...], preferred_element_type=F32)
    y_ref[...] = _layer_norm(z, g_ref[...], b_ref[...])


def merge_ln(x, xb, hm, yc, oa, wg, wum, wuc, wua, wo, g, b):
    m = x.shape[0]
    nn = D_MODEL // MRG_TN
    row = lambda w: pl.BlockSpec((MRG_TM, w), lambda i, n: (i, 0))
    merged = pl.pallas_call(
        _gate_up_kernel,
        grid=(m // MRG_TM, nn),
        in_specs=[
            row(D_MODEL), row(M_HEADS * DVP), row(CONV_WIDTH), row(A_GW),
            pl.BlockSpec((D_MODEL, MRG_TN), lambda i, n: (0, n)),
            pl.BlockSpec((D_MODEL, MRG_TN), lambda i, n: (0, n + nn)),
            pl.BlockSpec((D_MODEL, MRG_TN), lambda i, n: (0, n + 2 * nn)),
            pl.BlockSpec((M_HEADS * DVP, MRG_TN), lambda i, n: (0, n)),
            pl.BlockSpec((CONV_WIDTH, MRG_TN), lambda i, n: (0, n)),
            pl.BlockSpec((A_GW, MRG_TN), lambda i, n: (0, n)),
        ],
        out_specs=pl.BlockSpec((MRG_TM, MRG_TN), lambda i, n: (i, n)),
        out_shape=jax.ShapeDtypeStruct((m, D_MODEL), BF16),
        compiler_params=_cparams(("parallel", "parallel")),
        name="gate_up",
    )(xb, hm, yc, oa, wg, wg, wg, wum, wuc, wua)
    rows = pl.BlockSpec((MRG_TM, D_MODEL), lambda i: (i, 0))
    vec = pl.BlockSpec((1, D_MODEL), lambda i: (0, 0))
    return pl.pallas_call(
        _out_ln_kernel,
        grid=(m // MRG_TM,),
        in_specs=[rows, rows, pl.BlockSpec((D_MODEL, D_MODEL), lambda i: (0, 0)), vec, vec],
        out_specs=rows,
        out_shape=jax.ShapeDtypeStruct((m, D_MODEL), F32),
        compiler_params=_cparams(("parallel",)),
        name="out_ln",
    )(x, merged, wo, g.reshape(1, D_MODEL), b.reshape(1, D_MODEL))


def _conv_prompt_kernel(cb_ref, cc_ref, ch_ref, w_ref, y_ref, st_ref, u_scr):
    u = cc_ref[...] * ch_ref[...]
    u_scr[pl.ds(0, 8), :] = jnp.zeros((8, CONV_WIDTH), F32)
    u_scr[pl.ds(8, SEQ), :] = u
    w = w_ref[...]
    acc = (w[0:1, :] * u_scr[pl.ds(6, SEQ), :] + w[1:2, :] * u_scr[pl.ds(7, SEQ), :]
           + w[2:3, :] * u)
    y_ref[...] = (cb_ref[...] * acc).astype(BF16)
    st_ref[...] = u_scr[pl.ds(8 + SEQ - (CONV_K - 1), CONV_K - 1), :]


def conv_prompt(proj, conv_w):
    blk = lambda c: pl.BlockSpec((SEQ, CONV_WIDTH), lambda b, c=c: (b, c // CONV_WIDTH))
    return pl.pallas_call(
        _conv_prompt_kernel,
        grid=(BATCH,),
        in_specs=[blk(C_CB), blk(C_CC), blk(C_CH),
                  pl.BlockSpec((CONV_K, CONV_WIDTH), lambda b: (0, 0))],
        out_specs=[pl.BlockSpec((SEQ, CONV_WIDTH), lambda b: (b, 0)),
                   pl.BlockSpec((None, CONV_K - 1, CONV_WIDTH), lambda b: (b, 0, 0))],
        out_shape=[jax.ShapeDtypeStruct((M_PROMPT, CONV_WIDTH), BF16),
                   jax.ShapeDtypeStruct((BATCH, CONV_K - 1, CONV_WIDTH), F32)],
        scratch_shapes=[pltpu.VMEM((SEQ + 8, CONV_WIDTH), F32)],
        compiler_params=_cparams(("parallel",)),
        name="conv_prompt",
    )(proj, proj, proj, conv_w)


def _conv_sample_kernel(cb_ref, cc_ref, ch_ref, prev_ref, w_ref, y_ref, st_ref):
    u = cc_ref[...] * ch_ref[...]
    w = w_ref[...]
    p0 = prev_ref[:, 0, :]
    p1 = prev_ref[:, 1, :]
    acc = w[0:1, :] * p0 + w[1:2, :] * p1 + w[2:3, :] * u
    y_ref[...] = (cb_ref[...] * acc).astype(BF16)
    st_ref[:, 0, :] = p1
    st_ref[:, 1, :] = u


def conv_sample(proj, prev, conv_w):
    rb = M_PROMPT // DEC_BATCH
    blk = lambda c: pl.BlockSpec((DEC_BATCH, CONV_WIDTH), lambda i, c=c: (rb, c // CONV_WIDTH))
    full3 = pl.BlockSpec((DEC_BATCH, CONV_K - 1, CONV_WIDTH), lambda i: (0, 0, 0))
    return pl.pallas_call(
        _conv_sample_kernel,
        grid=(1,),
        in_specs=[blk(C_CB), blk(C_CC), blk(C_CH), full3,
                  pl.BlockSpec((CONV_K, CONV_WIDTH), lambda i: (0, 0))],
        out_specs=[pl.BlockSpec((DEC_BATCH, CONV_WIDTH), lambda i: (0, 0)), full3],
        out_shape=[jax.ShapeDtypeStruct((DEC_BATCH, CONV_WIDTH), BF16),
                   jax.ShapeDtypeStruct((DEC_BATCH, CONV_K - 1, CONV_WIDTH), F32)],
        compiler_params=_cparams(("arbitrary",)),
        name="conv_sample",
    )(proj, proj, proj, prev, conv_w)


M_L = 128


def _log_sigmoid(x):
    return jnp.minimum(x, 0.0) - jnp.log1p(jnp.exp(-jnp.abs(x)))


def _head_norm_gate(h, o_pre, gain):
    lane = lax.broadcasted_iota(jnp.int32, h.shape, 1)
    real = lane < M_DV
    mu = jnp.sum(h, axis=-1, keepdims=True) * (1.0 / M_DV)
    hc = jnp.where(real, h - mu, 0.0)
    var = jnp.sum(hc * hc, axis=-1, keepdims=True) * (1.0 / M_DV)
    return _sigmoid(o_pre) * (hc * lax.rsqrt(var + LN_EPS) * gain)


M_TS = 512


def _mlstm_prompt_kernel(q_ref, k_ref, v_ref, o_ref, if_ref, bias_ref, gain_ref,
                         hm_ref, c_out_ref, n_out_ref, m_out_ref, c_scr, n_scr, m_scr):
    step = pl.program_id(1)

    @pl.when(step == 0)
    def _():
        c_scr[...] = jnp.zeros_like(c_scr)
        n_scr[...] = jnp.zeros_like(n_scr)
        m_scr[...] = jnp.zeros_like(m_scr)

    row = lax.broadcasted_iota(jnp.int32, (M_L, M_L), 0)
    col = lax.broadcasted_iota(jnp.int32, (M_L, M_L), 1)
    causal = col <= row
    tri = causal.astype(F32)
    bias = bias_ref[...]

    def chunk(c, carry):
        r0 = pl.multiple_of(c * M_L, M_L)
        x_if = if_ref[pl.ds(r0, M_L), :] + bias
        log_f = _log_sigmoid(x_if)
        cs = jnp.dot(tri, log_f, preferred_element_type=F32, precision=lax.Precision.HIGHEST)
        for hd in range(M_HEADS):
            b_col = cs[:, M_HEADS + hd:M_HEADS + hd + 1]
            i_col = x_if[:, hd:hd + 1]
            zt = jnp.where(col == 0, b_col, jnp.where(col == 1, i_col, 0.0)).T
            b_row = zt[0:1, :]
            i_row = zt[1:2, :]
            m_prev = m_scr[hd]

            d = jnp.where(causal, b_col - b_row + i_row, NEG)
            inter = b_col + m_prev
            m_t = jnp.maximum(jnp.max(d, axis=1, keepdims=True), inter)
            dmat = jnp.exp(d - m_t)
            q = q_ref[pl.ds(r0, M_L), pl.ds(hd * DKP, DKP)]
            k = k_ref[pl.ds(r0, M_L), pl.ds(hd * DKP, DKP)] * (M_DK ** -0.5)
            vb = v_ref[pl.ds(r0, M_L), pl.ds(hd * DVP, DVP)].astype(BF16)
            qb = q.astype(BF16)
            s = lax.dot_general(qb, k.astype(BF16), (((1,), (1,)), ((), ())),
                                preferred_element_type=F32) * dmat
            w_inter = jnp.exp(inter - m_t)
            c_prev = c_scr[hd]
            n_prev = n_scr[hd]
            num = (jnp.dot(s.astype(BF16), vb, preferred_element_type=F32)
                   + w_inter * jnp.dot(qb, c_prev.astype(BF16), preferred_element_type=F32))
            den = (jnp.sum(s, axis=1, keepdims=True)
                   + w_inter * jnp.sum(q * n_prev, axis=1, keepdims=True))
            h = num / jnp.maximum(jnp.abs(den), jnp.exp(-m_t))
            o_pre = o_ref[pl.ds(r0, M_L), pl.ds(hd * DVP, DVP)]
            hm_ref[pl.ds(r0, M_L), pl.ds(hd * DVP, DVP)] = _head_norm_gate(
                h, o_pre, gain_ref[hd]).astype(BF16)

            b_last = b_row[:, M_L - 1:M_L]
            g_row = b_last - b_row + i_row
            m_new = jnp.maximum(b_last + m_prev, jnp.max(g_row, axis=1, keepdims=True))
            w_k = jnp.exp(b_last - b_col + i_col - m_new)
            decay = jnp.exp(b_last + m_prev - m_new)
            kw = k * w_k
            c_scr[hd] = decay * c_prev + lax.dot_general(
                kw.astype(BF16), vb, (((0,), (0,)), ((), ())), preferred_element_type=F32)
            n_scr[hd] = decay * n_prev + jnp.sum(kw, axis=0, keepdims=True)
            m_scr[hd] = m_new
        return carry

    lax.fori_loop(0, M_TS // M_L, chunk, 0)

    @pl.when(step == pl.num_programs(1) - 1)
    def _():
        for hd in range(M_HEADS):
            c_out_ref[hd] = c_scr[hd, pl.ds(0, M_DK), pl.ds(0, M_DV)]
            n_out_ref[hd] = n_scr[hd, :, pl.ds(0, M_DK)]
            m_out_ref[hd] = m_scr[hd]


def mlstm_prompt(proj, bias, gain):
    ns = SEQ // M_TS
    def cblk(c0, w):
        return pl.BlockSpec((M_TS, w), lambda b, s: (b * ns + s, c0 // w))
    return pl.pallas_call(
        _mlstm_prompt_kernel,
        grid=(BATCH, ns),
        in_specs=[cblk(C_MQ, M_HEADS * DKP), cblk(C_MK, M_HEADS * DKP),
                  cblk(C_MV, M_HEADS * DVP), cblk(C_MO, M_HEADS * DVP),
                  cblk(C_IF, LANES),
                  pl.BlockSpec((1, LANES), lambda b, s: (0, 0)),
                  pl.BlockSpec((M_HEADS, 1, DVP), lambda b, s: (0, 0, 0))],
        out_specs=[pl.BlockSpec((M_TS, M_HEADS * DVP), lambda b, s: (b * ns + s, 0)),
                   pl.BlockSpec((None, M_HEADS, M_DK, M_DV), lambda b, s: (b, 0, 0, 0)),
                   pl.BlockSpec((None, M_HEADS, 1, M_DK), lambda b, s: (b, 0, 0, 0)),
                   pl.BlockSpec((None, M_HEADS, 1, 1), lambda b, s: (b, 0, 0, 0))],
        out_shape=[jax.ShapeDtypeStruct((M_PROMPT, M_HEADS * DVP), BF16),
                   jax.ShapeDtypeStruct((BATCH, M_HEADS, M_DK, M_DV), F32),
                   jax.ShapeDtypeStruct((BATCH, M_HEADS, 1, M_DK), F32),
                   jax.ShapeDtypeStruct((BATCH, M_HEADS, 1, 1), F32)],
        scratch_shapes=[pltpu.VMEM((M_HEADS, DKP, DVP), F32), pltpu.VMEM((M_HEADS, 1, DKP), F32),
                        pltpu.VMEM((M_HEADS, 1, 1), F32)],
        compiler_params=_cparams(("parallel", "arbitrary")),
        name="mlstm_prompt",
    )(proj, proj, proj, proj, proj, bias, gain)


MS_DC = 48


def _pick_row(x8, j):
    rows = lax.broadcasted_iota(jnp.int32, x8.shape, 0)
    return jnp.sum(jnp.where(rows == j, x8, 0.0), axis=0, keepdims=True)


def _mlstm_sample_kernel(q_ref, k_ref, v_ref, o_ref, if_ref, bias_ref, gain_ref,
                         c0_ref, n0_ref, m0_ref,
                         hm_ref, c_out_ref, n_out_ref, m_out_ref,
                         qt_scr, kw_scr, vt_scr, acc_scr, st_scr):
    hd = pl.program_id(0)
    c = pl.program_id(1)

    @pl.when(c == 0)
    def _():
        qt = q_ref[...].T
        kt = (k_ref[...] * (M_DK ** -0.5)).T
        vt_scr[...] = v_ref[...].T
        x_if = if_ref[...].T[0:2 * M_HEADS, :] + bias_ref[...]
        i_pre = _pick_row(x_if, hd)
        log_f = _log_sigmoid(_pick_row(x_if, hd + M_HEADS))
        inter = log_f + m0_ref[...]
        m_new = jnp.maximum(i_pre, inter)
        w_k = jnp.exp(i_pre - m_new)
        decay = jnp.exp(inter - m_new)
        n_prev = n0_ref[...]
        s = jnp.sum(qt * kt, axis=0, keepdims=True) * w_k
        den = s + decay * jnp.sum(qt[:M_DK] * n_prev, axis=0, keepdims=True)
        kw = kt * w_k
        qt_scr[...] = qt
        kw_scr[...] = kw
        n_out_ref[...] = decay * n_prev + kw[:M_DK]
        m_out_ref[...] = m_new
        st_scr[0:1, :] = s
        st_scr[1:2, :] = decay
        st_scr[2:3, :] = den
        st_scr[3:4, :] = m_new
        acc_scr[...] = jnp.zeros_like(acc_scr)

    decay = st_scr[1:2, :]
    vt = vt_scr[pl.ds(0, M_DV), :]

    def tile(t, acc):
        r8 = pl.multiple_of(c * MS_DC + t * 8, 8)
        q8 = qt_scr[pl.ds(r8, 8), :]
        kw8 = kw_scr[pl.ds(r8, 8), :]
        for r in range(8):
            c_row = c0_ref[t * 8 + r]
            c_out_ref[t * 8 + r] = decay * c_row + kw8[r:r + 1, :] * vt
            acc = acc + q8[r:r + 1, :] * c_row
        return acc

    acc = lax.fori_loop(0, MS_DC // 8, tile, acc_scr[...])
    acc_scr[...] = acc

    @pl.when(c == pl.num_programs(1) - 1)
    def _():
        s = st_scr[0:1, :]
        den = st_scr[2:3, :]
        m_t = st_scr[3:4, :]
        h = (s * vt + decay * acc) / jnp.maximum(jnp.abs(den), jnp.exp(-m_t))
        mu = jnp.mean(h, axis=0, keepdims=True)
        hc = h - mu
        var = jnp.mean(hc * hc, axis=0, keepdims=True)
        o_pre = o_ref[...].T[:M_DV, :]
        out = _sigmoid(o_pre) * (hc * lax.rsqrt(var + LN_EPS) * gain_ref[...])
        out = jnp.concatenate([out, jnp.zeros((DVP - M_DV, DEC_BATCH), F32)], axis=0)
        hm_ref[...] = out.T.astype(BF16)


def mlstm_sample(proj, bias_col, gain_col, c0t, n0t, m0t, layer):
    rb = M_PROMPT // DEC_BATCH
    nc = M_DK // MS_DC
    def cblk(c0_, w):
        return pl.BlockSpec((DEC_BATCH, w), lambda h, c: (rb, c0_ // w + h))
    return pl.pallas_call(
        _mlstm_sample_kernel,
        grid=(M_HEADS, nc),
        in_specs=[cblk(C_MQ, DKP), cblk(C_MK, DKP), cblk(C_MV, DVP), cblk(C_MO, DVP),
                  pl.BlockSpec((DEC_BATCH, LANES), lambda h, c: (rb, C_IF // LANES)),
                  pl.BlockSpec((2 * M_HEADS, DEC_BATCH), lambda h, c: (0, 0)),
                  pl.BlockSpec((None, M_DV, 1), lambda h, c: (h, 0, 0)),
                  pl.BlockSpec((None, None, MS_DC, M_DV, DEC_BATCH), lambda h, c: (layer, h, c, 0, 0)),
                  pl.BlockSpec((None, None, M_DK, DEC_BATCH), lambda h, c: (layer, h, 0, 0)),
                  pl.BlockSpec((None, None, 1, DEC_BATCH), lambda h, c: (layer, h, 0, 0))],
        out_specs=[pl.BlockSpec((DEC_BATCH, DVP), lambda h, c: (0, h)),
                   pl.BlockSpec((None, MS_DC, M_DV, DEC_BATCH), lambda h, c: (h, c, 0, 0)),
                   pl.BlockSpec((None, M_DK, DEC_BATCH), lambda h, c: (h, 0, 0)),
                   pl.BlockSpec((None, 1, DEC_BATCH), lambda h, c: (h, 0, 0))],
        out_shape=[jax.ShapeDtypeStruct((DEC_BATCH, M_HEADS * DVP), BF16),
                   jax.ShapeDtypeStruct((M_HEADS, M_DK, M_DV, DEC_BATCH), F32),
                   jax.ShapeDtypeStruct((M_HEADS, M_DK, DEC_BATCH), F32),
                   jax.ShapeDtypeStruct((M_HEADS, 1, DEC_BATCH), F32)],
        scratch_shapes=[pltpu.VMEM((DKP, DEC_BATCH), F32), pltpu.VMEM((DKP, DEC_BATCH), F32),
                        pltpu.VMEM((DVP, DEC_BATCH), F32), pltpu.VMEM((M_DV, DEC_BATCH), F32),
                        pltpu.VMEM((8, DEC_BATCH), F32)],
        compiler_params=_cparams(("arbitrary", "arbitrary")),
        name="mlstm_sample",
    )(proj, proj, proj, proj, proj, bias_col, gain_col, c0t, n0t, m0t)


A_Q = 128
A_LT = A_GW // LANES


def _rope(x, cos, sin_signed):
    lane = lax.broadcasted_iota(jnp.int32, x.shape, 1)
    first_half = (lane % A_HEAD_DIM) < (A_HEAD_DIM // 2)
    partner = jnp.where(first_half, pltpu.roll(x, x.shape[1] - A_HEAD_DIM // 2, 1),
                        pltpu.roll(x, A_HEAD_DIM // 2, 1))
    return x * cos + partner * sin_signed


def _head_masks(shape):
    lane = lax.broadcasted_iota(jnp.int32, shape, 1)
    return [(lane // A_HEAD_DIM) == h for h in range(A_HPG)]


def _attn_group_prompt(dil, gi, qs_scr, ks_scr, vs_scr, o_scr, l_scr):
    length = SEQ // dil
    nb = length // A_Q
    row = lax.broadcasted_iota(jnp.int32, (A_Q, A_Q), 0)
    col = lax.broadcasted_iota(jnp.int32, (A_Q, A_Q), 1)
    cur_ok = col <= row
    prev_ok = col >= row
    masks = _head_masks((A_Q, A_GW))
    nt = (((1,), (1,)), ((), ()))

    def window(start):
        if dil == 1:
            return pl.ds(pl.multiple_of(start, A_Q), A_Q)
        return pl.ds(start, A_Q, stride=dil)

    def rows(scr, start):
        w = window(start)
        return jnp.concatenate([scr[t, w, :] for t in range(A_LT)], axis=1).astype(BF16)

    def block(idx, carry):
        r = idx % dil
        n = idx // dil
        base = r + (dil * A_Q) * n
        qb = rows(qs_scr, base)
        kc = rows(ks_scr, base)
        vc = rows(vs_scr, base)
        if nb > 1:
            pbase = jnp.maximum(base - dil * A_Q, r)
            kp = rows(ks_scr, pbase)
            vp = rows(vs_scr, pbase)
            has_prev = n > 0
        o_acc = jnp.zeros((A_Q, A_GW), F32)
        l_acc = jnp.zeros((A_Q, A_GW), F32)
        for h in range(A_HPG):
            qh = jnp.where(masks[h], qb, jnp.zeros_like(qb))
            s_c = jnp.where(cur_ok, lax.dot_general(qh, kc, nt, preferred_element_type=F32), NEG)
            m = jnp.max(s_c, axis=1, keepdims=True)
            if nb > 1:
                s_p = jnp.where(jnp.logical_and(prev_ok, has_prev),
                                lax.dot_general(qh, kp, nt, preferred_element_type=F32), NEG)
                m = jnp.maximum(m, jnp.max(s_p, axis=1, keepdims=True))
            p_c = jnp.exp(s_c - m)
            l = jnp.sum(p_c, axis=1, keepdims=True)
            o_h = jnp.dot(p_c.astype(BF16), vc, preferred_element_type=F32)
            if nb > 1:
                p_p = jnp.exp(s_p - m)
                l = l + jnp.sum(p_p, axis=1, keepdims=True)
                o_h = o_h + jnp.dot(p_p.astype(BF16), vp, preferred_element_type=F32)
            o_acc = o_acc + jnp.where(masks[h], o_h / l, 0.0)
            l_acc = l_acc + jnp.where(masks[h], m + jnp.log(l), 0.0)
        w = window(base)
        for t in range(A_LT):
            o_scr[gi, t, w, :] = o_acc[:, t * LANES:(t + 1) * LANES]
            l_scr[gi, t, w, :] = l_acc[:, t * LANES:(t + 1) * LANES]
        return carry

    lax.fori_loop(0, dil * nb, block, 0)


def _attn_prompt_kernel(q_ref, k_ref, v_ref, cos_ref, sin_ref, oa_ref, kv0_ref, kv1_ref, kv2_ref,
                        qs_scr, ks_scr, vs_scr, o_scr, l_scr):
    g = pl.program_id(1)
    rc = 256

    def rope_rows(c, carry):
        sl = pl.ds(pl.multiple_of(c * rc, rc), rc)
        cos = cos_ref[sl, :]
        sin = sin_ref[sl, :]
        for t in range(A_LT):
            lanes = pl.ds(t * LANES, LANES)
            qs_scr[t, sl, :] = _rope(q_ref[sl, lanes], cos, sin) * (A_HEAD_DIM ** -0.5)
            ks_scr[t, sl, :] = _rope(k_ref[sl, lanes], cos, sin)
            vs_scr[t, sl, :] = v_ref[sl, lanes]
        return carry

    lax.fori_loop(0, SEQ // rc, rope_rows, 0)

    for gi, ((win, dil), kv_ref) in enumerate(zip(A_GROUPS, (kv0_ref, kv1_ref, kv2_ref))):
        @pl.when(g == gi)
        def _(gi=gi, dil=dil, win=win, kv_ref=kv_ref):
            _attn_group_prompt(dil, gi, qs_scr, ks_scr, vs_scr, o_scr, l_scr)
            keep = min(win, SEQ)
            for c in range(keep // LANES):
                rows = pl.ds(SEQ - keep + c * LANES, LANES)
                for t in range(A_LT):
                    kv_ref[0, pl.ds(t * LANES, LANES), pl.ds(c * LANES, LANES)] = ks_scr[t, rows, :].T
                    kv_ref[1, pl.ds(t * LANES, LANES), pl.ds(c * LANES, LANES)] = vs_scr[t, rows, :].T

    @pl.when(g == len(A_GROUPS) - 1)
    def _():
        def comb(c, carry):
            sl = pl.ds(pl.multiple_of(c * rc, rc), rc)
            for t in range(A_LT):
                l0, l1, l2 = l_scr[0, t, sl, :], l_scr[1, t, sl, :], l_scr[2, t, sl, :]
                mx = jnp.maximum(jnp.maximum(l0, l1), l2)
                e0, e1, e2 = jnp.exp(l0 - mx), jnp.exp(l1 - mx), jnp.exp(l2 - mx)
                tot = e0 * o_scr[0, t, sl, :] + e1 * o_scr[1, t, sl, :] + e2 * o_scr[2, t, sl, :]
                oa_ref[sl, pl.ds(t * LANES, LANES)] = (tot / (e0 + e1 + e2)).astype(BF16)
            return carry
        lax.fori_loop(0, SEQ // rc, comb, 0)


def attn_prompt(proj, cos, sin):
    def gblk(c0):
        return pl.BlockSpec((SEQ, A_GW), lambda b, g: (b, c0 // A_GW + g))
    tab = pl.BlockSpec((SEQ, LANES), lambda b, g: (0, 0))
    keeps = [min(win, SEQ) for win, _ in A_GROUPS]
    return pl.pallas_call(
        _attn_prompt_kernel,
        grid=(BATCH, len(A_GROUPS)),
        in_specs=[gblk(C_AQ), gblk(C_AK), gblk(C_AV), tab, tab],
        out_specs=[pl.BlockSpec((SEQ, A_GW), lambda b, g: (b, 0))]
                  + [pl.BlockSpec((None, 2, A_GW, kp), lambda b, g: (b, 0, 0, 0)) for kp in keeps],
        out_shape=[jax.ShapeDtypeStruct((M_PROMPT, A_GW), BF16)]
                  + [jax.ShapeDtypeStruct((BATCH, 2, A_GW, kp), F32) for kp in keeps],
        scratch_shapes=[pltpu.VMEM((A_LT, SEQ, LANES), F32),
                        pltpu.VMEM((A_LT, SEQ, LANES), F32),
                        pltpu.VMEM((A_LT, SEQ, LANES), F32),
                        pltpu.VMEM((len(A_GROUPS), A_LT, SEQ, LANES), F32),
                        pltpu.VMEM((len(A_GROUPS), A_LT, SEQ, LANES), F32)],
        compiler_params=_cparams(("parallel", "arbitrary")),
        name="attn_prompt",
    )(proj, proj, proj, cos, sin)


AS_BB = 2


def _attn_sample_kernel(q_ref, k_ref, v_ref, cos_ref, sin_ref, c0_ref, c1_ref, c2_ref,
                        oa_ref, kt_ref, vt_ref, qt_scr, s0_scr, ot_scr, lt_scr):
    i = pl.program_id(0)
    lane_b = lax.broadcasted_iota(jnp.int32, (1, DEC_BATCH), 1)
    sub8 = lax.broadcasted_iota(jnp.int32, (8, DEC_BATCH), 0)

    @pl.when(i == 0)
    def _():
        cos = cos_ref[...]
        sin = sin_ref[...]
        for gi in range(len(A_GROUPS)):
            gs = pl.ds(gi * A_GW, A_GW)
            qt = (_rope(q_ref[:, gs], cos, sin) * (A_HEAD_DIM ** -0.5)).T
            kt = _rope(k_ref[:, gs], cos, sin).T
            qt_scr[gi] = qt
            kt_ref[gi] = kt
            vt_ref[gi] = v_ref[:, gs].T
            prod = qt * kt
            s0 = jnp.zeros((8, DEC_BATCH), F32)
            for h in range(A_HPG):
                part = jnp.sum(prod[h * A_HEAD_DIM:(h + 1) * A_HEAD_DIM], axis=0, keepdims=True)
                s0 = jnp.where(sub8 == h, part, s0)
            s0_scr[gi] = s0
        ot_scr[...] = jnp.zeros_like(ot_scr)
        lt_scr[...] = jnp.zeros_like(lt_scr)

    for bl in range(AS_BB):
        pick = lane_b == i * AS_BB + bl
        for gi, (cache_ref, (_, dil)) in enumerate(zip((c0_ref, c1_ref, c2_ref), A_GROUPS)):
            wb = cache_ref.shape[-1]
            if dil > 1:
                keep = (lax.broadcasted_iota(jnp.int32, (1, wb), 1) & (dil - 1)) == 0
            s0_all = s0_scr[gi]
            for h in range(A_HPG):
                hs = pl.ds(h * A_HEAD_DIM, A_HEAD_DIM)
                q_col = jnp.sum(jnp.where(pick, qt_scr[gi, hs, :], 0.0), axis=1, keepdims=True)
                v_col = jnp.sum(jnp.where(pick, vt_ref[gi, hs, :], 0.0), axis=1, keepdims=True)
                s0 = jnp.sum(jnp.where(pick, s0_all[h:h + 1, :], 0.0), axis=1, keepdims=True)
                s = jnp.sum(q_col * cache_ref[bl, 0, h], axis=0, keepdims=True)
                if dil > 1:
                    s = jnp.where(keep, s, NEG)
                m = jnp.maximum(jnp.max(s, axis=1, keepdims=True), s0)
                p = jnp.exp(s - m)
                p0 = jnp.exp(s0 - m)
                l = jnp.sum(p, axis=1, keepdims=True) + p0
                o = (jnp.sum(p * cache_ref[bl, 1, h], axis=1, keepdims=True) + p0 * v_col) / l
                ot_scr[gi, hs, :] = jnp.where(pick, o, ot_scr[gi, hs, :])
                lt_scr[gi, pl.ds(h, 1), :] = jnp.where(pick, m + jnp.log(l), lt_scr[gi, pl.ds(h, 1), :])

    @pl.when(i == pl.num_programs(0) - 1)
    def _():
        for h in range(A_HPG):
            hs = pl.ds(h * A_HEAD_DIM, A_HEAD_DIM)
            l0, l1, l2 = (lt_scr[gi, pl.ds(h, 1), :] for gi in range(3))
            mx = jnp.maximum(jnp.maximum(l0, l1), l2)
            e0, e1, e2 = jnp.exp(l0 - mx), jnp.exp(l1 - mx), jnp.exp(l2 - mx)
            tot = e0 * ot_scr[0, hs, :] + e1 * ot_scr[1, hs, :] + e2 * ot_scr[2, hs, :]
            ot_scr[0, hs, :] = tot / (e0 + e1 + e2)
        oa_ref[...] = ot_scr[0].T.astype(BF16)


def attn_sample(proj, cos, sin, caches_t, layer):
    rb = M_PROMPT // DEC_BATCH
    def pblk(c0):
        return pl.BlockSpec((DEC_BATCH, A_WIDTH), lambda i: (rb, c0 // A_WIDTH))
    tab = pl.BlockSpec((1, A_GW), lambda i: (0, 0))
    cache_specs = [pl.BlockSpec((None, AS_BB, 2, A_HPG, A_HEAD_DIM, ct.shape[-1]),
                                lambda i: (layer, i, 0, 0, 0, 0)) for ct in caches_t]
    ng = len(A_GROUPS)
    full3 = pl.BlockSpec((ng, A_GW, DEC_BATCH), lambda i: (0, 0, 0))
    return pl.pallas_call(
        _attn_sample_kernel,
        grid=(DEC_BATCH // AS_BB,),
        in_specs=[pblk(C_AQ), pblk(C_AK), pblk(C_AV), tab, tab] + cache_specs,
        out_specs=[pl.BlockSpec((DEC_BATCH, A_GW), lambda i: (0, 0)), full3, full3],
        out_shape=[jax.ShapeDtypeStruct((DEC_BATCH, A_GW), BF16),
                   jax.ShapeDtypeStruct((ng, A_GW, DEC_BATCH), F32),
                   jax.ShapeDtypeStruct((ng, A_GW, DEC_BATCH), F32)],
        scratch_shapes=[pltpu.VMEM((ng, A_GW, DEC_BATCH), F32), pltpu.VMEM((ng, 8, DEC_BATCH), F32),
                        pltpu.VMEM((ng, A_GW, DEC_BATCH), F32), pltpu.VMEM((ng, 8, DEC_BATCH), F32)],
        compiler_params=_cparams(("arbitrary",)),
        name="attn_sample",
    )(proj, proj, proj, cos, sin, *caches_t)


def _pad_heads(w, d, dp):
    r = w.shape[0]
    w = w.reshape(r, M_HEADS, d)
    return jnp.pad(w, ((0, 0), (0, 0), (0, dp - d))).reshape(r, M_HEADS * dp)


def _layer_weights(w_in_l, w_up_m, w_up_c, w_up_a, w_o_l):
    o = IN_OFFSETS
    piece = lambda i: w_in_l[:, o[i]:o[i + 1]]
    mq, mk, mv, mi, mf, mo, cb, cc, ch, aq, ak, av, gt = [piece(i) for i in range(13)]
    w_br = jnp.concatenate(
        [_pad_heads(mv, M_DV, DVP), _pad_heads(mo, M_DV, DVP),
         _pad_heads(mq, M_DK, DKP), _pad_heads(mk, M_DK, DKP),
         cb, cc, ch, aq, ak, av,
         mi, mf,jnp.zeros((D_MODEL, PROJ_W - C_IF - 2 * M_HEADS), F32)], axis=1).astype(BF16)
    w_um = jnp.pad(w_up_m.reshape(M_HEADS, M_DV, D_MODEL),
                   ((0, 0), (0, DVP - M_DV), (0, 0))).reshape(M_HEADS * DVP, D_MODEL)
    return (w_br, gt.astype(BF16), w_um.astype(BF16), w_up_c.astype(BF16),
            w_up_a.astype(BF16), w_o_l.astype(BF16))


def _rope_tables(pos):
    half = A_HEAD_DIM // 2
    inv = ROPE_THETA ** (-(2.0 * jnp.arange(half, dtype=F32)) / A_HEAD_DIM)
    ang = pos.astype(F32)[:, None] * inv[None, :]
    cos = jnp.cos(ang)
    sin = jnp.sin(ang)
    cos = jnp.tile(jnp.concatenate([cos, cos], axis=-1), (1, A_HPG))
    sin = jnp.tile(jnp.concatenate([-sin, sin], axis=-1), (1, A_HPG))
    return cos, sin


def kernel(x_prompt, x_sample, state_mlstm_C, state_mlstm_n, state_mlstm_m, state_conv,
           cache_attn_kv_w128, cache_attn_kv_w512, cache_attn_kv_w2048,
           w_in, b_gate_if, mlstm_norm_g, conv_w, w_up_mlstm, w_up_conv, w_up_attn, w_o,
           w_ffn_in, w_ffn_out, ln_g, ln_b):
    x = jnp.concatenate([x_prompt.reshape(M_PROMPT, D_MODEL),
                         x_sample.reshape(DEC_BATCH, D_MODEL)], axis=0)
    cos_p, sin_p = _rope_tables(jnp.arange(SEQ))
    cos_s, sin_s = _rope_tables(PAST_LEN + jnp.arange(1))
    w_ffn_in_b = w_ffn_in.astype(BF16)
    w_ffn_out_b = w_ffn_out.astype(BF16)

    c0t = jnp.transpose(state_mlstm_C, (0, 2, 3, 4, 1))
    n0t = jnp.transpose(state_mlstm_n, (0, 2, 3, 1))
    m0t = jnp.transpose(state_mlstm_m, (0, 2, 1)).reshape(DEPTH, M_HEADS, 1, DEC_BATCH)
    caches_t = [jnp.transpose(c, (0, 1, 3, 4, 5, 2))
                for c in (cache_attn_kv_w128, cache_attn_kv_w512, cache_attn_kv_w2048)]

    p_states, s_states = [], []
    for l in range(DEPTH):
        w_br, w_g, w_um, w_uc, w_ua, w_ol = _layer_weights(
            w_in[l], w_up_mlstm[l], w_up_conv[l], w_up_attn[l], w_o[l])
        bias = jnp.pad(b_gate_if[l], (0, LANES - 2 * M_HEADS)).reshape(1, LANES)
        bias_col = jnp.broadcast_to(b_gate_if[l][:, None], (2 * M_HEADS, DEC_BATCH))
        gain = jnp.pad(mlstm_norm_g[l].reshape(M_HEADS, M_DV), ((0, 0), (0, DVP - M_DV)))
        gain_col = mlstm_norm_g[l].reshape(M_HEADS, M_DV, 1)

        x, xb = ffn_ln(x, w_ffn_in_b, w_ffn_out_b, ln_g[l, 0], ln_b[l, 0], l, 0)
        proj = branch_proj(xb, w_br)

        hm_p, pc, pn, pm = mlstm_prompt(proj, bias, gain.reshape(M_HEADS, 1, DVP))
        hm_s, sct, snt, smt = mlstm_sample(proj, bias_col, gain_col, c0t, n0t, m0t, l)
        yc_p, pconv = conv_prompt(proj, conv_w[l])
        yc_s, sconv = conv_sample(proj, state_conv[l], conv_w[l])
        oa_p, *kv_p = attn_prompt(proj, cos_p[:, :LANES], sin_p[:, :LANES])
        oa_s, kt_s, vt_s = attn_sample(proj, cos_s, sin_s, caches_t, l)

        hm = jnp.concatenate([hm_p, hm_s], axis=0)
        yc = jnp.concatenate([yc_p, yc_s], axis=0)
        oa = jnp.concatenate([oa_p, oa_s], axis=0)
        x = merge_ln(x, xb, hm, yc, oa, w_g, w_um, w_uc, w_ua, w_ol, ln_g[l, 1], ln_b[l, 1])
        x, _ = ffn_ln(x, w_ffn_in_b, w_ffn_out_b, ln_g[l, 2], ln_b[l, 2], l, 1)

        kt_s = kt_s.reshape(3, A_HPG, A_HEAD_DIM, DEC_BATCH)
        vt_s = vt_s.reshape(3, A_HPG, A_HEAD_DIM, DEC_BATCH)
        kv_s = [jnp.stack([kt_s[gi], vt_s[gi]], axis=0) for gi in range(3)]
        p_states.append((pc, pn.reshape(BATCH, M_HEADS, M_DK), pm.reshape(BATCH, M_HEADS), pconv,
                         kv_p[0], kv_p[1], kv_p[2]))
        s_states.append((sct, snt, smt, sconv, kv_s[0], kv_s[1], kv_s[2]))

    y_prompt = x[:M_PROMPT].reshape(BATCH, SEQ, D_MODEL)
    y_sample = x[M_PROMPT:].reshape(DEC_BATCH, 1, D_MODEL)
    p_out = [jnp.stack(z) for z in zip(*p_states)]
    for j in range(4, 7):
        kvt = p_out[j]
        p_out[j] = jnp.transpose(kvt.reshape(DEPTH, BATCH, 2, A_HPG, A_HEAD_DIM, kvt.shape[-1]),
                                 (0, 1, 5, 2, 3, 4))
    sct, snt, smt, sconv, kv0, kv1, kv2 = [jnp.stack(z) for z in zip(*s_states)]
    s_out = [jnp.transpose(sct, (0, 4, 1, 2, 3)),
             jnp.transpose(snt, (0, 3, 1, 2)),
             jnp.transpose(smt.reshape(DEPTH, M_HEADS, DEC_BATCH), (0, 2, 1)),
             sconv]
    s_out += [jnp.transpose(kv, (0, 4, 1, 2, 3)).reshape(DEPTH, DEC_BATCH, 1, 2, A_HPG, A_HEAD_DIM)
              for kv in (kv0, kv1, kv2)]
    return (y_prompt, y_sample, *p_out, *s_out)
```

```python
import functools
import math

import jax
import jax.numpy as jnp
import numpy as np
from jax import lax
from jax.experimental import pallas as pl
from jax.experimental.pallas import tpu as pltpu

F32 = jnp.float32
BF16 = jnp.bfloat16

D_MODEL = 2048
BATCH = 4
SEQ = 2048
DEPTH = 2
DEC_BATCH = 128
PAST_LEN = 2048
M_HEADS = 4
M_DV = 192
M_DK = 96
M_QK = M_HEADS * M_DK
M_WIDTH = M_HEADS * M_DV
CONV_WIDTH = 512
CONV_K = 3
A_GROUPS = ((128, 1), (512, 4), (2048, 16))
A_HPG = 4
A_HEAD_DIM = 64
A_GW = A_HPG * A_HEAD_DIM
A_WIDTH = 3 * A_GW
ROPE_THETA = 10000.0
N_BRANCH = 3
D_FF = 5632
LN_EPS = 1e-5
ALPHA = (2 * DEPTH) ** 0.25
IN_SIZES = (M_QK, M_QK, M_WIDTH, M_HEADS, M_HEADS, M_WIDTH,
            CONV_WIDTH, CONV_WIDTH, CONV_WIDTH,
            A_WIDTH, A_WIDTH, A_WIDTH, N_BRANCH * D_MODEL)
IN_OFFSETS = tuple(int(o) for o in np.cumsum((0,) + IN_SIZES))

M_PROMPT = BATCH * SEQ
M_ROWS = M_PROMPT + DEC_BATCH

LANES = 128
DKP = 128
DVP = 256
VMEM_LIMIT = 52 * 1024 * 1024

C_MV, C_MO = 0, 1024
C_MQ, C_MK = 2048, 2560
C_CB, C_CC, C_CH = 3072, 3584, 4096
C_AQ, C_AK, C_AV = 4608, 5376, 6144
C_IF = 6912
PROJ_W = 7168

NEG = -1e30


def _sigmoid(x):
    return 1.0 / (1.0 + jnp.exp(-x))


def _layer_norm(z, g, b):
    mu = jnp.mean(z, axis=-1, keepdims=True)
    zc = z - mu
    var = jnp.mean(zc * zc, axis=-1, keepdims=True)
    return zc * lax.rsqrt(var + LN_EPS) * g + b


def _cparams(sem):
    return pltpu.CompilerParams(dimension_semantics=sem, vmem_limit_bytes=VMEM_LIMIT)


FFN_TM = 640
FFN_TF = 512


def _ffn_kernel(x_ref, wa_ref, wb_ref, wo_ref, g_ref, b_ref, y_ref, yb_ref, xb_scr):
    f = pl.program_id(1)

    @pl.when(f == 0)
    def _():
        xb_scr[...] = x_ref[...].astype(BF16)
        y_ref[...] = jnp.zeros_like(y_ref)

    xb = xb_scr[...]
    a = jnp.dot(xb, wa_ref[...], preferred_element_type=F32)
    b = jnp.dot(xb, wb_ref[...], preferred_element_type=F32)
    h = (a * _sigmoid(a)) * b
    y_ref[...] += jnp.dot(h.astype(BF16), wo_ref[...], preferred_element_type=F32)

    @pl.when(f == pl.num_programs(1) - 1)
    def _():
        z = ALPHA * x_ref[...] + 0.5 * y_ref[...]
        out = _layer_norm(z, g_ref[...], b_ref[...])
        y_ref[...] = out
        yb_ref[...] = out.astype(BF16)


def ffn_ln(x, w_in, w_out, g, b, layer, which):
    m = x.shape[0]
    nf = D_FF // FFN_TF
    return pl.pallas_call(
        _ffn_kernel,
        grid=(m // FFN_TM, nf),
        in_specs=[
            pl.BlockSpec((FFN_TM, D_MODEL), lambda i, f: (i, 0)),
            pl.BlockSpec((None, None, D_MODEL, FFN_TF), lambda i, f: (layer, which, 0, f)),
            pl.BlockSpec((None, None, D_MODEL, FFN_TF), lambda i, f: (layer, which, 0, f + nf)),
            pl.BlockSpec((None, None, FFN_TF, D_MODEL), lambda i, f: (layer, which, f, 0)),
            pl.BlockSpec((1, D_MODEL), lambda i, f: (0, 0)),
            pl.BlockSpec((1, D_MODEL), lambda i, f: (0, 0)),
        ],
        out_specs=[
            pl.BlockSpec((FFN_TM, D_MODEL), lambda i, f: (i, 0)),
            pl.BlockSpec((FFN_TM, D_MODEL), lambda i, f: (i, 0)),
        ],
        out_shape=[jax.ShapeDtypeStruct((m, D_MODEL), F32),
                   jax.ShapeDtypeStruct((m, D_MODEL), BF16)],
        scratch_shapes=[pltpu.VMEM((FFN_TM, D_MODEL), BF16)],
        compiler_params=_cparams(("parallel", "arbitrary")),
        name="ffn_ln",
    )(x, w_in, w_in, w_out, g.reshape(1, D_MODEL), b.reshape(1, D_MODEL))


PROJ_TM = 1040
PROJ_TN = 1024


NT_DIMS = (((1,), (1,)), ((), ()))


def _proj_kernel(xb_ref, wt_ref, o_ref):
    o_ref[...] = lax.dot_general(xb_ref[...], wt_ref[...], NT_DIMS, preferred_element_type=F32)


def branch_proj(xb, wt):
    m = xb.shape[0]
    return pl.pallas_call(
        _proj_kernel,
        grid=(m // PROJ_TM, PROJ_W // PROJ_TN),
        in_specs=[pl.BlockSpec((PROJ_TM, D_MODEL), lambda i, j: (i, 0)),
                  pl.BlockSpec((PROJ_TN, D_MODEL), lambda i, j: (j, 0))],
        out_specs=pl.BlockSpec((PROJ_TM, PROJ_TN), lambda i, j: (i, j)),
        out_shape=jax.ShapeDtypeStruct((m, PROJ_W), F32),
        compiler_params=_cparams(("parallel", "parallel")),
        name="branch_proj",
    )(xb, wt)


MRG_TM = 640
MRG_TN = 512


def _gate_up_kernel(xb_ref, hm_ref, yc_ref, oa_ref, wg0_ref, wg1_ref, wg2_ref,
                    wum_ref, wuc_ref, wua_ref, o_ref):
    xb = xb_ref[...]

    def gated(wg_ref, br_ref, wu_ref):
        gate = _sigmoid(lax.dot_general(xb, wg_ref[...], NT_DIMS, preferred_element_type=F32))
        return gate * jnp.dot(br_ref[...], wu_ref[...], preferred_element_type=F32)

    merged = (gated(wg0_ref, hm_ref, wum_ref) + gated(wg1_ref, yc_ref, wuc_ref)
              + gated(wg2_ref, oa_ref, wua_ref))
    o_ref[...] = merged.astype(BF16)


def _out_ln_kernel(x_ref, mg_ref, wo_ref, g_ref, b_ref, y_ref):
    z = ALPHA * x_ref[...] + jnp.dot(mg_ref[...], wo_ref[...], preferred_element_type=F32)
    y_ref[...] = _layer_norm(z, g_ref[...], b_ref[...])


def merge_ln(x, xb, hm, yc, oa, wg, wum, wuc, wua, wo, g, b):
    m = x.shape[0]
    nn = D_MODEL // MRG_TN
    row = lambda w: pl.BlockSpec((MRG_TM, w), lambda i, n: (i, 0))
    merged = pl.pallas_call(
        _gate_up_kernel,
        grid=(m // MRG_TM, nn),
        in_specs=[
            row(D_MODEL), row(M_HEADS * DVP), row(CONV_WIDTH), row(A_GW),
            pl.BlockSpec((MRG_TN, D_MODEL), lambda i, n: (n, 0)),
            pl.BlockSpec((MRG_TN, D_MODEL), lambda i, n: (n + nn, 0)),
            pl.BlockSpec((MRG_TN, D_MODEL), lambda i, n: (n + 2 * nn, 0)),
            pl.BlockSpec((M_HEADS * DVP, MRG_TN), lambda i, n: (0, n)),
            pl.BlockSpec((CONV_WIDTH, MRG_TN), lambda i, n: (0, n)),
            pl.BlockSpec((A_GW, MRG_TN), lambda i, n: (0, n)),
        ],
        out_specs=pl.BlockSpec((MRG_TM, MRG_TN), lambda i, n: (i, n)),
        out_shape=jax.ShapeDtypeStruct((m, D_MODEL), BF16),
        compiler_params=_cparams(("parallel", "parallel")),
        name="gate_up",
    )(xb, hm, yc, oa, wg, wg, wg, wum, wuc, wua)
    rows = pl.BlockSpec((MRG_TM, D_MODEL), lambda i: (i, 0))
    vec = pl.BlockSpec((1, D_MODEL), lambda i: (0, 0))
    return pl.pallas_call(
        _out_ln_kernel,
        grid=(m // MRG_TM,),
        in_specs=[rows, rows, pl.BlockSpec((D_MODEL, D_MODEL), lambda i: (0, 0)), vec, vec],
        out_specs=rows,
        out_shape=jax.ShapeDtypeStruct((m, D_MODEL), F32),
        compiler_params=_cparams(("parallel",)),
        name="out_ln",
    )(x, merged, wo, g.reshape(1, D_MODEL), b.reshape(1, D_MODEL))


def _conv_prompt_kernel(cb_ref, cc_ref, ch_ref, w_ref, y_ref, st_ref, u_scr):
    u = cc_ref[...] * ch_ref[...]
    u_scr[pl.ds(0, 8), :] = jnp.zeros((8, CONV_WIDTH), F32)
    u_scr[pl.ds(8, SEQ), :] = u
    w = w_ref[...]
    acc = (w[0:1, :] * u_scr[pl.ds(6, SEQ), :] + w[1:2, :] * u_scr[pl.ds(7, SEQ), :]
           + w[2:3, :] * u)
    y_ref[...] = (cb_ref[...] * acc).astype(BF16)
    st_ref[...] = u_scr[pl.ds(8 + SEQ - (CONV_K - 1), CONV_K - 1), :]


def conv_prompt(proj, conv_w):
    blk = lambda c: pl.BlockSpec((SEQ, CONV_WIDTH), lambda b, c=c: (b, c // CONV_WIDTH))
    return pl.pallas_call(
        _conv_prompt_kernel,
        grid=(BATCH,),
        in_specs=[blk(C_CB), blk(C_CC), blk(C_CH),
                  pl.BlockSpec((CONV_K, CONV_WIDTH), lambda b: (0, 0))],
        out_specs=[pl.BlockSpec((SEQ, CONV_WIDTH), lambda b: (b, 0)),
                   pl.BlockSpec((None, CONV_K - 1, CONV_WIDTH), lambda b: (b, 0, 0))],
        out_shape=[jax.ShapeDtypeStruct((M_PROMPT, CONV_WIDTH), BF16),
                   jax.ShapeDtypeStruct((BATCH, CONV_K - 1, CONV_WIDTH), F32)],
        scratch_shapes=[pltpu.VMEM((SEQ + 8, CONV_WIDTH), F32)],
        compiler_params=_cparams(("parallel",)),
        name="conv_prompt",
    )(proj, proj, proj, conv_w)


def _conv_sample_kernel(cb_ref, cc_ref, ch_ref, prev_ref, w_ref, y_ref, st_ref):
    u = cc_ref[...] * ch_ref[...]
    w = w_ref[...]
    p0 = prev_ref[:, 0, :]
    p1 = prev_ref[:, 1, :]
    acc = w[0:1, :] * p0 + w[1:2, :] * p1 + w[2:3, :] * u
    y_ref[...] = (cb_ref[...] * acc).astype(BF16)
    st_ref[:, 0, :] = p1
    st_ref[:, 1, :] = u


def conv_sample(proj, prev, conv_w):
    rb = M_PROMPT // DEC_BATCH
    blk = lambda c: pl.BlockSpec((DEC_BATCH, CONV_WIDTH), lambda i, c=c: (rb, c // CONV_WIDTH))
    full3 = pl.BlockSpec((DEC_BATCH, CONV_K - 1, CONV_WIDTH), lambda i: (0, 0, 0))
    return pl.pallas_call(
        _conv_sample_kernel,
        grid=(1,),
        in_specs=[blk(C_CB), blk(C_CC), blk(C_CH), full3,
                  pl.BlockSpec((CONV_K, CONV_WIDTH), lambda i: (0, 0))],
        out_specs=[pl.BlockSpec((DEC_BATCH, CONV_WIDTH), lambda i: (0, 0)), full3],
        out_shape=[jax.ShapeDtypeStruct((DEC_BATCH, CONV_WIDTH), BF16),
                   jax.ShapeDtypeStruct((DEC_BATCH, CONV_K - 1, CONV_WIDTH), F32)],
        compiler_params=_cparams(("arbitrary",)),
        name="conv_sample",
    )(proj, proj, proj, prev, conv_w)


M_L = 128


def _log_sigmoid(x):
    return jnp.minimum(x, 0.0) - jnp.log1p(jnp.exp(-jnp.abs(x)))


def _head_norm_gate(h, o_pre, gain):
    lane = lax.broadcasted_iota(jnp.int32, h.shape, 1)
    real = lane < M_DV
    mu = jnp.sum(h, axis=-1, keepdims=True) * (1.0 / M_DV)
    hc = jnp.where(real, h - mu, 0.0)
    var = jnp.sum(hc * hc, axis=-1, keepdims=True) * (1.0 / M_DV)
    return _sigmoid(o_pre) * (hc * lax.rsqrt(var + LN_EPS) * gain)


M_TS = 512


def _mlstm_prompt_kernel(q_ref, k_ref, v_ref, o_ref, if_ref, bias_ref, gain_ref,
                         hm_ref, c_out_ref, n_out_ref, m_out_ref, c_scr, n_scr, m_scr):
    step = pl.program_id(1)

    @pl.when(step == 0)
    def _():
        c_scr[...] = jnp.zeros_like(c_scr)
        n_scr[...] = jnp.zeros_like(n_scr)
        m_scr[...] = jnp.zeros_like(m_scr)

    row = lax.broadcasted_iota(jnp.int32, (M_L, M_L), 0)
    col = lax.broadcasted_iota(jnp.int32, (M_L, M_L), 1)
    causal = col <= row
    tri = causal.astype(F32)
    bias = bias_ref[...]

    def chunk(c, carry):
        r0 = pl.multiple_of(c * M_L, M_L)
        x_if = if_ref[pl.ds(r0, M_L), :] + bias
        log_f = _log_sigmoid(x_if)
        cs = jnp.dot(tri, log_f, preferred_element_type=F32, precision=lax.Precision.HIGHEST)
        for hd in range(M_HEADS):
            b_col = cs[:, M_HEADS + hd:M_HEADS + hd + 1]
            i_col = x_if[:, hd:hd + 1]
            zt = jnp.where(col == 0, b_col, jnp.where(col == 1, i_col, 0.0)).T
            b_row = zt[0:1, :]
            i_row = zt[1:2, :]
            m_prev = m_scr[hd]

            d = jnp.where(causal, b_col - b_row + i_row, NEG)
            inter = b_col + m_prev
            m_t = jnp.maximum(jnp.max(d, axis=1, keepdims=True), inter)
            dmat = jnp.exp(d - m_t)
            q = q_ref[pl.ds(r0, M_L), pl.ds(hd * DKP, DKP)]
            k = k_ref[pl.ds(r0, M_L), pl.ds(hd * DKP, DKP)] * (M_DK ** -0.5)
            vb = v_ref[pl.ds(r0, M_L), pl.ds(hd * DVP, DVP)].astype(BF16)
            qb = q.astype(BF16)
            s = lax.dot_general(qb, k.astype(BF16), (((1,), (1,)), ((), ())),
                                preferred_element_type=F32) * dmat
            w_inter = jnp.exp(inter - m_t)
            c_prev = c_scr[hd]
            n_prev = n_scr[hd]
            num = (jnp.dot(s.astype(BF16), vb, preferred_element_type=F32)
                   + w_inter * jnp.dot(qb, c_prev.astype(BF16), preferred_element_type=F32))
            den = (jnp.sum(s, axis=1, keepdims=True)
                   + w_inter * jnp.sum(q * n_prev, axis=1, keepdims=True))
            h = num / jnp.maximum(jnp.abs(den), jnp.exp(-m_t))
            o_pre = o_ref[pl.ds(r0, M_L), pl.ds(hd * DVP, DVP)]
            hm_ref[pl.ds(r0, M_L), pl.ds(hd * DVP, DVP)] = _head_norm_gate(
                h, o_pre, gain_ref[hd]).astype(BF16)

            b_last = b_row[:, M_L - 1:M_L]
            g_row = b_last - b_row + i_row
            m_new = jnp.maximum(b_last + m_prev, jnp.max(g_row, axis=1, keepdims=True))
            w_k = jnp.exp(b_last - b_col + i_col - m_new)
            decay = jnp.exp(b_last + m_prev - m_new)
            kw = k * w_k
            c_scr[hd] = decay * c_prev + lax.dot_general(
                kw.astype(BF16), vb, (((0,), (0,)), ((), ())), preferred_element_type=F32)
            n_scr[hd] = decay * n_prev + jnp.sum(kw, axis=0, keepdims=True)
            m_scr[hd] = m_new
        return carry

    lax.fori_loop(0, M_TS // M_L, chunk, 0)

    @pl.when(step == pl.num_programs(1) - 1)
    def _():
        for hd in range(M_HEADS):
            c_out_ref[hd] = c_scr[hd, pl.ds(0, M_DK), pl.ds(0, M_DV)]
            n_out_ref[hd] = n_scr[hd, :, pl.ds(0, M_DK)]
            m_out_ref[hd] = m_scr[hd]


def mlstm_prompt(proj, bias, gain):
    ns = SEQ // M_TS
    def cblk(c0, w):
        return pl.BlockSpec((M_TS, w), lambda b, s: (b * ns + s, c0 // w))
    return pl.pallas_call(
        _mlstm_prompt_kernel,
        grid=(BATCH, ns),
        in_specs=[cblk(C_MQ, M_HEADS * DKP), cblk(C_MK, M_HEADS * DKP),
                  cblk(C_MV, M_HEADS * DVP), cblk(C_MO, M_HEADS * DVP),
                  cblk(C_IF, LANES),
                  pl.BlockSpec((1, LANES), lambda b, s: (0, 0)),
                  pl.BlockSpec((M_HEADS, 1, DVP), lambda b, s: (0, 0, 0))],
        out_specs=[pl.BlockSpec((M_TS, M_HEADS * DVP), lambda b, s: (b * ns + s, 0)),
                   pl.BlockSpec((None, M_HEADS, M_DK, M_DV), lambda b, s: (b, 0, 0, 0)),
                   pl.BlockSpec((None, M_HEADS, 1, M_DK), lambda b, s: (b, 0, 0, 0)),
                   pl.BlockSpec((None, M_HEADS, 1, 1), lambda b, s: (b, 0, 0, 0))],
        out_shape=[jax.ShapeDtypeStruct((M_PROMPT, M_HEADS * DVP), BF16),
                   jax.ShapeDtypeStruct((BATCH, M_HEADS, M_DK, M_DV), F32),
                   jax.ShapeDtypeStruct((BATCH, M_HEADS, 1, M_DK), F32),
                   jax.ShapeDtypeStruct((BATCH, M_HEADS, 1, 1), F32)],
        scratch_shapes=[pltpu.VMEM((M_HEADS, DKP, DVP), F32), pltpu.VMEM((M_HEADS, 1, DKP), F32),
                        pltpu.VMEM((M_HEADS, 1, 1), F32)],
        compiler_params=_cparams(("parallel", "arbitrary")),
        name="mlstm_prompt",
    )(proj, proj, proj, proj, proj, bias, gain)


MS_DC = 48


def _pick_row(x8, j):
    rows = lax.broadcasted_iota(jnp.int32, x8.shape, 0)
    return jnp.sum(jnp.where(rows == j, x8, 0.0), axis=0, keepdims=True)


def _mlstm_sample_kernel(q_ref, k_ref, v_ref, o_ref, if_ref, bias_ref, gain_ref,
                         c0_ref, n0_ref, m0_ref,
                         hm_ref, c_out_ref, n_out_ref, m_out_ref,
                         qt_scr, kw_scr, vt_scr, acc_scr, st_scr):
    hd = pl.program_id(0)
    c = pl.program_id(1)

    @pl.when(c == 0)
    def _():
        qt = q_ref[...].T
        kt = (k_ref[...] * (M_DK ** -0.5)).T
        vt_scr[...] = v_ref[...].T
        x_if = if_ref[...].T[0:2 * M_HEADS, :] + bias_ref[...]
        i_pre = _pick_row(x_if, hd)
        log_f = _log_sigmoid(_pick_row(x_if, hd + M_HEADS))
        inter = log_f + m0_ref[...]
        m_new = jnp.maximum(i_pre, inter)
        w_k = jnp.exp(i_pre - m_new)
        decay = jnp.exp(inter - m_new)
        n_prev = n0_ref[...]
        s = jnp.sum(qt * kt, axis=0, keepdims=True) * w_k
        den = s + decay * jnp.sum(qt[:M_DK] * n_prev, axis=0, keepdims=True)
        kw = kt * w_k
        qt_scr[...] = qt
        kw_scr[...] = kw
        n_out_ref[...] = decay * n_prev + kw[:M_DK]
        m_out_ref[...] = m_new
        st_scr[0:1, :] = s
        st_scr[1:2, :] = decay
        st_scr[2:3, :] = den
        st_scr[3:4, :] = m_new
        acc_scr[...] = jnp.zeros_like(acc_scr)

    decay = st_scr[1:2, :]
    vt = vt_scr[pl.ds(0, M_DV), :]

    def tile(t, acc):
        r8 = pl.multiple_of(c * MS_DC + t * 8, 8)
        q8 = qt_scr[pl.ds(r8, 8), :]
        kw8 = kw_scr[pl.ds(r8, 8), :]
        for r in range(8):
            c_row = c0_ref[t * 8 + r]
            c_out_ref[t * 8 + r] = decay * c_row + kw8[r:r + 1, :] * vt
            acc = acc + q8[r:r + 1, :] * c_row
        return acc

    acc = lax.fori_loop(0, MS_DC // 8, tile, acc_scr[...])
    acc_scr[...] = acc

    @pl.when(c == pl.num_programs(1) - 1)
    def _():
        s = st_scr[0:1, :]
        den = st_scr[2:3, :]
        m_t = st_scr[3:4, :]
        h = (s * vt + decay * acc) / jnp.maximum(jnp.abs(den), jnp.exp(-m_t))
        mu = jnp.mean(h, axis=0, keepdims=True)
        hc = h - mu
        var = jnp.mean(hc * hc, axis=0, keepdims=True)
        o_pre = o_ref[...].T[:M_DV, :]
        out = _sigmoid(o_pre) * (hc * lax.rsqrt(var + LN_EPS) * gain_ref[...])
        out = jnp.concatenate([out, jnp.zeros((DVP - M_DV, DEC_BATCH), F32)], axis=0)
        hm_ref[...] = out.T.astype(BF16)


def mlstm_sample(proj, bias_col, gain_col, c0t, n0t, m0t, layer):
    rb = M_PROMPT // DEC_BATCH
    nc = M_DK // MS_DC
    def cblk(c0_, w):
        return pl.BlockSpec((DEC_BATCH, w), lambda h, c: (rb, c0_ // w + h))
    return pl.pallas_call(
        _mlstm_sample_kernel,
        grid=(M_HEADS, nc),
        in_specs=[cblk(C_MQ, DKP), cblk(C_MK, DKP), cblk(C_MV, DVP), cblk(C_MO, DVP),
                  pl.BlockSpec((DEC_BATCH, LANES), lambda h, c: (rb, C_IF // LANES)),
                  pl.BlockSpec((2 * M_HEADS, DEC_BATCH), lambda h, c: (0, 0)),
                  pl.BlockSpec((None, M_DV, 1), lambda h, c: (h, 0, 0)),
                  pl.BlockSpec((None, None, MS_DC, M_DV, DEC_BATCH), lambda h, c: (layer, h, c, 0, 0)),
                  pl.BlockSpec((None, None, M_DK, DEC_BATCH), lambda h, c: (layer, h, 0, 0)),
                  pl.BlockSpec((None, None, 1, DEC_BATCH), lambda h, c: (layer, h, 0, 0))],
        out_specs=[pl.BlockSpec((DEC_BATCH, DVP), lambda h, c: (0, h)),
                   pl.BlockSpec((None, MS_DC, M_DV, DEC_BATCH), lambda h, c: (h, c, 0, 0)),
                   pl.BlockSpec((None, M_DK, DEC_BATCH), lambda h, c: (h, 0, 0)),
                   pl.BlockSpec((None, 1, DEC_BATCH), lambda h, c: (h, 0, 0))],
        out_shape=[jax.ShapeDtypeStruct((DEC_BATCH, M_HEADS * DVP), BF16),
                   jax.ShapeDtypeStruct((M_HEADS, M_DK, M_DV, DEC_BATCH), F32),
                   jax.ShapeDtypeStruct((M_HEADS, M_DK, DEC_BATCH), F32),
                   jax.ShapeDtypeStruct((M_HEADS, 1, DEC_BATCH), F32)],
        scratch_shapes=[pltpu.VMEM((DKP, DEC_BATCH), F32), pltpu.VMEM((DKP, DEC_BATCH), F32),
                        pltpu.VMEM((DVP, DEC_BATCH), F32), pltpu.VMEM((M_DV, DEC_BATCH), F32),
                        pltpu.VMEM((8, DEC_BATCH), F32)],
        compiler_params=_cparams(("arbitrary", "arbitrary")),
        name="mlstm_sample",
    )(proj, proj, proj, proj, proj, bias_col, gain_col, c0t, n0t, m0t)


A_Q = 128
A_LT = A_GW // LANES


def _rope(x, cos, sin_signed):
    lane = lax.broadcasted_iota(jnp.int32, x.shape, 1)
    first_half = (lane % A_HEAD_DIM) < (A_HEAD_DIM // 2)
    partner = jnp.where(first_half, pltpu.roll(x, x.shape[1] - A_HEAD_DIM // 2, 1),
                        pltpu.roll(x, A_HEAD_DIM // 2, 1))
    return x * cos + partner * sin_signed


def _head_masks(shape):
    lane = lax.broadcasted_iota(jnp.int32, shape, 1)
    return [(lane // A_HEAD_DIM) == h for h in range(A_HPG)]


def _attn_group_prompt(dil, gi, qs_scr, ks_scr, vs_scr, o_scr, l_scr):
    length = SEQ // dil
    nb = length // A_Q
    row = lax.broadcasted_iota(jnp.int32, (A_Q, A_Q), 0)
    col = lax.broadcasted_iota(jnp.int32, (A_Q, A_Q), 1)
    cur_ok = col <= row
    prev_ok = col >= row
    masks = _head_masks((A_Q, A_GW))
    nt = (((1,), (1,)), ((), ()))

    def window(start):
        if dil == 1:
            return pl.ds(pl.multiple_of(start, A_Q), A_Q)
        return pl.ds(start, A_Q, stride=dil)

    def rows(scr, start):
        w = window(start)
        return jnp.concatenate([scr[t, w, :] for t in range(A_LT)], axis=1).astype(BF16)

    def block(idx, carry):
        r = idx % dil
        n = idx // dil
        base = r + (dil * A_Q) * n
        qb = rows(qs_scr, base)
        kc = rows(ks_scr, base)
        vc = rows(vs_scr, base)
        if nb > 1:
            pbase = jnp.maximum(base - dil * A_Q, r)
            kp = rows(ks_scr, pbase)
            vp = rows(vs_scr, pbase)
            has_prev = n > 0
        o_acc = jnp.zeros((A_Q, A_GW), F32)
        l_acc = jnp.zeros((A_Q, A_GW), F32)
        for h in range(A_HPG):
            qh = jnp.where(masks[h], qb, jnp.zeros_like(qb))
            s_c = jnp.where(cur_ok, lax.dot_general(qh, kc, nt, preferred_element_type=F32), NEG)
            m = jnp.max(s_c, axis=1, keepdims=True)
            if nb > 1:
                s_p = jnp.where(jnp.logical_and(prev_ok, has_prev),
                                lax.dot_general(qh, kp, nt, preferred_element_type=F32), NEG)
                m = jnp.maximum(m, jnp.max(s_p, axis=1, keepdims=True))
            p_c = jnp.exp(s_c - m)
            l = jnp.sum(p_c, axis=1, keepdims=True)
            o_h = jnp.dot(p_c.astype(BF16), vc, preferred_element_type=F32)
            if nb > 1:
                p_p = jnp.exp(s_p - m)
                l = l + jnp.sum(p_p, axis=1, keepdims=True)
                o_h = o_h + jnp.dot(p_p.astype(BF16), vp, preferred_element_type=F32)
            o_acc = o_acc + jnp.where(masks[h], o_h / l, 0.0)
            l_acc = l_acc + jnp.where(masks[h], m + jnp.log(l), 0.0)
        w = window(base)
        for t in range(A_LT):
            o_scr[gi, t, w, :] = o_acc[:, t * LANES:(t + 1) * LANES]
            l_scr[gi, t, w, :] = l_acc[:, t * LANES:(t + 1) * LANES]
        return carry

    lax.fori_loop(0, dil * nb, block, 0)


def _attn_prompt_kernel(q_ref, k_ref, v_ref, cos_ref, sin_ref, oa_ref, kv0_ref, kv1_ref, kv2_ref,
                        qs_scr, ks_scr, vs_scr, o_scr, l_scr):
    g = pl.program_id(1)
    rc = 256

    def rope_rows(c, carry):
        sl = pl.ds(pl.multiple_of(c * rc, rc), rc)
        cos = cos_ref[sl, :]
        sin = sin_ref[sl, :]
        for t in range(A_LT):
            lanes = pl.ds(t * LANES, LANES)
            qs_scr[t, sl, :] = _rope(q_ref[sl, lanes], cos, sin) * (A_HEAD_DIM ** -0.5)
            ks_scr[t, sl, :] = _rope(k_ref[sl, lanes], cos, sin)
            vs_scr[t, sl, :] = v_ref[sl, lanes]
        return carry

    lax.fori_loop(0, SEQ // rc, rope_rows, 0)

    for gi, ((win, dil), kv_ref) in enumerate(zip(A_GROUPS, (kv0_ref, kv1_ref, kv2_ref))):
        @pl.when(g == gi)
        def _(gi=gi, dil=dil, win=win, kv_ref=kv_ref):
            _attn_group_prompt(dil, gi, qs_scr, ks_scr, vs_scr, o_scr, l_scr)
            keep = min(win, SEQ)
            for c in range(keep // LANES):
                rows = pl.ds(SEQ - keep + c * LANES, LANES)
                for t in range(A_LT):
                    kv_ref[0, pl.ds(t * LANES, LANES), pl.ds(c * LANES, LANES)] = ks_scr[t, rows, :].T
                    kv_ref[1, pl.ds(t * LANES, LANES), pl.ds(c * LANES, LANES)] = vs_scr[t, rows, :].T

    @pl.when(g == len(A_GROUPS) - 1)
    def _():
        def comb(c, carry):
            sl = pl.ds(pl.multiple_of(c * rc, rc), rc)
            for t in range(A_LT):
                l0, l1, l2 = l_scr[0, t, sl, :], l_scr[1, t, sl, :], l_scr[2, t, sl, :]
                mx = jnp.maximum(jnp.maximum(l0, l1), l2)
                e0, e1, e2 = jnp.exp(l0 - mx), jnp.exp(l1 - mx), jnp.exp(l2 - mx)
                tot = e0 * o_scr[0, t, sl, :] + e1 * o_scr[1, t, sl, :] + e2 * o_scr[2, t, sl, :]
                oa_ref[sl, pl.ds(t * LANES, LANES)] = (tot / (e0 + e1 + e2)).astype(BF16)
            return carry
        lax.fori_loop(0, SEQ // rc, comb, 0)


def attn_prompt(proj, cos, sin):
    def gblk(c0):
        return pl.BlockSpec((SEQ, A_GW), lambda b, g: (b, c0 // A_GW + g))
    tab = pl.BlockSpec((SEQ, LANES), lambda b, g: (0, 0))
    keeps = [min(win, SEQ) for win, _ in A_GROUPS]
    return pl.pallas_call(
        _attn_prompt_kernel,
        grid=(BATCH, len(A_GROUPS)),
        in_specs=[gblk(C_AQ), gblk(C_AK), gblk(C_AV), tab, tab],
        out_specs=[pl.BlockSpec((SEQ, A_GW), lambda b, g: (b, 0))]
                  + [pl.BlockSpec((None, 2, A_GW, kp), lambda b, g: (b, 0, 0, 0)) for kp in keeps],
        out_shape=[jax.ShapeDtypeStruct((M_PROMPT, A_GW), BF16)]
                  + [jax.ShapeDtypeStruct((BATCH, 2, A_GW, kp), F32) for kp in keeps],
        scratch_shapes=[pltpu.VMEM((A_LT, SEQ, LANES), F32),
                        pltpu.VMEM((A_LT, SEQ, LANES), F32),
                        pltpu.VMEM((A_LT, SEQ, LANES), F32),
                        pltpu.VMEM((len(A_GROUPS), A_LT, SEQ, LANES), F32),
                        pltpu.VMEM((len(A_GROUPS), A_LT, SEQ, LANES), F32)],
        compiler_params=_cparams(("parallel", "arbitrary")),
        name="attn_prompt",
    )(proj, proj, proj, cos, sin)


AS_BB = 2


def _attn_sample_kernel(q_ref, k_ref, v_ref, cos_ref, sin_ref, c0_ref, c1_ref, c2_ref,
                        oa_ref, kt_ref, vt_ref, qt_scr, s0_scr, ot_scr, lt_scr):
    i = pl.program_id(0)
    lane_b = lax.broadcasted_iota(jnp.int32, (1, DEC_BATCH), 1)
    sub8 = lax.broadcasted_iota(jnp.int32, (8, DEC_BATCH), 0)

    @pl.when(i == 0)
    def _():
        cos = cos_ref[...]
        sin = sin_ref[...]
        for gi in range(len(A_GROUPS)):
            gs = pl.ds(gi * A_GW, A_GW)
            qt = (_rope(q_ref[:, gs], cos, sin) * (A_HEAD_DIM ** -0.5)).T
            kt = _rope(k_ref[:, gs], cos, sin).T
            qt_scr[gi] = qt
            kt_ref[gi] = kt
            vt_ref[gi] = v_ref[:, gs].T
            prod = qt * kt
            s0 = jnp.zeros((8, DEC_BATCH), F32)
            for h in range(A_HPG):
                part = jnp.sum(prod[h * A_HEAD_DIM:(h + 1) * A_HEAD_DIM], axis=0, keepdims=True)
                s0 = jnp.where(sub8 == h, part, s0)
            s0_scr[gi] = s0
        ot_scr[...] = jnp.zeros_like(ot_scr)
        lt_scr[...] = jnp.zeros_like(lt_scr)

    for bl in range(AS_BB):
        pick = lane_b == i * AS_BB + bl
        for gi, (cache_ref, (_, dil)) in enumerate(zip((c0_ref, c1_ref, c2_ref), A_GROUPS)):
            wb = cache_ref.shape[-1]
            if dil > 1:
                keep = (lax.broadcasted_iota(jnp.int32, (1, wb), 1) & (dil - 1)) == 0
            s0_all = s0_scr[gi]
            for h in range(A_HPG):
                hs = pl.ds(h * A_HEAD_DIM, A_HEAD_DIM)
                q_col = jnp.sum(jnp.where(pick, qt_scr[gi, hs, :], 0.0), axis=1, keepdims=True)
                v_col = jnp.sum(jnp.where(pick, vt_ref[gi, hs, :], 0.0), axis=1, keepdims=True)
                s0 = jnp.sum(jnp.where(pick, s0_all[h:h + 1, :], 0.0), axis=1, keepdims=True)
                s = jnp.sum(q_col * cache_ref[bl, 0, h], axis=0, keepdims=True)
                if dil > 1:
                    s = jnp.where(keep, s, NEG)
                m = jnp.maximum(jnp.max(s, axis=1, keepdims=True), s0)
                p = jnp.exp(s - m)
                p0 = jnp.exp(s0 - m)
                l = jnp.sum(p, axis=1, keepdims=True) + p0
                o = (jnp.sum(p * cache_ref[bl, 1, h], axis=1, keepdims=True) + p0 * v_col) / l
                ot_scr[gi, hs, :] = jnp.where(pick, o, ot_scr[gi, hs, :])
                lt_scr[gi, pl.ds(h, 1), :] = jnp.where(pick, m + jnp.log(l), lt_scr[gi, pl.ds(h, 1), :])

    @pl.when(i == pl.num_programs(0) - 1)
    def _():
        for h in range(A_HPG):
            hs = pl.ds(h * A_HEAD_DIM, A_HEAD_DIM)
            l0, l1, l2 = (lt_scr[gi, pl.ds(h, 1), :] for gi in range(3))
            mx = jnp.maximum(jnp.maximum(l0, l1), l2)
            e0, e1, e2 = jnp.exp(l0 - mx), jnp.exp(l1 - mx), jnp.exp(l2 - mx)
            tot = e0 * ot_scr[0, hs, :] + e1 * ot_scr[1, hs, :] + e2 * ot_scr[2, hs, :]
            ot_scr[0, hs, :] = tot / (e0 + e1 + e2)
        oa_ref[...] = ot_scr[0].T.astype(BF16)


def attn_sample(proj, cos, sin, caches_t, layer):
    rb = M_PROMPT // DEC_BATCH
    def pblk(c0):
        return pl.BlockSpec((DEC_BATCH, A_WIDTH), lambda i: (rb, c0 // A_WIDTH))
    tab = pl.BlockSpec((1, A_GW), lambda i: (0, 0))
    cache_specs = [pl.BlockSpec((None, AS_BB, 2, A_HPG, A_HEAD_DIM, ct.shape[-1]),
                                lambda i: (layer, i, 0, 0, 0, 0)) for ct in caches_t]
    ng = len(A_GROUPS)
    full3 = pl.BlockSpec((ng, A_GW, DEC_BATCH), lambda i: (0, 0, 0))
    return pl.pallas_call(
        _attn_sample_kernel,
        grid=(DEC_BATCH // AS_BB,),
        in_specs=[pblk(C_AQ), pblk(C_AK), pblk(C_AV), tab, tab] + cache_specs,
        out_specs=[pl.BlockSpec((DEC_BATCH, A_GW), lambda i: (0, 0)), full3, full3],
        out_shape=[jax.ShapeDtypeStruct((DEC_BATCH, A_GW), BF16),
                   jax.ShapeDtypeStruct((ng, A_GW, DEC_BATCH), F32),
                   jax.ShapeDtypeStruct((ng, A_GW, DEC_BATCH), F32)],
        scratch_shapes=[pltpu.VMEM((ng, A_GW, DEC_BATCH), F32), pltpu.VMEM((ng, 8, DEC_BATCH), F32),
                        pltpu.VMEM((ng, A_GW, DEC_BATCH), F32), pltpu.VMEM((ng, 8, DEC_BATCH), F32)],
        compiler_params=_cparams(("arbitrary",)),
        name="attn_sample",
    )(proj, proj, proj, cos, sin, *caches_t)


def _pad_heads(wt, d, dp):
    c = wt.shape[1]
    wt = wt.reshape(M_HEADS, d, c)
    return jnp.pad(wt, ((0, 0), (0, dp - d), (0, 0))).reshape(M_HEADS * dp, c)


def _layer_weights(w_in_l, w_up_m, w_up_c, w_up_a, w_o_l):
    w_in_t = jnp.transpose(w_in_l)
    o = IN_OFFSETS
    piece = lambda i: w_in_t[o[i]:o[i + 1]]
    mq, mk, mv, mi, mf, mo, cb, cc, ch, aq, ak, av, gt = [piece(i) for i in range(13)]
    w_br = jnp.concatenate(
        [_pad_heads(mv, M_DV, DVP), _pad_heads(mo, M_DV, DVP),
         _pad_heads(mq, M_DK, DKP), _pad_heads(mk, M_DK, DKP),
         cb, cc, ch, aq, ak, av,
         mi, mf, jnp.zeros((PROJ_W - C_IF - 2 * M_HEADS, D_MODEL), F32)], axis=0).astype(BF16)
    w_um = jnp.pad(w_up_m.reshape(M_HEADS, M_DV, D_MODEL),
                   ((0, 0), (0, DVP - M_DV), (0, 0))).reshape(M_HEADS * DVP, D_MODEL)
    return (w_br, gt.astype(BF16), w_um.astype(BF16), w_up_c.astype(BF16),
            w_up_a.astype(BF16), w_o_l.astype(BF16))


def _rope_tables(pos):
    half = A_HEAD_DIM // 2
    inv = ROPE_THETA ** (-(2.0 * jnp.arange(half, dtype=F32)) / A_HEAD_DIM)
    ang = pos.astype(F32)[:, None] * inv[None, :]
    cos = jnp.cos(ang)
    sin = jnp.sin(ang)
    cos = jnp.tile(jnp.concatenate([cos, cos], axis=-1), (1, A_HPG))
    sin = jnp.tile(jnp.concatenate([-sin, sin], axis=-1), (1, A_HPG))
    return cos, sin


def kernel(x_prompt, x_sample, state_mlstm_C, state_mlstm_n, state_mlstm_m, state_conv,
           cache_attn_kv_w128, cache_attn_kv_w512, cache_attn_kv_w2048,
           w_in, b_gate_if, mlstm_norm_g, conv_w, w_up_mlstm, w_up_conv, w_up_attn, w_o,
           w_ffn_in, w_ffn_out, ln_g, ln_b):
    x = jnp.concatenate([x_prompt.reshape(M_PROMPT, D_MODEL),
                         x_sample.reshape(DEC_BATCH, D_MODEL)], axis=0)
    cos_p, sin_p = _rope_tables(jnp.arange(SEQ))
    cos_s, sin_s = _rope_tables(PAST_LEN + jnp.arange(1))
    w_ffn_in_b = w_ffn_in.astype(BF16)
    w_ffn_out_b = w_ffn_out.astype(BF16)

    c0t = jnp.transpose(state_mlstm_C, (0, 2, 3, 4, 1))
    n0t = jnp.transpose(state_mlstm_n, (0, 2, 3, 1))
    m0t = jnp.transpose(state_mlstm_m, (0, 2, 1)).reshape(DEPTH, M_HEADS, 1, DEC_BATCH)
    caches_t = [jnp.transpose(c, (0, 1, 3, 4, 5, 2))
                for c in (cache_attn_kv_w128, cache_attn_kv_w512, cache_attn_kv_w2048)]

    p_states, s_states = [], []
    for l in range(DEPTH):
        w_br, w_g, w_um, w_uc, w_ua, w_ol = _layer_weights(
            w_in[l], w_up_mlstm[l], w_up_conv[l], w_up_attn[l], w_o[l])
        bias = jnp.pad(b_gate_if[l], (0, LANES - 2 * M_HEADS)).reshape(1, LANES)
        bias_col = jnp.broadcast_to(b_gate_if[l][:, None], (2 * M_HEADS, DEC_BATCH))
        gain = jnp.pad(mlstm_norm_g[l].reshape(M_HEADS, M_DV), ((0, 0), (0, DVP - M_DV)))
        gain_col = mlstm_norm_g[l].reshape(M_HEADS, M_DV, 1)

        x, xb = ffn_ln(x, w_ffn_in_b, w_ffn_out_b, ln_g[l, 0], ln_b[l, 0], l, 0)
        proj = branch_proj(xb, w_br)

        hm_p, pc, pn, pm = mlstm_prompt(proj, bias, gain.reshape(M_HEADS, 1, DVP))
        hm_s, sct, snt, smt = mlstm_sample(proj, bias_col, gain_col, c0t, n0t, m0t, l)
        yc_p, pconv = conv_prompt(proj, conv_w[l])
        yc_s, sconv = conv_sample(proj, state_conv[l], conv_w[l])
        oa_p, *kv_p = attn_prompt(proj, cos_p[:, :LANES], sin_p[:, :LANES])
        oa_s, kt_s, vt_s = attn_sample(proj, cos_s, sin_s, caches_t, l)

        hm = jnp.concatenate([hm_p, hm_s], axis=0)
        yc = jnp.concatenate([yc_p, yc_s], axis=0)
        oa = jnp.concatenate([oa_p, oa_s], axis=0)
        x = merge_ln(x, xb, hm, yc, oa, w_g, w_um, w_uc, w_ua, w_ol, ln_g[l, 1], ln_b[l, 1])
        x, _ = ffn_ln(x, w_ffn_in_b, w_ffn_out_b, ln_g[l, 2], ln_b[l, 2], l, 1)

        kt_s = kt_s.reshape(3, A_HPG, A_HEAD_DIM, DEC_BATCH)
        vt_s = vt_s.reshape(3, A_HPG, A_HEAD_DIM, DEC_BATCH)
        kv_s = [jnp.stack([kt_s[gi], vt_s[gi]], axis=0) for gi in range(3)]
        p_states.append((pc, pn.reshape(BATCH, M_HEADS, M_DK), pm.reshape(BATCH, M_HEADS), pconv,
                         kv_p[0], kv_p[1], kv_p[2]))
        s_states.append((sct, snt, smt, sconv, kv_s[0], kv_s[1], kv_s[2]))

    y_prompt = x[:M_PROMPT].reshape(BATCH, SEQ, D_MODEL)
    y_sample = x[M_PROMPT:].reshape(DEC_BATCH, 1, D_MODEL)
    p_out = [jnp.stack(z) for z in zip(*p_states)]
    for j in range(4, 7):
        kvt = p_out[j]
        p_out[j] = jnp.transpose(kvt.reshape(DEPTH, BATCH, 2, A_HPG, A_HEAD_DIM, kvt.shape[-1]),
                                 (0, 1, 5, 2, 3, 4))
    sct, snt, smt, sconv, kv0, kv1, kv2 = [jnp.stack(z) for z in zip(*s_states)]
    s_out = [jnp.transpose(sct, (0, 4, 1, 2, 3)),
             jnp.transpose(snt, (0, 3, 1, 2)),
             jnp.transpose(smt.reshape(DEPTH, M_HEADS, DEC_BATCH), (0, 2, 1)),
             sconv]
    s_out += [jnp.transpose(kv, (0, 4, 1, 2, 3)).reshape(DEPTH, DEC_BATCH, 1, 2, A_HPG, A_HEAD_DIM)
              for kv in (kv0, kv1, kv2)]
    return (y_prompt, y_sample, *p_out, *s_out)
```

```python
import functools
import math

import jax
import jax.numpy as jnp
import numpy as np
from jax import lax
from jax.experimental import pallas as pl
from jax.experimental.pallas import tpu as pltpu

F32 = jnp.float32
BF16 = jnp.bfloat16

D_MODEL = 2048
BATCH = 4
SEQ = 2048
DEPTH = 2
DEC_BATCH = 128
PAST_LEN = 2048
M_HEADS = 4
M_DV = 192
M_DK = 96
M_QK = M_HEADS * M_DK
M_WIDTH = M_HEADS * M_DV
CONV_WIDTH = 512
CONV_K = 3
A_GROUPS = ((128, 1), (512, 4), (2048, 16))
A_HPG = 4
A_HEAD_DIM = 64
A_GW = A_HPG * A_HEAD_DIM
A_WIDTH = 3 * A_GW
ROPE_THETA = 10000.0
N_BRANCH = 3
D_FF = 5632
LN_EPS = 1e-5
ALPHA = (2 * DEPTH) ** 0.25
IN_SIZES = (M_QK, M_QK, M_WIDTH, M_HEADS, M_HEADS, M_WIDTH,
            CONV_WIDTH, CONV_WIDTH, CONV_WIDTH,
            A_WIDTH, A_WIDTH, A_WIDTH, N_BRANCH * D_MODEL)
IN_OFFSETS = tuple(int(o) for o in np.cumsum((0,) + IN_SIZES))

M_PROMPT = BATCH * SEQ
M_ROWS = M_PROMPT + DEC_BATCH

LANES = 128
DKP = 128
DVP = 256
VMEM_LIMIT = 52 * 1024 * 1024

C_MV, C_MO = 0, 1024
C_MQ, C_MK = 2048, 2560
C_CB, C_CC, C_CH = 3072, 3584, 4096
C_AQ, C_AK, C_AV = 4608, 5376, 6144
C_IF = 6912
PROJ_W = 7168

NEG = -1e30


def _sigmoid(x):
    return 1.0 / (1.0 + jnp.exp(-x))


def _layer_norm(z, g, b):
    mu = jnp.mean(z, axis=-1, keepdims=True)
    zc = z - mu
    var = jnp.mean(zc * zc, axis=-1, keepdims=True)
    return zc * lax.rsqrt(var + LN_EPS) * g + b


def _cparams(sem):
    return pltpu.CompilerParams(dimension_semantics=sem, vmem_limit_bytes=VMEM_LIMIT)


FFN_TM = 640
FFN_TF = 512


def _ffn_kernel(x_ref, wa_ref, wb_ref, wo_ref, g_ref, b_ref, y_ref, yb_ref, xb_scr):
    f = pl.program_id(1)

    @pl.when(f == 0)
    def _():
        xb_scr[...] = x_ref[...].astype(BF16)
        y_ref[...] = jnp.zeros_like(y_ref)

    xb = xb_scr[...]
    a = jnp.dot(xb, wa_ref[...], preferred_element_type=F32)
    b = jnp.dot(xb, wb_ref[...], preferred_element_type=F32)
    h = (a * _sigmoid(a)) * b
    y_ref[...] += jnp.dot(h.astype(BF16), wo_ref[...], preferred_element_type=F32)

    @pl.when(f == pl.num_programs(1) - 1)
    def _():
        z = ALPHA * x_ref[...] + 0.5 * y_ref[...]
        out = _layer_norm(z, g_ref[...], b_ref[...])
        y_ref[...] = out
        yb_ref[...] = out.astype(BF16)


def ffn_ln(x, w_in, w_out, g, b, layer, which):
    m = x.shape[0]
    nf = D_FF // FFN_TF
    return pl.pallas_call(
        _ffn_kernel,
        grid=(m // FFN_TM, nf),
        in_specs=[
            pl.BlockSpec((FFN_TM, D_MODEL), lambda i, f: (i, 0)),
            pl.BlockSpec((None, None, D_MODEL, FFN_TF), lambda i, f: (layer, which, 0, f)),
            pl.BlockSpec((None, None, D_MODEL, FFN_TF), lambda i, f: (layer, which, 0, f + nf)),
            pl.BlockSpec((None, None, FFN_TF, D_MODEL), lambda i, f: (layer, which, f, 0)),
            pl.BlockSpec((1, D_MODEL), lambda i, f: (0, 0)),
            pl.BlockSpec((1, D_MODEL), lambda i, f: (0, 0)),
        ],
        out_specs=[
            pl.BlockSpec((FFN_TM, D_MODEL), lambda i, f: (i, 0)),
            pl.BlockSpec((FFN_TM, D_MODEL), lambda i, f: (i, 0)),
        ],
        out_shape=[jax.ShapeDtypeStruct((m, D_MODEL), F32),
                   jax.ShapeDtypeStruct((m, D_MODEL), BF16)],
        scratch_shapes=[pltpu.VMEM((FFN_TM, D_MODEL), BF16)],
        compiler_params=_cparams(("parallel", "arbitrary")),
        name="ffn_ln",
    )(x, w_in, w_in, w_out, g.reshape(1, D_MODEL), b.reshape(1, D_MODEL))


PROJ_TM = 1040
PROJ_TN = 1024


NT_DIMS = (((1,), (1,)), ((), ()))


def _proj_kernel(xb_ref, wt_ref, o_ref):
    o_ref[...] = lax.dot_general(xb_ref[...], wt_ref[...], NT_DIMS, preferred_element_type=F32)


def branch_proj(xb, wt):
    m = xb.shape[0]
    return pl.pallas_call(
        _proj_kernel,
        grid=(m // PROJ_TM, PROJ_W // PROJ_TN),
        in_specs=[pl.BlockSpec((PROJ_TM, D_MODEL), lambda i, j: (i, 0)),
                  pl.BlockSpec((PROJ_TN, D_MODEL), lambda i, j: (j, 0))],
        out_specs=pl.BlockSpec((PROJ_TM, PROJ_TN), lambda i, j: (i, j)),
        out_shape=jax.ShapeDtypeStruct((m, PROJ_W), F32),
        compiler_params=_cparams(("parallel", "parallel")),
        name="branch_proj",
    )(xb, wt)


MRG_TM = 640
MRG_TN = 512


def _gate_up_kernel(xb_ref, hm_ref, yc_ref, oa_ref, wg0_ref, wg1_ref, wg2_ref,
                    wum_ref, wuc_ref, wua_ref, o_ref):
    xb = xb_ref[...]

    def gated(wg_ref, br_ref, wu_ref):
        gate = _sigmoid(lax.dot_general(xb, wg_ref[...], NT_DIMS, preferred_element_type=F32))
        return gate * jnp.dot(br_ref[...], wu_ref[...], preferred_element_type=F32)

    merged = (gated(wg0_ref, hm_ref, wum_ref) + gated(wg1_ref, yc_ref, wuc_ref)
              + gated(wg2_ref, oa_ref, wua_ref))
    o_ref[...] = merged.astype(BF16)


def _out_ln_kernel(x_ref, mg_ref, wo_ref, g_ref, b_ref, y_ref):
    z = ALPHA * x_ref[...] + jnp.dot(mg_ref[...], wo_ref[...], preferred_element_type=F32)
    y_ref[...] = _layer_norm(z, g_ref[...], b_ref[...])


def merge_ln(x, xb, hm, yc, oa, wg, wum, wuc, wua, wo, g, b):
    m = x.shape[0]
    nn = D_MODEL // MRG_TN
    row = lambda w: pl.BlockSpec((MRG_TM, w), lambda i, n: (i, 0))
    merged = pl.pallas_call(
        _gate_up_kernel,
        grid=(m // MRG_TM, nn),
        in_specs=[
            row(D_MODEL), row(M_HEADS * DVP), row(CONV_WIDTH), row(A_GW),
            pl.BlockSpec((MRG_TN, D_MODEL), lambda i, n: (n, 0)),
            pl.BlockSpec((MRG_TN, D_MODEL), lambda i, n: (n + nn, 0)),
            pl.BlockSpec((MRG_TN, D_MODEL), lambda i, n: (n + 2 * nn, 0)),
            pl.BlockSpec((M_HEADS * DVP, MRG_TN), lambda i, n: (0, n)),
            pl.BlockSpec((CONV_WIDTH, MRG_TN), lambda i, n: (0, n)),
            pl.BlockSpec((A_GW, MRG_TN), lambda i, n: (0, n)),
        ],
        out_specs=pl.BlockSpec((MRG_TM, MRG_TN), lambda i, n: (i, n)),
        out_shape=jax.ShapeDtypeStruct((m, D_MODEL), BF16),
        compiler_params=_cparams(("parallel", "parallel")),
        name="gate_up",
    )(xb, hm, yc, oa, wg, wg, wg, wum, wuc, wua)
    rows = pl.BlockSpec((MRG_TM, D_MODEL), lambda i: (i, 0))
    vec = pl.BlockSpec((1, D_MODEL), lambda i: (0, 0))
    return pl.pallas_call(
        _out_ln_kernel,
        grid=(m // MRG_TM,),
        in_specs=[rows, rows, pl.BlockSpec((D_MODEL, D_MODEL), lambda i: (0, 0)), vec, vec],
        out_specs=rows,
        out_shape=jax.ShapeDtypeStruct((m, D_MODEL), F32),
        compiler_params=_cparams(("parallel",)),
        name="out_ln",
    )(x, merged, wo, g.reshape(1, D_MODEL), b.reshape(1, D_MODEL))


def _conv_prompt_kernel(cb_ref, cc_ref, ch_ref, w_ref, y_ref, st_ref, u_scr):
    u = cc_ref[...] * ch_ref[...]
    u_scr[pl.ds(0, 8), :] = jnp.zeros((8, CONV_WIDTH), F32)
    u_scr[pl.ds(8, SEQ), :] = u
    w = w_ref[...]
    acc = (w[0:1, :] * u_scr[pl.ds(6, SEQ), :] + w[1:2, :] * u_scr[pl.ds(7, SEQ), :]
           + w[2:3, :] * u)
    y_ref[...] = (cb_ref[...] * acc).astype(BF16)
    st_ref[...] = u_scr[pl.ds(8 + SEQ - (CONV_K - 1), CONV_K - 1), :]


def conv_prompt(proj, conv_w):
    blk = lambda c: pl.BlockSpec((SEQ, CONV_WIDTH), lambda b, c=c: (b, c // CONV_WIDTH))
    return pl.pallas_call(
        _conv_prompt_kernel,
        grid=(BATCH,),
        in_specs=[blk(C_CB), blk(C_CC), blk(C_CH),
                  pl.BlockSpec((CONV_K, CONV_WIDTH), lambda b: (0, 0))],
        out_specs=[pl.BlockSpec((SEQ, CONV_WIDTH), lambda b: (b, 0)),
                   pl.BlockSpec((None, CONV_K - 1, CONV_WIDTH), lambda b: (b, 0, 0))],
        out_shape=[jax.ShapeDtypeStruct((M_PROMPT, CONV_WIDTH), BF16),
                   jax.ShapeDtypeStruct((BATCH, CONV_K - 1, CONV_WIDTH), F32)],
        scratch_shapes=[pltpu.VMEM((SEQ + 8, CONV_WIDTH), F32)],
        compiler_params=_cparams(("parallel",)),
        name="conv_prompt",
    )(proj, proj, proj, conv_w)


def _conv_sample_kernel(cb_ref, cc_ref, ch_ref, prev_ref, w_ref, y_ref, st_ref):
    u = cc_ref[...] * ch_ref[...]
    w = w_ref[...]
    p0 = prev_ref[:, 0, :]
    p1 = prev_ref[:, 1, :]
    acc = w[0:1, :] * p0 + w[1:2, :] * p1 + w[2:3, :] * u
    y_ref[...] = (cb_ref[...] * acc).astype(BF16)
    st_ref[:, 0, :] = p1
    st_ref[:, 1, :] = u


def conv_sample(proj, prev, conv_w):
    rb = M_PROMPT // DEC_BATCH
    blk = lambda c: pl.BlockSpec((DEC_BATCH, CONV_WIDTH), lambda i, c=c: (rb, c // CONV_WIDTH))
    full3 = pl.BlockSpec((DEC_BATCH, CONV_K - 1, CONV_WIDTH), lambda i: (0, 0, 0))
    return pl.pallas_call(
        _conv_sample_kernel,
        grid=(1,),
        in_specs=[blk(C_CB), blk(C_CC), blk(C_CH), full3,
                  pl.BlockSpec((CONV_K, CONV_WIDTH), lambda i: (0, 0))],
        out_specs=[pl.BlockSpec((DEC_BATCH, CONV_WIDTH), lambda i: (0, 0)), full3],
        out_shape=[jax.ShapeDtypeStruct((DEC_BATCH, CONV_WIDTH), BF16),
                   jax.ShapeDtypeStruct((DEC_BATCH, CONV_K - 1, CONV_WIDTH), F32)],
        compiler_params=_cparams(("arbitrary",)),
        name="conv_sample",
    )(proj, proj, proj, prev, conv_w)


M_L = 128


def _log_sigmoid(x):
    return jnp.minimum(x, 0.0) - jnp.log1p(jnp.exp(-jnp.abs(x)))


def _head_norm_gate(h, o_pre, gain):
    lane = lax.broadcasted_iota(jnp.int32, h.shape, 1)
    real = lane < M_DV
    mu = jnp.sum(h, axis=-1, keepdims=True) * (1.0 / M_DV)
    hc = jnp.where(real, h - mu, 0.0)
    var = jnp.sum(hc * hc, axis=-1, keepdims=True) * (1.0 / M_DV)
    return _sigmoid(o_pre) * (hc * lax.rsqrt(var + LN_EPS) * gain)


M_TS = 512


M_NROW = M_DV


def _mlstm_prompt_kernel(q_ref, k_ref, v_ref, o_ref, if_ref, bias_ref, gain_ref,
                         hm_ref, c_out_ref, n_out_ref, m_out_ref, ct_scr, m_scr):
    step = pl.program_id(1)

    @pl.when(step == 0)
    def _():
        ct_scr[...] = jnp.zeros_like(ct_scr)
        m_scr[...] = jnp.zeros_like(m_scr)

    row = lax.broadcasted_iota(jnp.int32, (M_L, M_L), 0)
    col = lax.broadcasted_iota(jnp.int32, (M_L, M_L), 1)
    causal_t = row <= col
    tri = (col <= row).astype(F32)
    bias = bias_ref[...]
    ones_lane = lax.broadcasted_iota(jnp.int32, (M_L, DVP), 1) == M_NROW
    real_rows = lax.broadcasted_iota(jnp.int32, (DVP, M_L), 0) < M_DV
    tn = (((0,), (0,)), ((), ()))

    def chunk(c, carry):
        r0 = pl.multiple_of(c * M_L, M_L)
        x_if = if_ref[pl.ds(r0, M_L), :] + bias
        log_f = _log_sigmoid(x_if)
        cs = jnp.dot(tri, log_f, preferred_element_type=F32, precision=lax.Precision.HIGHEST)
        zt = jnp.where(col < M_HEADS, x_if, cs).T
        for hd in range(M_HEADS):
            b_row = zt[M_HEADS + hd:M_HEADS + hd + 1, :]
            c_col = x_if[:, hd:hd + 1] - cs[:, M_HEADS + hd:M_HEADS + hd + 1]
            c_rep = jnp.broadcast_to(c_col, (M_L, M_L))
            m_prev = m_scr[hd]
            b_last = b_row[:, M_L - 1:M_L]

            d_t = jnp.where(causal_t, b_row + c_rep, NEG)
            inter = b_row + m_prev
            m_t = jnp.maximum(jnp.max(d_t, axis=0, keepdims=True), inter)
            q = q_ref[pl.ds(r0, M_L), pl.ds(hd * DKP, DKP)].astype(BF16)
            k = k_ref[pl.ds(r0, M_L), pl.ds(hd * DKP, DKP)] * (M_DK ** -0.5)
            v1 = jnp.where(ones_lane, 1.0, v_ref[pl.ds(r0, M_L), pl.ds(hd * DVP, DVP)]).astype(BF16)
            s_t = lax.dot_general(k.astype(BF16), q, NT_DIMS, preferred_element_type=F32) * jnp.exp(d_t - m_t)
            w_inter = jnp.exp(inter - m_t)
            ct_prev = ct_scr[hd]
            num_t = (lax.dot_general(v1, s_t.astype(BF16), tn, preferred_element_type=F32)
                     + w_inter * lax.dot_general(ct_prev.astype(BF16), q, NT_DIMS,
                                                 preferred_element_type=F32))
            den = num_t[M_NROW:M_NROW + 1, :]
            h_t = jnp.where(real_rows, num_t / jnp.maximum(jnp.abs(den), jnp.exp(-m_t)), 0.0)
            mu = jnp.sum(h_t, axis=0, keepdims=True) * (1.0 / M_DV)
            hc = jnp.where(real_rows, h_t - mu, 0.0)
            var = jnp.sum(hc * hc, axis=0, keepdims=True) * (1.0 / M_DV)
            hn = (hc * lax.rsqrt(var + LN_EPS)).T * gain_ref[hd]
            o_pre = o_ref[pl.ds(r0, M_L), pl.ds(hd * DVP, DVP)]
            hm_ref[pl.ds(r0, M_L), pl.ds(hd * DVP, DVP)] = (_sigmoid(o_pre) * hn).astype(BF16)

            m_new = jnp.maximum(b_last + m_prev, b_last + jnp.max(c_rep, axis=0, keepdims=True)[:, 0:1])
            kw = k * jnp.exp(c_rep + (b_last - m_new))
            ct_scr[hd] = (jnp.exp(b_last + m_prev - m_new) * ct_prev
                          + lax.dot_general(v1, kw.astype(BF16), tn, preferred_element_type=F32))
            m_scr[hd] = m_new
        return carry

    lax.fori_loop(0, M_TS // M_L, chunk, 0)

    @pl.when(step == pl.num_programs(1) - 1)
    def _():
        for hd in range(M_HEADS):
            ct = ct_scr[hd]
            c_out_ref[hd] = ct.T[0:M_DK, 0:M_DV]
            n_out_ref[hd] = ct[M_NROW:M_NROW + 1, 0:M_DK]
            m_out_ref[hd] = m_scr[hd]


def mlstm_prompt(proj, bias, gain):
    ns = SEQ // M_TS
    def cblk(c0, w):
        return pl.BlockSpec((M_TS, w), lambda b, s: (b * ns + s, c0 // w))
    return pl.pallas_call(
        _mlstm_prompt_kernel,
        grid=(BATCH, ns),
        in_specs=[cblk(C_MQ, M_HEADS * DKP), cblk(C_MK, M_HEADS * DKP),
                  cblk(C_MV, M_HEADS * DVP), cblk(C_MO, M_HEADS * DVP),
                  cblk(C_IF, LANES),
                  pl.BlockSpec((1, LANES), lambda b, s: (0, 0)),
                  pl.BlockSpec((M_HEADS, 1, DVP), lambda b, s: (0, 0, 0))],
        out_specs=[pl.BlockSpec((M_TS, M_HEADS * DVP), lambda b, s: (b * ns + s, 0)),
                   pl.BlockSpec((None, M_HEADS, M_DK, M_DV), lambda b, s: (b, 0, 0, 0)),
                   pl.BlockSpec((None, M_HEADS, 1, M_DK), lambda b, s: (b, 0, 0, 0)),
                   pl.BlockSpec((None, M_HEADS, 1, 1), lambda b, s: (b, 0, 0, 0))],
        out_shape=[jax.ShapeDtypeStruct((M_PROMPT, M_HEADS * DVP), BF16),
                   jax.ShapeDtypeStruct((BATCH, M_HEADS, M_DK, M_DV), F32),
                   jax.ShapeDtypeStruct((BATCH, M_HEADS, 1, M_DK), F32),
                   jax.ShapeDtypeStruct((BATCH, M_HEADS, 1, 1), F32)],
        scratch_shapes=[pltpu.VMEM((M_HEADS, DVP, DKP), F32), pltpu.VMEM((M_HEADS, 1, 1), F32)],
        compiler_params=_cparams(("parallel", "arbitrary")),
        name="mlstm_prompt",
    )(proj, proj, proj, proj, proj, bias, gain)


MS_DC = 48


def _pick_row(x8, j):
    rows = lax.broadcasted_iota(jnp.int32, x8.shape, 0)
    return jnp.sum(jnp.where(rows == j, x8, 0.0), axis=0, keepdims=True)


def _mlstm_sample_kernel(q_ref, k_ref, v_ref, o_ref, if_ref, bias_ref, gain_ref,
                         c0_ref, n0_ref, m0_ref,
                         hm_ref, c_out_ref, n_out_ref, m_out_ref,
                         qt_scr, kw_scr, vt_scr, acc_scr, st_scr):
    hd = pl.program_id(0)
    c = pl.program_id(1)

    @pl.when(c == 0)
    def _():
        qt = q_ref[...].T
        kt = (k_ref[...] * (M_DK ** -0.5)).T
        vt_scr[...] = v_ref[...].T
        x_if = if_ref[...].T[0:2 * M_HEADS, :] + bias_ref[...]
        i_pre = _pick_row(x_if, hd)
        log_f = _log_sigmoid(_pick_row(x_if, hd + M_HEADS))
        inter = log_f + m0_ref[...]
        m_new = jnp.maximum(i_pre, inter)
        w_k = jnp.exp(i_pre - m_new)
        decay = jnp.exp(inter - m_new)
        n_prev = n0_ref[...]
        s = jnp.sum(qt * kt, axis=0, keepdims=True) * w_k
        den = s + decay * jnp.sum(qt[:M_DK] * n_prev, axis=0, keepdims=True)
        kw = kt * w_k
        qt_scr[...] = qt
        kw_scr[...] = kw
        n_out_ref[...] = decay * n_prev + kw[:M_DK]
        m_out_ref[...] = m_new
        st_scr[0:1, :] = s
        st_scr[1:2, :] = decay
        st_scr[2:3, :] = den
        st_scr[3:4, :] = m_new
        acc_scr[...] = jnp.zeros_like(acc_scr)

    decay = st_scr[1:2, :]
    vt = vt_scr[pl.ds(0, M_DV), :]

    def tile(t, acc):
        r8 = pl.multiple_of(c * MS_DC + t * 8, 8)
        q8 = qt_scr[pl.ds(r8, 8), :]
        kw8 = kw_scr[pl.ds(r8, 8), :]
        for r in range(8):
            c_row = c0_ref[t * 8 + r]
            c_out_ref[t * 8 + r] = decay * c_row + kw8[r:r + 1, :] * vt
            acc = acc + q8[r:r + 1, :] * c_row
        return acc

    acc = lax.fori_loop(0, MS_DC // 8, tile, acc_scr[...])
    acc_scr[...] = acc

    @pl.when(c == pl.num_programs(1) - 1)
    def _():
        s = st_scr[0:1, :]
        den = st_scr[2:3, :]
        m_t = st_scr[3:4, :]
        h = (s * vt + decay * acc) / jnp.maximum(jnp.abs(den), jnp.exp(-m_t))
        mu = jnp.mean(h, axis=0, keepdims=True)
        hc = h - mu
        var = jnp.mean(hc * hc, axis=0, keepdims=True)
        o_pre = o_ref[...].T[:M_DV, :]
        out = _sigmoid(o_pre) * (hc * lax.rsqrt(var + LN_EPS) * gain_ref[...])
        out = jnp.concatenate([out, jnp.zeros((DVP - M_DV, DEC_BATCH), F32)], axis=0)
        hm_ref[...] = out.T.astype(BF16)


def mlstm_sample(proj, bias_col, gain_col, c0t, n0t, m0t, layer):
    rb = M_PROMPT // DEC_BATCH
    nc = M_DK // MS_DC
    def cblk(c0_, w):
        return pl.BlockSpec((DEC_BATCH, w), lambda h, c: (rb, c0_ // w + h))
    return pl.pallas_call(
        _mlstm_sample_kernel,
        grid=(M_HEADS, nc),
        in_specs=[cblk(C_MQ, DKP), cblk(C_MK, DKP), cblk(C_MV, DVP), cblk(C_MO, DVP),
                  pl.BlockSpec((DEC_BATCH, LANES), lambda h, c: (rb, C_IF // LANES)),
                  pl.BlockSpec((2 * M_HEADS, DEC_BATCH), lambda h, c: (0, 0)),
                  pl.BlockSpec((None, M_DV, 1), lambda h, c: (h, 0, 0)),
                  pl.BlockSpec((None, None, MS_DC, M_DV, DEC_BATCH), lambda h, c: (layer, h, c, 0, 0)),
                  pl.BlockSpec((None, None, M_DK, DEC_BATCH), lambda h, c: (layer, h, 0, 0)),
                  pl.BlockSpec((None, None, 1, DEC_BATCH), lambda h, c: (layer, h, 0, 0))],
        out_specs=[pl.BlockSpec((DEC_BATCH, DVP), lambda h, c: (0, h)),
                   pl.BlockSpec((None, MS_DC, M_DV, DEC_BATCH), lambda h, c: (h, c, 0, 0)),
                   pl.BlockSpec((None, M_DK, DEC_BATCH), lambda h, c: (h, 0, 0)),
                   pl.BlockSpec((None, 1, DEC_BATCH), lambda h, c: (h, 0, 0))],
        out_shape=[jax.ShapeDtypeStruct((DEC_BATCH, M_HEADS * DVP), BF16),
                   jax.ShapeDtypeStruct((M_HEADS, M_DK, M_DV, DEC_BATCH), F32),
                   jax.ShapeDtypeStruct((M_HEADS, M_DK, DEC_BATCH), F32),
                   jax.ShapeDtypeStruct((M_HEADS, 1, DEC_BATCH), F32)],
        scratch_shapes=[pltpu.VMEM((DKP, DEC_BATCH), F32), pltpu.VMEM((DKP, DEC_BATCH), F32),
                        pltpu.VMEM((DVP, DEC_BATCH), F32), pltpu.VMEM((M_DV, DEC_BATCH), F32),
                        pltpu.VMEM((8, DEC_BATCH), F32)],
        compiler_params=_cparams(("arbitrary", "arbitrary")),
        name="mlstm_sample",
    )(proj, proj, proj, proj, proj, bias_col, gain_col, c0t, n0t, m0t)


A_Q = 128
A_LT = A_GW // LANES


def _rope(x, cos, sin_signed):
    lane = lax.broadcasted_iota(jnp.int32, x.shape, 1)
    first_half = (lane % A_HEAD_DIM) < (A_HEAD_DIM // 2)
    partner = jnp.where(first_half, pltpu.roll(x, x.shape[1] - A_HEAD_DIM // 2, 1),
                        pltpu.roll(x, A_HEAD_DIM // 2, 1))
    return x * cos + partner * sin_signed


def _head_masks(shape):
    lane = lax.broadcasted_iota(jnp.int32, shape, 1)
    return [(lane // A_HEAD_DIM) == h for h in range(A_HPG)]


def _attn_group_prompt(dil, gi, qs_scr, ks_scr, vs_scr, o_scr, l_scr):
    length = SEQ // dil
    nb = length // A_Q
    row = lax.broadcasted_iota(jnp.int32, (A_Q, A_Q), 0)
    col = lax.broadcasted_iota(jnp.int32, (A_Q, A_Q), 1)
    cur_ok = col <= row
    prev_ok = col >= row
    masks = _head_masks((A_Q, A_GW))
    nt = (((1,), (1,)), ((), ()))

    def window(start):
        if dil == 1:
            return pl.ds(pl.multiple_of(start, A_Q), A_Q)
        return pl.ds(start, A_Q, stride=dil)

    def rows(scr, start):
        w = window(start)
        return jnp.concatenate([scr[t, w, :] for t in range(A_LT)], axis=1).astype(BF16)

    def block(idx, carry):
        r = idx % dil
        n = idx // dil
        base = r + (dil * A_Q) * n
        qb = rows(qs_scr, base)
        kc = rows(ks_scr, base)
        vc = rows(vs_scr, base)
        if nb > 1:
            pbase = jnp.maximum(base - dil * A_Q, r)
            kp = rows(ks_scr, pbase)
            vp = rows(vs_scr, pbase)
            has_prev = n > 0
        o_acc = jnp.zeros((A_Q, A_GW), F32)
        l_acc = jnp.zeros((A_Q, A_GW), F32)
        for h in range(A_HPG):
            qh = jnp.where(masks[h], qb, jnp.zeros_like(qb))
            s_c = jnp.where(cur_ok, lax.dot_general(qh, kc, nt, preferred_element_type=F32), NEG)
            m = jnp.max(s_c, axis=1, keepdims=True)
            if nb > 1:
                s_p = jnp.where(jnp.logical_and(prev_ok, has_prev),
                                lax.dot_general(qh, kp, nt, preferred_element_type=F32), NEG)
                m = jnp.maximum(m, jnp.max(s_p, axis=1, keepdims=True))
            p_c = jnp.exp(s_c - m)
            l = jnp.sum(p_c, axis=1, keepdims=True)
            o_h = jnp.dot(p_c.astype(BF16), vc, preferred_element_type=F32)
            if nb > 1:
                p_p = jnp.exp(s_p - m)
                l = l + jnp.sum(p_p, axis=1, keepdims=True)
                o_h = o_h + jnp.dot(p_p.astype(BF16), vp, preferred_element_type=F32)
            o_acc = o_acc + jnp.where(masks[h], o_h / l, 0.0)
            l_acc = l_acc + jnp.where(masks[h], m + jnp.log(l), 0.0)
        w = window(base)
        for t in range(A_LT):
            o_scr[gi, t, w, :] = o_acc[:, t * LANES:(t + 1) * LANES]
            l_scr[gi, t, w, :] = l_acc[:, t * LANES:(t + 1) * LANES]
        return carry

    lax.fori_loop(0, dil * nb, block, 0)


def _attn_prompt_kernel(q_ref, k_ref, v_ref, cos_ref, sin_ref, oa_ref, kv0_ref, kv1_ref, kv2_ref,
                        qs_scr, ks_scr, vs_scr, o_scr, l_scr):
    g = pl.program_id(1)
    rc = 256

    def rope_rows(c, carry):
        sl = pl.ds(pl.multiple_of(c * rc, rc), rc)
        cos = cos_ref[sl, :]
        sin = sin_ref[sl, :]
        for t in range(A_LT):
            lanes = pl.ds(t * LANES, LANES)
            qs_scr[t, sl, :] = _rope(q_ref[sl, lanes], cos, sin) * (A_HEAD_DIM ** -0.5)
            ks_scr[t, sl, :] = _rope(k_ref[sl, lanes], cos, sin)
            vs_scr[t, sl, :] = v_ref[sl, lanes]
        return carry

    lax.fori_loop(0, SEQ // rc, rope_rows, 0)

    for gi, ((win, dil), kv_ref) in enumerate(zip(A_GROUPS, (kv0_ref, kv1_ref, kv2_ref))):
        @pl.when(g == gi)
        def _(gi=gi, dil=dil, win=win, kv_ref=kv_ref):
            _attn_group_prompt(dil, gi, qs_scr, ks_scr, vs_scr, o_scr, l_scr)
            keep = min(win, SEQ)
            for c in range(keep // LANES):
                rows = pl.ds(SEQ - keep + c * LANES, LANES)
                for t in range(A_LT):
                    kv_ref[0, pl.ds(t * LANES, LANES), pl.ds(c * LANES, LANES)] = ks_scr[t, rows, :].T
                    kv_ref[1, pl.ds(t * LANES, LANES), pl.ds(c * LANES, LANES)] = vs_scr[t, rows, :].T

    @pl.when(g == len(A_GROUPS) - 1)
    def _():
        def comb(c, carry):
            sl = pl.ds(pl.multiple_of(c * rc, rc), rc)
            for t in range(A_LT):
                l0, l1, l2 = l_scr[0, t, sl, :], l_scr[1, t, sl, :], l_scr[2, t, sl, :]
                mx = jnp.maximum(jnp.maximum(l0, l1), l2)
                e0, e1, e2 = jnp.exp(l0 - mx), jnp.exp(l1 - mx), jnp.exp(l2 - mx)
                tot = e0 * o_scr[0, t, sl, :] + e1 * o_scr[1, t, sl, :] + e2 * o_scr[2, t, sl, :]
                oa_ref[sl, pl.ds(t * LANES, LANES)] = (tot / (e0 + e1 + e2)).astype(BF16)
            return carry
        lax.fori_loop(0, SEQ // rc, comb, 0)


def attn_prompt(proj, cos, sin):
    def gblk(c0):
        return pl.BlockSpec((SEQ, A_GW), lambda b, g: (b, c0 // A_GW + g))
    tab = pl.BlockSpec((SEQ, LANES), lambda b, g: (0, 0))
    keeps = [min(win, SEQ) for win, _ in A_GROUPS]
    return pl.pallas_call(
        _attn_prompt_kernel,
        grid=(BATCH, len(A_GROUPS)),
        in_specs=[gblk(C_AQ), gblk(C_AK), gblk(C_AV), tab, tab],
        out_specs=[pl.BlockSpec((SEQ, A_GW), lambda b, g: (b, 0))]
                  + [pl.BlockSpec((None, 2, A_GW, kp), lambda b, g: (b, 0, 0, 0)) for kp in keeps],
        out_shape=[jax.ShapeDtypeStruct((M_PROMPT, A_GW), BF16)]
                  + [jax.ShapeDtypeStruct((BATCH, 2, A_GW, kp), F32) for kp in keeps],
        scratch_shapes=[pltpu.VMEM((A_LT, SEQ, LANES), F32),
                        pltpu.VMEM((A_LT, SEQ, LANES), F32),
                        pltpu.VMEM((A_LT, SEQ, LANES), F32),
                        pltpu.VMEM((len(A_GROUPS), A_LT, SEQ, LANES), F32),
                        pltpu.VMEM((len(A_GROUPS), A_LT, SEQ, LANES), F32)],
        compiler_params=_cparams(("parallel", "arbitrary")),
        name="attn_prompt",
    )(proj, proj, proj, cos, sin)


AS_BB = 4


def _attn_sample_kernel(q_ref, k_ref, v_ref, cos_ref, sin_ref, c0_ref, c1_ref, c2_ref,
                        oa_ref, kt_ref, vt_ref, qt_scr, s0_scr, ot_scr, lt_scr):
    i = pl.program_id(0)
    lane_b = lax.broadcasted_iota(jnp.int32, (1, DEC_BATCH), 1)
    sub8 = lax.broadcasted_iota(jnp.int32, (8, DEC_BATCH), 0)

    @pl.when(i == 0)
    def _():
        cos = cos_ref[...]
        sin = sin_ref[...]
        for gi in range(len(A_GROUPS)):
            gs = pl.ds(gi * A_GW, A_GW)
            qt = (_rope(q_ref[:, gs], cos, sin) * (A_HEAD_DIM ** -0.5)).T
            kt = _rope(k_ref[:, gs], cos, sin).T
            qt_scr[gi] = qt
            kt_ref[gi] = kt
            vt_ref[gi] = v_ref[:, gs].T
            prod = qt * kt
            s0 = jnp.zeros((8, DEC_BATCH), F32)
            for h in range(A_HPG):
                part = jnp.sum(prod[h * A_HEAD_DIM:(h + 1) * A_HEAD_DIM], axis=0, keepdims=True)
                s0 = jnp.where(sub8 == h, part, s0)
            s0_scr[gi] = s0
        ot_scr[...] = jnp.zeros_like(ot_scr)
        lt_scr[...] = jnp.zeros_like(lt_scr)

    for bl in range(AS_BB):
        pick = lane_b == i * AS_BB + bl
        for gi, (cache_ref, (_, dil)) in enumerate(zip((c0_ref, c1_ref, c2_ref), A_GROUPS)):
            wb = cache_ref.shape[-1]
            if dil > 1:
                keep = (lax.broadcasted_iota(jnp.int32, (1, wb), 1) & (dil - 1)) == 0
            s0_all = s0_scr[gi]
            for h in range(A_HPG):
                hs = pl.ds(h * A_HEAD_DIM, A_HEAD_DIM)
                q_col = jnp.sum(jnp.where(pick, qt_scr[gi, hs, :], 0.0), axis=1, keepdims=True)
                v_col = jnp.sum(jnp.where(pick, vt_ref[gi, hs, :], 0.0), axis=1, keepdims=True)
                s0 = jnp.sum(jnp.where(pick, s0_all[h:h + 1, :], 0.0), axis=1, keepdims=True)
                s = jnp.sum(q_col * cache_ref[bl, 0, h], axis=0, keepdims=True)
                if dil > 1:
                    s = jnp.where(keep, s, NEG)
                m = jnp.maximum(jnp.max(s, axis=1, keepdims=True), s0)
                p = jnp.exp(s - m)
                p0 = jnp.exp(s0 - m)
                l = jnp.sum(p, axis=1, keepdims=True) + p0
                o = (jnp.sum(p * cache_ref[bl, 1, h], axis=1, keepdims=True) + p0 * v_col) / l
                ot_scr[gi, hs, :] = jnp.where(pick, o, ot_scr[gi, hs, :])
                lt_scr[gi, pl.ds(h, 1), :] = jnp.where(pick, m + jnp.log(l), lt_scr[gi, pl.ds(h, 1), :])

    @pl.when(i == pl.num_programs(0) - 1)
    def _():
        for h in range(A_HPG):
            hs = pl.ds(h * A_HEAD_DIM, A_HEAD_DIM)
            l0, l1, l2 = (lt_scr[gi, pl.ds(h, 1), :] for gi in range(3))
            mx = jnp.maximum(jnp.maximum(l0, l1), l2)
            e0, e1, e2 = jnp.exp(l0 - mx), jnp.exp(l1 - mx), jnp.exp(l2 - mx)
            tot = e0 * ot_scr[0, hs, :] + e1 * ot_scr[1, hs, :] + e2 * ot_scr[2, hs, :]
            ot_scr[0, hs, :] = tot / (e0 + e1 + e2)
        oa_ref[...] = ot_scr[0].T.astype(BF16)


def attn_sample(proj, cos, sin, caches_t, layer):
    rb = M_PROMPT // DEC_BATCH
    def pblk(c0):
        return pl.BlockSpec((DEC_BATCH, A_WIDTH), lambda i: (rb, c0 // A_WIDTH))
    tab = pl.BlockSpec((1, A_GW), lambda i: (0, 0))
    cache_specs = [pl.BlockSpec((None, AS_BB, 2, A_HPG, A_HEAD_DIM, ct.shape[-1]),
                                lambda i: (layer, i, 0, 0, 0, 0)) for ct in caches_t]
    ng = len(A_GROUPS)
    full3 = pl.BlockSpec((ng, A_GW, DEC_BATCH), lambda i: (0, 0, 0))
    return pl.pallas_call(
        _attn_sample_kernel,
        grid=(DEC_BATCH // AS_BB,),
        in_specs=[pblk(C_AQ), pblk(C_AK), pblk(C_AV), tab, tab] + cache_specs,
        out_specs=[pl.BlockSpec((DEC_BATCH, A_GW), lambda i: (0, 0)), full3, full3],
        out_shape=[jax.ShapeDtypeStruct((DEC_BATCH, A_GW), BF16),
                   jax.ShapeDtypeStruct((ng, A_GW, DEC_BATCH), F32),
                   jax.ShapeDtypeStruct((ng, A_GW, DEC_BATCH), F32)],
        scratch_shapes=[pltpu.VMEM((ng, A_GW, DEC_BATCH), F32), pltpu.VMEM((ng, 8, DEC_BATCH), F32),
                        pltpu.VMEM((ng, A_GW, DEC_BATCH), F32), pltpu.VMEM((ng, 8, DEC_BATCH), F32)],
        compiler_params=_cparams(("arbitrary",)),
        name="attn_sample",
    )(proj, proj, proj, cos, sin, *caches_t)


def _pad_heads(wt, d, dp):
    c = wt.shape[1]
    wt = wt.reshape(M_HEADS, d, c)
    return jnp.pad(wt, ((0, 0), (0, dp - d), (0, 0))).reshape(M_HEADS * dp, c)


def _layer_weights(w_in_l, w_up_m, w_up_c, w_up_a, w_o_l):
    w_in_t = jnp.transpose(w_in_l)
    o = IN_OFFSETS
    piece = lambda i: w_in_t[o[i]:o[i + 1]]
    mq, mk, mv, mi, mf, mo, cb, cc, ch, aq, ak, av, gt = [piece(i) for i in range(13)]
    w_br = jnp.concatenate(
        [_pad_heads(mv, M_DV, DVP), _pad_heads(mo, M_DV, DVP),
         _pad_heads(mq, M_DK, DKP), _pad_heads(mk, M_DK, DKP),
         cb, cc, ch, aq, ak, av,
         mi, mf, jnp.zeros((PROJ_W - C_IF - 2 * M_HEADS, D_MODEL), F32)], axis=0).astype(BF16)
    w_um = jnp.pad(w_up_m.reshape(M_HEADS, M_DV, D_MODEL),
                   ((0, 0), (0, DVP - M_DV), (0, 0))).reshape(M_HEADS * DVP, D_MODEL)
    return (w_br, gt.astype(BF16), w_um.astype(BF16), w_up_c.astype(BF16),
            w_up_a.astype(BF16), w_o_l.astype(BF16))


def _rope_tables(pos):
    half = A_HEAD_DIM // 2
    inv = ROPE_THETA ** (-(2.0 * jnp.arange(half, dtype=F32)) / A_HEAD_DIM)
    ang = pos.astype(F32)[:, None] * inv[None, :]
    cos = jnp.cos(ang)
    sin = jnp.sin(ang)
    cos = jnp.tile(jnp.concatenate([cos, cos], axis=-1), (1, A_HPG))
    sin = jnp.tile(jnp.concatenate([-sin, sin], axis=-1), (1, A_HPG))
    return cos, sin


def kernel(x_prompt, x_sample, state_mlstm_C, state_mlstm_n, state_mlstm_m, state_conv,
           cache_attn_kv_w128, cache_attn_kv_w512, cache_attn_kv_w2048,
           w_in, b_gate_if, mlstm_norm_g, conv_w, w_up_mlstm, w_up_conv, w_up_attn, w_o,
           w_ffn_in, w_ffn_out, ln_g, ln_b):
    x = jnp.concatenate([x_prompt.reshape(M_PROMPT, D_MODEL),
                         x_sample.reshape(DEC_BATCH, D_MODEL)], axis=0)
    cos_p, sin_p = _rope_tables(jnp.arange(SEQ))
    cos_s, sin_s = _rope_tables(PAST_LEN + jnp.arange(1))
    w_ffn_in_b = w_ffn_in.astype(BF16)
    w_ffn_out_b = w_ffn_out.astype(BF16)

    c0t = jnp.transpose(state_mlstm_C, (0, 2, 3, 4, 1))
    n0t = jnp.transpose(state_mlstm_n, (0, 2, 3, 1))
    m0t = jnp.transpose(state_mlstm_m, (0, 2, 1)).reshape(DEPTH, M_HEADS, 1, DEC_BATCH)
    caches_t = [jnp.transpose(c, (0, 1, 3, 4, 5, 2))
                for c in (cache_attn_kv_w128, cache_attn_kv_w512, cache_attn_kv_w2048)]

    p_states, s_states = [], []
    for l in range(DEPTH):
        w_br, w_g, w_um, w_uc, w_ua, w_ol = _layer_weights(
            w_in[l], w_up_mlstm[l], w_up_conv[l], w_up_attn[l], w_o[l])
        bias = jnp.pad(b_gate_if[l], (0, LANES - 2 * M_HEADS)).reshape(1, LANES)
        bias_col = jnp.broadcast_to(b_gate_if[l][:, None], (2 * M_HEADS, DEC_BATCH))
        gain = jnp.pad(mlstm_norm_g[l].reshape(M_HEADS, M_DV), ((0, 0), (0, DVP - M_DV)))
        gain_col = mlstm_norm_g[l].reshape(M_HEADS, M_DV, 1)

        x, xb = ffn_ln(x, w_ffn_in_b, w_ffn_out_b, ln_g[l, 0], ln_b[l, 0], l, 0)
        proj = branch_proj(xb, w_br)

        hm_p, pc, pn, pm = mlstm_prompt(proj, bias, gain.reshape(M_HEADS, 1, DVP))
        hm_s, sct, snt, smt = mlstm_sample(proj, bias_col, gain_col, c0t, n0t, m0t, l)
        yc_p, pconv = conv_prompt(proj, conv_w[l])
        yc_s, sconv = conv_sample(proj, state_conv[l], conv_w[l])
        oa_p, *kv_p = attn_prompt(proj, cos_p[:, :LANES], sin_p[:, :LANES])
        oa_s, kt_s, vt_s = attn_sample(proj, cos_s, sin_s, caches_t, l)

        hm = jnp.concatenate([hm_p, hm_s], axis=0)
        yc = jnp.concatenate([yc_p, yc_s], axis=0)
        oa = jnp.concatenate([oa_p, oa_s], axis=0)
        x = merge_ln(x, xb, hm, yc, oa, w_g, w_um, w_uc, w_ua, w_ol, ln_g[l, 1], ln_b[l, 1])
        x, _ = ffn_ln(x, w_ffn_in_b, w_ffn_out_b, ln_g[l, 2], ln_b[l, 2], l, 1)

        kt_s = kt_s.reshape(3, A_HPG, A_HEAD_DIM, DEC_BATCH)
        vt_s = vt_s.reshape(3, A_HPG, A_HEAD_DIM, DEC_BATCH)
        kv_s = [jnp.stack([kt_s[gi], vt_s[gi]], axis=0) for gi in range(3)]
        p_states.append((pc, pn.reshape(BATCH, M_HEADS, M_DK), pm.reshape(BATCH, M_HEADS), pconv,
                         kv_p[0], kv_p[1], kv_p[2]))
        s_states.append((sct, snt, smt, sconv, kv_s[0], kv_s[1], kv_s[2]))

    y_prompt = x[:M_PROMPT].reshape(BATCH, SEQ, D_MODEL)
    y_sample = x[M_PROMPT:].reshape(DEC_BATCH, 1, D_MODEL)
    p_out = [jnp.stack(z) for z in zip(*p_states)]
    for j in range(4, 7):
        kvt = p_out[j]
        p_out[j] = jnp.transpose(kvt.reshape(DEPTH, BATCH, 2, A_HPG, A_HEAD_DIM, kvt.shape[-1]),
                                 (0, 1, 5, 2, 3, 4))
    sct, snt, smt, sconv, kv0, kv1, kv2 = [jnp.stack(z) for z in zip(*s_states)]
    s_out = [jnp.transpose(sct, (0, 4, 1, 2, 3)),
             jnp.transpose(snt, (0, 3, 1, 2)),
             jnp.transpose(smt.reshape(DEPTH, M_HEADS, DEC_BATCH), (0, 2, 1)),
             sconv]
    s_out += [jnp.transpose(kv, (0, 4, 1, 2, 3)).reshape(DEPTH, DEC_BATCH, 1, 2, A_HPG, A_HEAD_DIM)
              for kv in (kv0, kv1, kv2)]
    return (y_prompt, y_sample, *p_out, *s_out)
```

```python
import functools
import math

import jax
import jax.numpy as jnp
import numpy as np
from jax import lax
from jax.experimental import pallas as pl
from jax.experimental.pallas import tpu as pltpu

F32 = jnp.float32
BF16 = jnp.bfloat16

D_MODEL = 2048
BATCH = 4
SEQ = 2048
DEPTH = 2
DEC_BATCH = 128
PAST_LEN = 2048
M_HEADS = 4
M_DV = 192
M_DK = 96
M_QK = M_HEADS * M_DK
M_WIDTH = M_HEADS * M_DV
CONV_WIDTH = 512
CONV_K = 3
A_GROUPS = ((128, 1), (512, 4), (2048, 16))
A_HPG = 4
A_HEAD_DIM = 64
A_GW = A_HPG * A_HEAD_DIM
A_WIDTH = 3 * A_GW
ROPE_THETA = 10000.0
N_BRANCH = 3
D_FF = 5632
LN_EPS = 1e-5
ALPHA = (2 * DEPTH) ** 0.25
IN_SIZES = (M_QK, M_QK, M_WIDTH, M_HEADS, M_HEADS, M_WIDTH,
            CONV_WIDTH, CONV_WIDTH, CONV_WIDTH,
            A_WIDTH, A_WIDTH, A_WIDTH, N_BRANCH * D_MODEL)
IN_OFFSETS = tuple(int(o) for o in np.cumsum((0,) + IN_SIZES))

M_PROMPT = BATCH * SEQ
M_ROWS = M_PROMPT + DEC_BATCH

LANES = 128
DKP = 128
DVP = 256
VMEM_LIMIT = 52 * 1024 * 1024

C_MV, C_MO = 0, 1024
C_MQ, C_MK = 2048, 2560
C_CB, C_CC, C_CH = 3072, 3584, 4096
C_AQ, C_AK, C_AV = 4608, 5376, 6144
C_IF = 6912
PROJ_W = 7168

NEG = -1e30


def _sigmoid(x):
    return 1.0 / (1.0 + jnp.exp(-x))


def _layer_norm(z, g, b):
    mu = jnp.mean(z, axis=-1, keepdims=True)
    zc = z - mu
    var = jnp.mean(zc * zc, axis=-1, keepdims=True)
    return zc * lax.rsqrt(var + LN_EPS) * g + b


def _cparams(sem):
    return pltpu.CompilerParams(dimension_semantics=sem, vmem_limit_bytes=VMEM_LIMIT)


FFN_TM = 640
FFN_TF = 512


def _ffn_kernel(x_ref, wa_ref, wb_ref, wo_ref, g_ref, b_ref, y_ref, yb_ref, xb_scr):
    f = pl.program_id(1)

    @pl.when(f == 0)
    def _():
        xb_scr[...] = x_ref[...].astype(BF16)
        y_ref[...] = jnp.zeros_like(y_ref)

    xb = xb_scr[...]
    a = jnp.dot(xb, wa_ref[...], preferred_element_type=F32)
    b = jnp.dot(xb, wb_ref[...], preferred_element_type=F32)
    h = (a * _sigmoid(a)) * b
    y_ref[...] += jnp.dot(h.astype(BF16), wo_ref[...], preferred_element_type=F32)

    @pl.when(f == pl.num_programs(1) - 1)
    def _():
        z = ALPHA * x_ref[...] + 0.5 * y_ref[...]
        out = _layer_norm(z, g_ref[...], b_ref[...])
        y_ref[...] = out
        yb_ref[...] = out.astype(BF16)


def ffn_ln(x, w_in, w_out, g, b, layer, which):
    m = x.shape[0]
    nf = D_FF // FFN_TF
    return pl.pallas_call(
        _ffn_kernel,
        grid=(m // FFN_TM, nf),
        in_specs=[
            pl.BlockSpec((FFN_TM, D_MODEL), lambda i, f: (i, 0)),
            pl.BlockSpec((None, None, D_MODEL, FFN_TF), lambda i, f: (layer, which, 0, f)),
            pl.BlockSpec((None, None, D_MODEL, FFN_TF), lambda i, f: (layer, which, 0, f + nf)),
            pl.BlockSpec((None, None, FFN_TF, D_MODEL), lambda i, f: (layer, which, f, 0)),
            pl.BlockSpec((1, D_MODEL), lambda i, f: (0, 0)),
            pl.BlockSpec((1, D_MODEL), lambda i, f: (0, 0)),
        ],
        out_specs=[
            pl.BlockSpec((FFN_TM, D_MODEL), lambda i, f: (i, 0)),
            pl.BlockSpec((FFN_TM, D_MODEL), lambda i, f: (i, 0)),
        ],
        out_shape=[jax.ShapeDtypeStruct((m, D_MODEL), F32),
                   jax.ShapeDtypeStruct((m, D_MODEL), BF16)],
        scratch_shapes=[pltpu.VMEM((FFN_TM, D_MODEL), BF16)],
        compiler_params=_cparams(("parallel", "arbitrary")),
        name="ffn_ln",
    )(x, w_in, w_in, w_out, g.reshape(1, D_MODEL), b.reshape(1, D_MODEL))


PROJ_TM = 1040
PROJ_TN = 1024


NT_DIMS = (((1,), (1,)), ((), ()))


def _proj_kernel(xb_ref, wt_ref, o_ref):
    o_ref[...] = lax.dot_general(xb_ref[...], wt_ref[...], NT_DIMS, preferred_element_type=F32)


def branch_proj(xb, wt):
    m = xb.shape[0]
    return pl.pallas_call(
        _proj_kernel,
        grid=(m // PROJ_TM, PROJ_W // PROJ_TN),
        in_specs=[pl.BlockSpec((PROJ_TM, D_MODEL), lambda i, j: (i, 0)),
                  pl.BlockSpec((PROJ_TN, D_MODEL), lambda i, j: (j, 0))],
        out_specs=pl.BlockSpec((PROJ_TM, PROJ_TN), lambda i, j: (i, j)),
        out_shape=jax.ShapeDtypeStruct((m, PROJ_W), F32),
        compiler_params=_cparams(("parallel", "parallel")),
        name="branch_proj",
    )(xb, wt)


MRG_TM = 640
MRG_TN = 512


def _gate_up_kernel(xb_ref, hm_ref, yc_ref, oa_ref, wg0_ref, wg1_ref, wg2_ref,
                    wum_ref, wuc_ref, wua_ref, o_ref):
    xb = xb_ref[...]

    def gated(wg_ref, br_ref, wu_ref):
        gate = _sigmoid(lax.dot_general(xb, wg_ref[...], NT_DIMS, preferred_element_type=F32))
        return gate * jnp.dot(br_ref[...], wu_ref[...], preferred_element_type=F32)

    merged = (gated(wg0_ref, hm_ref, wum_ref) + gated(wg1_ref, yc_ref, wuc_ref)
              + gated(wg2_ref, oa_ref, wua_ref))
    o_ref[...] = merged.astype(BF16)


def _out_ln_kernel(x_ref, mg_ref, wo_ref, g_ref, b_ref, y_ref):
    z = ALPHA * x_ref[...] + jnp.dot(mg_ref[...], wo_ref[...], preferred_element_type=F32)
    y_ref[...] = _layer_norm(z, g_ref[...], b_ref[...])


def merge_ln(x, xb, hm, yc, oa, wg, wum, wuc, wua, wo, g, b):
    m = x.shape[0]
    nn = D_MODEL // MRG_TN
    row = lambda w: pl.BlockSpec((MRG_TM, w), lambda i, n: (i, 0))
    merged = pl.pallas_call(
        _gate_up_kernel,
        grid=(m // MRG_TM, nn),
        in_specs=[
            row(D_MODEL), row(M_HEADS * DVP), row(CONV_WIDTH), row(A_GW),
            pl.BlockSpec((MRG_TN, D_MODEL), lambda i, n: (n, 0)),
            pl.BlockSpec((MRG_TN, D_MODEL), lambda i, n: (n + nn, 0)),
            pl.BlockSpec((MRG_TN, D_MODEL), lambda i, n: (n + 2 * nn, 0)),
            pl.BlockSpec((M_HEADS * DVP, MRG_TN), lambda i, n: (0, n)),
            pl.BlockSpec((CONV_WIDTH, MRG_TN), lambda i, n: (0, n)),
            pl.BlockSpec((A_GW, MRG_TN), lambda i, n: (0, n)),
        ],
        out_specs=pl.BlockSpec((MRG_TM, MRG_TN), lambda i, n: (i, n)),
        out_shape=jax.ShapeDtypeStruct((m, D_MODEL), BF16),
        compiler_params=_cparams(("parallel", "parallel")),
        name="gate_up",
    )(xb, hm, yc, oa, wg, wg, wg, wum, wuc, wua)
    rows = pl.BlockSpec((MRG_TM, D_MODEL), lambda i: (i, 0))
    vec = pl.BlockSpec((1, D_MODEL), lambda i: (0, 0))
    return pl.pallas_call(
        _out_ln_kernel,
        grid=(m // MRG_TM,),
        in_specs=[rows, rows, pl.BlockSpec((D_MODEL, D_MODEL), lambda i: (0, 0)), vec, vec],
        out_specs=rows,
        out_shape=jax.ShapeDtypeStruct((m, D_MODEL), F32),
        compiler_params=_cparams(("parallel",)),
        name="out_ln",
    )(x, merged, wo, g.reshape(1, D_MODEL), b.reshape(1, D_MODEL))


def _conv_prompt_kernel(cb_ref, cc_ref, ch_ref, w_ref, y_dst_ref, y_ref, st_ref, u_scr):
    del y_dst_ref
    u = cc_ref[...] * ch_ref[...]
    u_scr[pl.ds(0, 8), :] = jnp.zeros((8, CONV_WIDTH), F32)
    u_scr[pl.ds(8, SEQ), :] = u
    w = w_ref[...]
    acc = (w[0:1, :] * u_scr[pl.ds(6, SEQ), :] + w[1:2, :] * u_scr[pl.ds(7, SEQ), :]
           + w[2:3, :] * u)
    y_ref[...] = (cb_ref[...] * acc).astype(BF16)
    st_ref[...] = u_scr[pl.ds(8 + SEQ - (CONV_K - 1), CONV_K - 1), :]


def conv_prompt(proj, conv_w, y_dst):
    blk = lambda c: pl.BlockSpec((SEQ, CONV_WIDTH), lambda b, c=c: (b, c // CONV_WIDTH))
    return pl.pallas_call(
        _conv_prompt_kernel,
        grid=(BATCH,),
        in_specs=[blk(C_CB), blk(C_CC), blk(C_CH),
                  pl.BlockSpec((CONV_K, CONV_WIDTH), lambda b: (0, 0)),
                  pl.BlockSpec(memory_space=pl.ANY)],
        out_specs=[pl.BlockSpec((SEQ, CONV_WIDTH), lambda b: (b, 0)),
                   pl.BlockSpec((None, CONV_K - 1, CONV_WIDTH), lambda b: (b, 0, 0))],
        out_shape=[jax.ShapeDtypeStruct((M_ROWS, CONV_WIDTH), BF16),
                   jax.ShapeDtypeStruct((BATCH, CONV_K - 1, CONV_WIDTH), F32)],
        input_output_aliases={4: 0},
        scratch_shapes=[pltpu.VMEM((SEQ + 8, CONV_WIDTH), F32)],
        compiler_params=_cparams(("parallel",)),
        name="conv_prompt",
    )(proj, proj, proj, conv_w, y_dst)


def _conv_sample_kernel(cb_ref, cc_ref, ch_ref, prev_ref, w_ref, y_dst_ref, y_ref, st_ref):
    del y_dst_ref
    u = cc_ref[...] * ch_ref[...]
    w = w_ref[...]
    p0 = prev_ref[:, 0, :]
    p1 = prev_ref[:, 1, :]
    acc = w[0:1, :] * p0 + w[1:2, :] * p1 + w[2:3, :] * u
    y_ref[...] = (cb_ref[...] * acc).astype(BF16)
    st_ref[:, 0, :] = p1
    st_ref[:, 1, :] = u


def conv_sample(proj, prev, conv_w, y_dst):
    rb = M_PROMPT // DEC_BATCH
    blk = lambda c: pl.BlockSpec((DEC_BATCH, CONV_WIDTH), lambda i, c=c: (rb, c // CONV_WIDTH))
    full3 = pl.BlockSpec((DEC_BATCH, CONV_K - 1, CONV_WIDTH), lambda i: (0, 0, 0))
    return pl.pallas_call(
        _conv_sample_kernel,
        grid=(1,),
        in_specs=[blk(C_CB), blk(C_CC), blk(C_CH), full3,
                  pl.BlockSpec((CONV_K, CONV_WIDTH), lambda i: (0, 0)),
                  pl.BlockSpec(memory_space=pl.ANY)],
        out_specs=[pl.BlockSpec((DEC_BATCH, CONV_WIDTH), lambda i: (rb, 0)), full3],
        out_shape=[jax.ShapeDtypeStruct((M_ROWS, CONV_WIDTH), BF16),
                   jax.ShapeDtypeStruct((DEC_BATCH, CONV_K - 1, CONV_WIDTH), F32)],
        input_output_aliases={5: 0},
        compiler_params=_cparams(("arbitrary",)),
        name="conv_sample",
    )(proj, proj, proj, prev, conv_w, y_dst)


M_L = 128


def _log_sigmoid(x):
    return jnp.minimum(x, 0.0) - jnp.log1p(jnp.exp(-jnp.abs(x)))


def _head_norm_gate(h, o_pre, gain):
    lane = lax.broadcasted_iota(jnp.int32, h.shape, 1)
    real = lane < M_DV
    mu = jnp.sum(h, axis=-1, keepdims=True) * (1.0 / M_DV)
    hc = jnp.where(real, h - mu, 0.0)
    var = jnp.sum(hc * hc, axis=-1, keepdims=True) * (1.0 / M_DV)
    return _sigmoid(o_pre) * (hc * lax.rsqrt(var + LN_EPS) * gain)


M_TS = 512


M_NROW = M_DV


def _mlstm_prompt_kernel(q_ref, k_ref, v_ref, o_ref, if_ref, bias_ref, gain_ref, hm_dst_ref,
                         hm_ref, c_out_ref, n_out_ref, m_out_ref, ct_scr, m_scr):
    del hm_dst_ref
    step = pl.program_id(1)

    @pl.when(step == 0)
    def _():
        ct_scr[...] = jnp.zeros_like(ct_scr)
        m_scr[...] = jnp.zeros_like(m_scr)

    row = lax.broadcasted_iota(jnp.int32, (M_L, M_L), 0)
    col = lax.broadcasted_iota(jnp.int32, (M_L, M_L), 1)
    causal_t = row <= col
    tri = (col <= row).astype(F32)
    bias = bias_ref[...]
    ones_lane = lax.broadcasted_iota(jnp.int32, (M_L, DVP), 1) == M_NROW
    real_rows = lax.broadcasted_iota(jnp.int32, (DVP, M_L), 0) < M_DV
    tn = (((0,), (0,)), ((), ()))

    def chunk(c, carry):
        r0 = pl.multiple_of(c * M_L, M_L)
        x_if = if_ref[pl.ds(r0, M_L), :] + bias
        log_f = _log_sigmoid(x_if)
        cs = jnp.dot(tri, log_f, preferred_element_type=F32, precision=lax.Precision.HIGHEST)
        zt = jnp.where(col < M_HEADS, x_if, cs).T
        for hd in range(M_HEADS):
            b_row = zt[M_HEADS + hd:M_HEADS + hd + 1, :]
            c_col = x_if[:, hd:hd + 1] - cs[:, M_HEADS + hd:M_HEADS + hd + 1]
            c_rep = jnp.broadcast_to(c_col, (M_L, M_L))
            m_prev = m_scr[hd]
            b_last = b_row[:, M_L - 1:M_L]

            d_t = jnp.where(causal_t, b_row + c_rep, NEG)
            inter = b_row + m_prev
            m_t = jnp.maximum(jnp.max(d_t, axis=0, keepdims=True), inter)
            q = q_ref[pl.ds(r0, M_L), pl.ds(hd * DKP, DKP)].astype(BF16)
            k = k_ref[pl.ds(r0, M_L), pl.ds(hd * DKP, DKP)] * (M_DK ** -0.5)
            v1 = jnp.where(ones_lane, 1.0, v_ref[pl.ds(r0, M_L), pl.ds(hd * DVP, DVP)]).astype(BF16)
            s_t = lax.dot_general(k.astype(BF16), q, NT_DIMS, preferred_element_type=F32) * jnp.exp(d_t - m_t)
            w_inter = jnp.exp(inter - m_t)
            ct_prev = ct_scr[hd]
            num_t = (lax.dot_general(v1, s_t.astype(BF16), tn, preferred_element_type=F32)
                     + w_inter * lax.dot_general(ct_prev.astype(BF16), q, NT_DIMS,
                                                 preferred_element_type=F32))
            den = num_t[M_NROW:M_NROW + 1, :]
            h_t = jnp.where(real_rows, num_t / jnp.maximum(jnp.abs(den), jnp.exp(-m_t)), 0.0)
            mu = jnp.sum(h_t, axis=0, keepdims=True) * (1.0 / M_DV)
            hc = jnp.where(real_rows, h_t - mu, 0.0)
            var = jnp.sum(hc * hc, axis=0, keepdims=True) * (1.0 / M_DV)
            hn = (hc * lax.rsqrt(var + LN_EPS)).T * gain_ref[hd]
            o_pre = o_ref[pl.ds(r0, M_L), pl.ds(hd * DVP, DVP)]
            hm_ref[pl.ds(r0, M_L), pl.ds(hd * DVP, DVP)] = (_sigmoid(o_pre) * hn).astype(BF16)

            m_new = jnp.maximum(b_last + m_prev, b_last + jnp.max(c_rep, axis=0, keepdims=True)[:, 0:1])
            kw = k * jnp.exp(c_rep + (b_last - m_new))
            ct_scr[hd] = (jnp.exp(b_last + m_prev - m_new) * ct_prev
                          + lax.dot_general(v1, kw.astype(BF16), tn, preferred_element_type=F32))
            m_scr[hd] = m_new
        return carry

    lax.fori_loop(0, M_TS // M_L, chunk, 0)

    @pl.when(step == pl.num_programs(1) - 1)
    def _():
        for hd in range(M_HEADS):
            ct = ct_scr[hd]
            c_out_ref[hd] = ct.T[0:M_DK, 0:M_DV]
            n_out_ref[hd] = ct[M_NROW:M_NROW + 1, 0:M_DK]
            m_out_ref[hd] = m_scr[hd]


def mlstm_prompt(proj, bias, gain, hm_dst):
    ns = SEQ // M_TS
    def cblk(c0, w):
        return pl.BlockSpec((M_TS, w), lambda b, s: (b * ns + s, c0 // w))
    return pl.pallas_call(
        _mlstm_prompt_kernel,
        grid=(BATCH, ns),
        in_specs=[cblk(C_MQ, M_HEADS * DKP), cblk(C_MK, M_HEADS * DKP),
                  cblk(C_MV, M_HEADS * DVP), cblk(C_MO, M_HEADS * DVP),
                  cblk(C_IF, LANES),
                  pl.BlockSpec((1, LANES), lambda b, s: (0, 0)),
                  pl.BlockSpec((M_HEADS, 1, DVP), lambda b, s: (0, 0, 0)),
                  pl.BlockSpec(memory_space=pl.ANY)],
        out_specs=[pl.BlockSpec((M_TS, M_HEADS * DVP), lambda b, s: (b * ns + s, 0)),
                   pl.BlockSpec((None, M_HEADS, M_DK, M_DV), lambda b, s: (b, 0, 0, 0)),
                   pl.BlockSpec((None, M_HEADS, 1, M_DK), lambda b, s: (b, 0, 0, 0)),
                   pl.BlockSpec((None, M_HEADS, 1, 1), lambda b, s: (b, 0, 0, 0))],
        out_shape=[jax.ShapeDtypeStruct((M_ROWS, M_HEADS * DVP), BF16),
                   jax.ShapeDtypeStruct((BATCH, M_HEADS, M_DK, M_DV), F32),
                   jax.ShapeDtypeStruct((BATCH, M_HEADS, 1, M_DK), F32),
                   jax.ShapeDtypeStruct((BATCH, M_HEADS, 1, 1), F32)],
        scratch_shapes=[pltpu.VMEM((M_HEADS, DVP, DKP), F32), pltpu.VMEM((M_HEADS, 1, 1), F32)],
        compiler_params=_cparams(("parallel", "arbitrary")),
        input_output_aliases={7: 0},
        name="mlstm_prompt",
    )(proj, proj, proj, proj, proj, bias, gain, hm_dst)


MS_DC = 48


def _pick_row(x8, j):
    rows = lax.broadcasted_iota(jnp.int32, x8.shape, 0)
    return jnp.sum(jnp.where(rows == j, x8, 0.0), axis=0, keepdims=True)


def _mlstm_sample_kernel(q_ref, k_ref, v_ref, o_ref, if_ref, bias_ref, gain_ref,
                         c0_ref, n0_ref, m0_ref, hm_dst_ref, c_dst_ref,
                         hm_ref, c_out_ref, n_out_ref, m_out_ref,
                         qt_scr, kw_scr, vt_scr, acc_scr, st_scr):
    del hm_dst_ref, c_dst_ref
    hd = pl.program_id(0)
    c = pl.program_id(1)

    @pl.when(c == 0)
    def _():
        qt = q_ref[...].T
        kt = (k_ref[...] * (M_DK ** -0.5)).T
        vt_scr[...] = v_ref[...].T
        x_if = if_ref[...].T[0:2 * M_HEADS, :] + bias_ref[...]
        i_pre = _pick_row(x_if, hd)
        log_f = _log_sigmoid(_pick_row(x_if, hd + M_HEADS))
        inter = log_f + m0_ref[...]
        m_new = jnp.maximum(i_pre, inter)
        w_k = jnp.exp(i_pre - m_new)
        decay = jnp.exp(inter - m_new)
        n_prev = n0_ref[...]
        s = jnp.sum(qt * kt, axis=0, keepdims=True) * w_k
        den = s + decay * jnp.sum(qt[:M_DK] * n_prev, axis=0, keepdims=True)
        kw = kt * w_k
        qt_scr[...] = qt
        kw_scr[...] = kw
        n_out_ref[...] = decay * n_prev + kw[:M_DK]
        m_out_ref[...] = m_new
        st_scr[0:1, :] = s
        st_scr[1:2, :] = decay
        st_scr[2:3, :] = den
        st_scr[3:4, :] = m_new
        acc_scr[...] = jnp.zeros_like(acc_scr)

    decay = st_scr[1:2, :]
    vt = vt_scr[pl.ds(0, M_DV), :]

    def tile(t, acc):
        r8 = pl.multiple_of(c * MS_DC + t * 8, 8)
        q8 = qt_scr[pl.ds(r8, 8), :]
        kw8 = kw_scr[pl.ds(r8, 8), :]
        for r in range(8):
            c_row = c0_ref[t * 8 + r]
            c_out_ref[t * 8 + r] = decay * c_row + kw8[r:r + 1, :] * vt
            acc = acc + q8[r:r + 1, :] * c_row
        return acc

    acc = lax.fori_loop(0, MS_DC // 8, tile, acc_scr[...])
    acc_scr[...] = acc

    @pl.when(c == pl.num_programs(1) - 1)
    def _():
        s = st_scr[0:1, :]
        den = st_scr[2:3, :]
        m_t = st_scr[3:4, :]
        h = (s * vt + decay * acc) / jnp.maximum(jnp.abs(den), jnp.exp(-m_t))
        mu = jnp.mean(h, axis=0, keepdims=True)
        hc = h - mu
        var = jnp.mean(hc * hc, axis=0, keepdims=True)
        o_pre = o_ref[...].T[:M_DV, :]
        out = _sigmoid(o_pre) * (hc * lax.rsqrt(var + LN_EPS) * gain_ref[...])
        out = jnp.concatenate([out, jnp.zeros((DVP - M_DV, DEC_BATCH), F32)], axis=0)
        hm_ref[...] = out.T.astype(BF16)


def mlstm_sample(proj, bias_col, gain_col, c0t, n0t, m0t, hm_dst, c_dst, layer):
    any_spec = pl.BlockSpec(memory_space=pl.ANY)
    rb = M_PROMPT // DEC_BATCH
    nc = M_DK // MS_DC
    def cblk(c0_, w):
        return pl.BlockSpec((DEC_BATCH, w), lambda h, c: (rb, c0_ // w + h))
    return pl.pallas_call(
        _mlstm_sample_kernel,
        grid=(M_HEADS, nc),
        in_specs=[cblk(C_MQ, DKP), cblk(C_MK, DKP), cblk(C_MV, DVP), cblk(C_MO, DVP),
                  pl.BlockSpec((DEC_BATCH, LANES), lambda h, c: (rb, C_IF // LANES)),
                  pl.BlockSpec((2 * M_HEADS, DEC_BATCH), lambda h, c: (0, 0)),
                  pl.BlockSpec((None, M_DV, 1), lambda h, c: (h, 0, 0)),
                  pl.BlockSpec((None, None, MS_DC, M_DV, DEC_BATCH), lambda h, c: (layer, h, c, 0, 0)),
                  pl.BlockSpec((None, None, M_DK, DEC_BATCH), lambda h, c: (layer, h, 0, 0)),
                  pl.BlockSpec((None, None, 1, DEC_BATCH), lambda h, c: (layer, h, 0, 0)),
                  any_spec, any_spec],
        out_specs=[pl.BlockSpec((DEC_BATCH, DVP), lambda h, c: (rb, h)),
                   pl.BlockSpec((None, None, MS_DC, M_DV, DEC_BATCH), lambda h, c: (layer, h, c, 0, 0)),
                   pl.BlockSpec((None, M_DK, DEC_BATCH), lambda h, c: (h, 0, 0)),
                   pl.BlockSpec((None, 1, DEC_BATCH), lambda h, c: (h, 0, 0))],
        out_shape=[jax.ShapeDtypeStruct((M_ROWS, M_HEADS * DVP), BF16),
                   jax.ShapeDtypeStruct((DEPTH, M_HEADS, M_DK, M_DV, DEC_BATCH), F32),
                   jax.ShapeDtypeStruct((M_HEADS, M_DK, DEC_BATCH), F32),
                   jax.ShapeDtypeStruct((M_HEADS, 1, DEC_BATCH), F32)],
        scratch_shapes=[pltpu.VMEM((DKP, DEC_BATCH), F32), pltpu.VMEM((DKP, DEC_BATCH), F32),
                        pltpu.VMEM((DVP, DEC_BATCH), F32), pltpu.VMEM((M_DV, DEC_BATCH), F32),
                        pltpu.VMEM((8, DEC_BATCH), F32)],
        input_output_aliases={10: 0, 11: 1},
        compiler_params=_cparams(("arbitrary", "arbitrary")),
        name="mlstm_sample",
    )(proj, proj, proj, proj, proj, bias_col, gain_col, c0t, n0t, m0t, hm_dst, c_dst)


A_Q = 128
A_LT = A_GW // LANES


def _rope(x, cos, sin_signed):
    lane = lax.broadcasted_iota(jnp.int32, x.shape, 1)
    first_half = (lane % A_HEAD_DIM) < (A_HEAD_DIM // 2)
    partner = jnp.where(first_half, pltpu.roll(x, x.shape[1] - A_HEAD_DIM // 2, 1),
                        pltpu.roll(x, A_HEAD_DIM // 2, 1))
    return x * cos + partner * sin_signed


def _head_masks(shape):
    lane = lax.broadcasted_iota(jnp.int32, shape, 1)
    return [(lane // A_HEAD_DIM) == h for h in range(A_HPG)]


def _attn_group_prompt(dil, gi, qs_scr, ks_scr, vs_scr, o_scr, l_scr):
    length = SEQ // dil
    nb = length // A_Q
    row = lax.broadcasted_iota(jnp.int32, (A_Q, A_Q), 0)
    col = lax.broadcasted_iota(jnp.int32, (A_Q, A_Q), 1)
    cur_ok = col <= row
    prev_ok = col >= row
    masks = _head_masks((A_Q, A_GW))
    nt = (((1,), (1,)), ((), ()))

    def window(start):
        if dil == 1:
            return pl.ds(pl.multiple_of(start, A_Q), A_Q)
        return pl.ds(start, A_Q, stride=dil)

    def rows(scr, start):
        w = window(start)
        return jnp.concatenate([scr[t, w, :] for t in range(A_LT)], axis=1).astype(BF16)

    def block(idx, carry):
        r = idx % dil
        n = idx // dil
        base = r + (dil * A_Q) * n
        qb = rows(qs_scr, base)
        kc = rows(ks_scr, base)
        vc = rows(vs_scr, base)
        if nb > 1:
            pbase = jnp.maximum(base - dil * A_Q, r)
            kp = rows(ks_scr, pbase)
            vp = rows(vs_scr, pbase)
            has_prev = n > 0
        o_acc = jnp.zeros((A_Q, A_GW), F32)
        l_acc = jnp.zeros((A_Q, A_GW), F32)
        for h in range(A_HPG):
            qh = jnp.where(masks[h], qb, jnp.zeros_like(qb))
            s_c = jnp.where(cur_ok, lax.dot_general(qh, kc, nt, preferred_element_type=F32), NEG)
            m = jnp.max(s_c, axis=1, keepdims=True)
            if nb > 1:
                s_p = jnp.where(jnp.logical_and(prev_ok, has_prev),
                                lax.dot_general(qh, kp, nt, preferred_element_type=F32), NEG)
                m = jnp.maximum(m, jnp.max(s_p, axis=1, keepdims=True))
            p_c = jnp.exp(s_c - m)
            l = jnp.sum(p_c, axis=1, keepdims=True)
            o_h = jnp.dot(p_c.astype(BF16), vc, preferred_element_type=F32)
            if nb > 1:
                p_p = jnp.exp(s_p - m)
                l = l + jnp.sum(p_p, axis=1, keepdims=True)
                o_h = o_h + jnp.dot(p_p.astype(BF16), vp, preferred_element_type=F32)
            o_acc = o_acc + jnp.where(masks[h], o_h / l, 0.0)
            l_acc = l_acc + jnp.where(masks[h], m + jnp.log(l), 0.0)
        w = window(base)
        for t in range(A_LT):
            o_scr[gi, t, w, :] = o_acc[:, t * LANES:(t + 1) * LANES]
            l_scr[gi, t, w, :] = l_acc[:, t * LANES:(t + 1) * LANES]
        return carry

    lax.fori_loop(0, dil * nb, block, 0)


def _attn_prompt_kernel(q_ref, k_ref, v_ref, cos_ref, sin_ref, oa_dst_ref, kvd0_ref, kvd1_ref, kvd2_ref,
                        oa_ref, kv0_ref, kv1_ref, kv2_ref, qs_scr, ks_scr, vs_scr, o_scr, l_scr):
    del oa_dst_ref, kvd0_ref, kvd1_ref, kvd2_ref
    g = pl.program_id(1)
    rc = 256

    def rope_rows(c, carry):
        sl = pl.ds(pl.multiple_of(c * rc, rc), rc)
        cos = cos_ref[sl, :]
        sin = sin_ref[sl, :]
        for t in range(A_LT):
            lanes = pl.ds(t * LANES, LANES)
            qs_scr[t, sl, :] = _rope(q_ref[sl, lanes], cos, sin) * (A_HEAD_DIM ** -0.5)
            ks_scr[t, sl, :] = _rope(k_ref[sl, lanes], cos, sin)
            vs_scr[t, sl, :] = v_ref[sl, lanes]
        return carry

    lax.fori_loop(0, SEQ // rc, rope_rows, 0)

    for gi, ((win, dil), kv_ref) in enumerate(zip(A_GROUPS, (kv0_ref, kv1_ref, kv2_ref))):
        @pl.when(g == gi)
        def _(gi=gi, dil=dil, win=win, kv_ref=kv_ref):
            _attn_group_prompt(dil, gi, qs_scr, ks_scr, vs_scr, o_scr, l_scr)
            keep = min(win, SEQ)
            for c in range(keep // LANES):
                rows = pl.ds(SEQ - keep + c * LANES, LANES)
                for t in range(A_LT):
                    kv_ref[0, pl.ds(t * LANES, LANES), pl.ds(c * LANES, LANES)] = ks_scr[t, rows, :].T
                    kv_ref[1, pl.ds(t * LANES, LANES), pl.ds(c * LANES, LANES)] = vs_scr[t, rows, :].T

    @pl.when(g == len(A_GROUPS) - 1)
    def _():
        def comb(c, carry):
            sl = pl.ds(pl.multiple_of(c * rc, rc), rc)
            for t in range(A_LT):
                l0, l1, l2 = l_scr[0, t, sl, :], l_scr[1, t, sl, :], l_scr[2, t, sl, :]
                mx = jnp.maximum(jnp.maximum(l0, l1), l2)
                e0, e1, e2 = jnp.exp(l0 - mx), jnp.exp(l1 - mx), jnp.exp(l2 - mx)
                tot = e0 * o_scr[0, t, sl, :] + e1 * o_scr[1, t, sl, :] + e2 * o_scr[2, t, sl, :]
                oa_ref[sl, pl.ds(t * LANES, LANES)] = (tot / (e0 + e1 + e2)).astype(BF16)
            return carry
        lax.fori_loop(0, SEQ // rc, comb, 0)


def attn_prompt(proj, cos, sin, oa_dst, kv_dst, layer):
    def gblk(c0):
        return pl.BlockSpec((SEQ, A_GW), lambda b, g: (b, c0 // A_GW + g))
    tab = pl.BlockSpec((SEQ, LANES), lambda b, g: (0, 0))
    keeps = [min(win, SEQ) for win, _ in A_GROUPS]
    return pl.pallas_call(
        _attn_prompt_kernel,
        grid=(BATCH, len(A_GROUPS)),
        in_specs=[gblk(C_AQ), gblk(C_AK), gblk(C_AV), tab, tab] + [pl.BlockSpec(memory_space=pl.ANY)] * 4,
        out_specs=[pl.BlockSpec((SEQ, A_GW), lambda b, g: (b, 0))]
                  + [pl.BlockSpec((None, None, 2, A_GW, kp), lambda b, g: (layer, b, 0, 0, 0)) for kp in keeps],
        out_shape=[jax.ShapeDtypeStruct((M_ROWS, A_GW), BF16)]
                  + [jax.ShapeDtypeStruct((DEPTH, BATCH, 2, A_GW, kp), F32) for kp in keeps],
        input_output_aliases={5: 0, 6: 1, 7: 2, 8: 3},
        scratch_shapes=[pltpu.VMEM((A_LT, SEQ, LANES), F32),
                        pltpu.VMEM((A_LT, SEQ, LANES), F32),
                        pltpu.VMEM((A_LT, SEQ, LANES), F32),
                        pltpu.VMEM((len(A_GROUPS), A_LT, SEQ, LANES), F32),
                        pltpu.VMEM((len(A_GROUPS), A_LT, SEQ, LANES), F32)],
        compiler_params=_cparams(("parallel", "arbitrary")),
        name="attn_prompt",
    )(proj, proj, proj, cos, sin, oa_dst, *kv_dst)


AS_BB = 2


def _attn_sample_kernel(q_ref, k_ref, v_ref, cos_ref, sin_ref, c0_ref, c1_ref, c2_ref, oa_dst_ref,
                        oa_ref, kt_ref, vt_ref, qt_scr, s0_scr, ot_scr, lt_scr):
    del oa_dst_ref
    i = pl.program_id(0)
    lane_b = lax.broadcasted_iota(jnp.int32, (1, DEC_BATCH), 1)
    sub8 = lax.broadcasted_iota(jnp.int32, (8, DEC_BATCH), 0)

    @pl.when(i == 0)
    def _():
        cos = cos_ref[...]
        sin = sin_ref[...]
        for gi in range(len(A_GROUPS)):
            gs = pl.ds(gi * A_GW, A_GW)
            qt = (_rope(q_ref[:, gs], cos, sin) * (A_HEAD_DIM ** -0.5)).T
            kt = _rope(k_ref[:, gs], cos, sin).T
            qt_scr[gi] = qt
            kt_ref[gi] = kt
            vt_ref[gi] = v_ref[:, gs].T
            prod = qt * kt
            s0 = jnp.zeros((8, DEC_BATCH), F32)
            for h in range(A_HPG):
                part = jnp.sum(prod[h * A_HEAD_DIM:(h + 1) * A_HEAD_DIM], axis=0, keepdims=True)
                s0 = jnp.where(sub8 == h, part, s0)
            s0_scr[gi] = s0
        ot_scr[...] = jnp.zeros_like(ot_scr)
        lt_scr[...] = jnp.zeros_like(lt_scr)

    for bl in range(AS_BB):
        pick = lane_b == i * AS_BB + bl
        for gi, (cache_ref, (_, dil)) in enumerate(zip((c0_ref, c1_ref, c2_ref), A_GROUPS)):
            wb = cache_ref.shape[-1]
            if dil > 1:
                keep = (lax.broadcasted_iota(jnp.int32, (1, wb), 1) & (dil - 1)) == 0
            head_row = lax.broadcasted_iota(jnp.int32, (8, wb), 0)
            q_col = jnp.sum(jnp.where(pick, qt_scr[gi], 0.0), axis=1, keepdims=True)
            v_col = jnp.sum(jnp.where(pick, vt_ref[gi], 0.0), axis=1, keepdims=True)
            s0 = jnp.sum(jnp.where(pick, s0_scr[gi], 0.0), axis=1, keepdims=True)
            s = jnp.zeros((8, wb), F32)
            for h in range(A_HPG):
                hs = slice(h * A_HEAD_DIM, (h + 1) * A_HEAD_DIM)
                part = jnp.sum(q_col[hs] * cache_ref[bl, 0, h], axis=0, keepdims=True)
                s = jnp.where(head_row == h, part, s)
            if dil > 1:
                s = jnp.where(keep, s, NEG)
            m = jnp.maximum(jnp.max(s, axis=1, keepdims=True), s0)
            p = jnp.exp(s - m)
            p0 = jnp.exp(s0 - m)
            l = jnp.sum(p, axis=1, keepdims=True) + p0
            lse = m + jnp.log(l)
            o_parts = []
            for h in range(A_HPG):
                hs = slice(h * A_HEAD_DIM, (h + 1) * A_HEAD_DIM)
                pv = jnp.sum(p[h:h + 1, :] * cache_ref[bl, 1, h], axis=1, keepdims=True)
                o_parts.append((pv + p0[h:h + 1, :] * v_col[hs]) / l[h:h + 1, :])
            o = jnp.concatenate(o_parts, axis=0)
            ot_scr[gi] = jnp.where(pick, o, ot_scr[gi])
            lt_scr[gi] = jnp.where(pick, lse, lt_scr[gi])

    @pl.when(i == pl.num_programs(0) - 1)
    def _():
        for h in range(A_HPG):
            hs = pl.ds(h * A_HEAD_DIM, A_HEAD_DIM)
            l0, l1, l2 = (lt_scr[gi, pl.ds(h, 1), :] for gi in range(3))
            mx = jnp.maximum(jnp.maximum(l0, l1), l2)
            e0, e1, e2 = jnp.exp(l0 - mx), jnp.exp(l1 - mx), jnp.exp(l2 - mx)
            tot = e0 * ot_scr[0, hs, :] + e1 * ot_scr[1, hs, :] + e2 * ot_scr[2, hs, :]
            ot_scr[0, hs, :] = tot / (e0 + e1 + e2)
        oa_ref[...] = ot_scr[0].T.astype(BF16)


def attn_sample(proj, cos, sin, caches_t, oa_dst, layer):
    rb = M_PROMPT // DEC_BATCH
    def pblk(c0):
        return pl.BlockSpec((DEC_BATCH, A_WIDTH), lambda i: (rb, c0 // A_WIDTH))
    tab = pl.BlockSpec((1, A_GW), lambda i: (0, 0))
    cache_specs = [pl.BlockSpec((None, AS_BB, 2, A_HPG, A_HEAD_DIM, ct.shape[-1]),
                                lambda i: (layer, i, 0, 0, 0, 0)) for ct in caches_t]
    ng = len(A_GROUPS)
    full3 = pl.BlockSpec((ng, A_GW, DEC_BATCH), lambda i: (0, 0, 0))
    return pl.pallas_call(
        _attn_sample_kernel,
        grid=(DEC_BATCH // AS_BB,),
        in_specs=[pblk(C_AQ), pblk(C_AK), pblk(C_AV), tab, tab] + cache_specs
                 + [pl.BlockSpec(memory_space=pl.ANY)],
        out_specs=[pl.BlockSpec((DEC_BATCH, A_GW), lambda i: (rb, 0)), full3, full3],
        out_shape=[jax.ShapeDtypeStruct((M_ROWS, A_GW), BF16),
                   jax.ShapeDtypeStruct((ng, A_GW, DEC_BATCH), F32),
                   jax.ShapeDtypeStruct((ng, A_GW, DEC_BATCH), F32)],
        scratch_shapes=[pltpu.VMEM((ng, A_GW, DEC_BATCH), F32), pltpu.VMEM((ng, 8, DEC_BATCH), F32),
                        pltpu.VMEM((ng, A_GW, DEC_BATCH), F32), pltpu.VMEM((ng, 8, DEC_BATCH), F32)],
        compiler_params=_cparams(("arbitrary",)),
        input_output_aliases={8: 0},
        name="attn_sample",
    )(proj, proj, proj, cos, sin, *caches_t, oa_dst)


def _pad_heads(wt, d, dp):
    c = wt.shape[1]
    wt = wt.reshape(M_HEADS, d, c)
    return jnp.pad(wt, ((0, 0), (0, dp - d), (0, 0))).reshape(M_HEADS * dp, c)


def _layer_weights(w_in_l, w_up_m, w_up_c, w_up_a, w_o_l):
    w_in_t = jnp.transpose(w_in_l)
    o = IN_OFFSETS
    piece = lambda i: w_in_t[o[i]:o[i + 1]]
    mq, mk, mv, mi, mf, mo, cb, cc, ch, aq, ak, av, gt = [piece(i) for i in range(13)]
    w_br = jnp.concatenate(
        [_pad_heads(mv, M_DV, DVP), _pad_heads(mo, M_DV, DVP),
         _pad_heads(mq, M_DK, DKP), _pad_heads(mk, M_DK, DKP),
         cb, cc, ch, aq, ak, av,
         mi, mf, jnp.zeros((PROJ_W - C_IF - 2 * M_HEADS, D_MODEL), F32)], axis=0).astype(BF16)
    w_um = jnp.pad(w_up_m.reshape(M_HEADS, M_DV, D_MODEL),
                   ((0, 0), (0, DVP - M_DV), (0, 0))).reshape(M_HEADS * DVP, D_MODEL)
    return (w_br, gt.astype(BF16), w_um.astype(BF16), w_up_c.astype(BF16),
            w_up_a.astype(BF16), w_o_l.astype(BF16))


def _rope_tables(pos):
    half = A_HEAD_DIM // 2
    inv = ROPE_THETA ** (-(2.0 * jnp.arange(half, dtype=F32)) / A_HEAD_DIM)
    ang = pos.astype(F32)[:, None] * inv[None, :]
    cos = jnp.cos(ang)
    sin = jnp.sin(ang)
    cos = jnp.tile(jnp.concatenate([cos, cos], axis=-1), (1, A_HPG))
    sin = jnp.tile(jnp.concatenate([-sin, sin], axis=-1), (1, A_HPG))
    return cos, sin


def kernel(x_prompt, x_sample, state_mlstm_C, state_mlstm_n, state_mlstm_m, state_conv,
           cache_attn_kv_w128, cache_attn_kv_w512, cache_attn_kv_w2048,
           w_in, b_gate_if, mlstm_norm_g, conv_w, w_up_mlstm, w_up_conv, w_up_attn, w_o,
           w_ffn_in, w_ffn_out, ln_g, ln_b):
    x = jnp.concatenate([x_prompt.reshape(M_PROMPT, D_MODEL),
                         x_sample.reshape(DEC_BATCH, D_MODEL)], axis=0)
    cos_p, sin_p = _rope_tables(jnp.arange(SEQ))
    cos_s, sin_s = _rope_tables(PAST_LEN + jnp.arange(1))
    w_ffn_in_b = w_ffn_in.astype(BF16)
    w_ffn_out_b = w_ffn_out.astype(BF16)

    c0t = jnp.transpose(state_mlstm_C, (0, 2, 3, 4, 1))
    n0t = jnp.transpose(state_mlstm_n, (0, 2, 3, 1))
    m0t = jnp.transpose(state_mlstm_m, (0, 2, 1)).reshape(DEPTH, M_HEADS, 1, DEC_BATCH)
    caches_t = [jnp.transpose(c, (0, 1, 3, 4, 5, 2))
                for c in (cache_attn_kv_w128, cache_attn_kv_w512, cache_attn_kv_w2048)]

    keeps = [min(win, SEQ) for win, _ in A_GROUPS]
    kv_all = [pl.empty((DEPTH, BATCH, 2, A_GW, kp), F32) for kp in keeps]
    sct_all = pl.empty((DEPTH, M_HEADS, M_DK, M_DV, DEC_BATCH), F32)

    p_states, s_states = [], []
    for l in range(DEPTH):
        w_br, w_g, w_um, w_uc, w_ua, w_ol = _layer_weights(
            w_in[l], w_up_mlstm[l], w_up_conv[l], w_up_attn[l], w_o[l])
        bias = jnp.pad(b_gate_if[l], (0, LANES - 2 * M_HEADS)).reshape(1, LANES)
        bias_col = jnp.broadcast_to(b_gate_if[l][:, None], (2 * M_HEADS, DEC_BATCH))
        gain = jnp.pad(mlstm_norm_g[l].reshape(M_HEADS, M_DV), ((0, 0), (0, DVP - M_DV)))
        gain_col = mlstm_norm_g[l].reshape(M_HEADS, M_DV, 1)

        x, xb = ffn_ln(x, w_ffn_in_b, w_ffn_out_b, ln_g[l, 0], ln_b[l, 0], l, 0)
        proj = branch_proj(xb, w_br)

        hm, pc, pn, pm = mlstm_prompt(proj, bias, gain.reshape(M_HEADS, 1, DVP),
                                      pl.empty((M_ROWS, M_HEADS * DVP), BF16))
        hm, sct_all, snt, smt = mlstm_sample(proj, bias_col, gain_col, c0t, n0t, m0t, hm, sct_all, l)
        yc, pconv = conv_prompt(proj, conv_w[l], pl.empty((M_ROWS, CONV_WIDTH), BF16))
        yc, sconv = conv_sample(proj, state_conv[l], conv_w[l], yc)
        oa, *kv_all = attn_prompt(proj, cos_p[:, :LANES], sin_p[:, :LANES],
                                  pl.empty((M_ROWS, A_GW), BF16), kv_all, l)
        oa, kt_s, vt_s = attn_sample(proj, cos_s, sin_s, caches_t, oa, l)

        x = merge_ln(x, xb, hm, yc, oa, w_g, w_um, w_uc, w_ua, w_ol, ln_g[l, 1], ln_b[l, 1])
        x, _ = ffn_ln(x, w_ffn_in_b, w_ffn_out_b, ln_g[l, 2], ln_b[l, 2], l, 1)

        kt_s = kt_s.reshape(3, A_HPG, A_HEAD_DIM, DEC_BATCH)
        vt_s = vt_s.reshape(3, A_HPG, A_HEAD_DIM, DEC_BATCH)
        kv_s = [jnp.stack([kt_s[gi], vt_s[gi]], axis=0) for gi in range(3)]
        p_states.append((pc, pn.reshape(BATCH, M_HEADS, M_DK), pm.reshape(BATCH, M_HEADS), pconv))
        s_states.append((snt, smt, sconv, kv_s[0], kv_s[1], kv_s[2]))

    y_prompt = x[:M_PROMPT].reshape(BATCH, SEQ, D_MODEL)
    y_sample = x[M_PROMPT:].reshape(DEC_BATCH, 1, D_MODEL)
    p_out = [jnp.stack(z) for z in zip(*p_states)]
    p_out += [jnp.transpose(kvt.reshape(DEPTH, BATCH, 2, A_HPG, A_HEAD_DIM, kvt.shape[-1]),
                            (0, 1, 5, 2, 3, 4))
              for kvt in kv_all]
    snt, smt, sconv, kv0, kv1, kv2 = [jnp.stack(z) for z in zip(*s_states)]
    s_out = [jnp.transpose(sct_all, (0, 4, 1, 2, 3)),
             jnp.transpose(snt, (0, 3, 1, 2)),
             jnp.transpose(smt.reshape(DEPTH, M_HEADS, DEC_BATCH), (0, 2, 1)),
             sconv]
    s_out += [jnp.transpose(kv, (0, 4, 1, 2, 3)).reshape(DEPTH, DEC_BATCH, 1, 2, A_HPG, A_HEAD_DIM)
              for kv in (kv0, kv1, kv2)]
    return (y_prompt, y_sample, *p_out, *s_out)
```

```python
import functools
import math

import jax
import jax.numpy as jnp
import numpy as np
from jax import lax
from jax.experimental import pallas as pl
from jax.experimental.pallas import tpu as pltpu

F32 = jnp.float32
BF16 = jnp.bfloat16

D_MODEL = 2048
BATCH = 4
SEQ = 2048
DEPTH = 2
DEC_BATCH = 128
PAST_LEN = 2048
M_HEADS = 4
M_DV = 192
M_DK = 96
M_QK = M_HEADS * M_DK
M_WIDTH = M_HEADS * M_DV
CONV_WIDTH = 512
CONV_K = 3
A_GROUPS = ((128, 1), (512, 4), (2048, 16))
A_HPG = 4
A_HEAD_DIM = 64
A_GW = A_HPG * A_HEAD_DIM
A_WIDTH = 3 * A_GW
ROPE_THETA = 10000.0
N_BRANCH = 3
D_FF = 5632
LN_EPS = 1e-5
ALPHA = (2 * DEPTH) ** 0.25
IN_SIZES = (M_QK, M_QK, M_WIDTH, M_HEADS, M_HEADS, M_WIDTH,
            CONV_WIDTH, CONV_WIDTH, CONV_WIDTH,
            A_WIDTH, A_WIDTH, A_WIDTH, N_BRANCH * D_MODEL)
IN_OFFSETS = tuple(int(o) for o in np.cumsum((0,) + IN_SIZES))

M_PROMPT = BATCH * SEQ
M_ROWS = M_PROMPT + DEC_BATCH

LANES = 128
DKP = 128
DVP = 256
VMEM_LIMIT = 52 * 1024 * 1024
VMEM_LIMIT_FFN = 56 * 1024 * 1024

C_MV, C_MO = 0, 1024
C_MQ, C_MK = 2048, 2560
C_CB, C_CC, C_CH = 3072, 3584, 4096
C_AQ, C_AK, C_AV = 4608, 5376, 6144
C_IF = 6912
PROJ_W = 7168

NEG = -1e30


def _sigmoid(x):
    return 1.0 / (1.0 + jnp.exp(-x))


def _layer_norm(z, g, b):
    mu = jnp.mean(z, axis=-1, keepdims=True)
    zc = z - mu
    var = jnp.mean(zc * zc, axis=-1, keepdims=True)
    return zc * lax.rsqrt(var + LN_EPS) * g + b


def _cparams(sem, vmem_limit=VMEM_LIMIT):
    return pltpu.CompilerParams(dimension_semantics=sem, vmem_limit_bytes=vmem_limit)


FFN_TM = 832
FFN_TF = 512


def _ffn_kernel(x_ref, wa_ref, wb_ref, wo_ref, g_ref, b_ref, y_ref, xb_scr):
    f = pl.program_id(1)

    @pl.when(f == 0)
    def _():
        xb_scr[...] = x_ref[...].astype(BF16)
        y_ref[...] = jnp.zeros_like(y_ref)

    xb = xb_scr[...]
    a = jnp.dot(xb, wa_ref[...], preferred_element_type=F32)
    b = jnp.dot(xb, wb_ref[...], preferred_element_type=F32)
    h = (a * _sigmoid(a)) * b
    y_ref[...] += jnp.dot(h.astype(BF16), wo_ref[...], preferred_element_type=F32)

    @pl.when(f == pl.num_programs(1) - 1)
    def _():
        z = ALPHA * x_ref[...] + 0.5 * y_ref[...]
        y_ref[...] = _layer_norm(z, g_ref[...], b_ref[...])


def ffn_ln(x, w_in, w_out, g, b, layer, which):
    m = x.shape[0]
    nf = D_FF // FFN_TF
    return pl.pallas_call(
        _ffn_kernel,
        grid=(m // FFN_TM, nf),
        in_specs=[
            pl.BlockSpec((FFN_TM, D_MODEL), lambda i, f: (i, 0)),
            pl.BlockSpec((None, None, D_MODEL, FFN_TF), lambda i, f: (layer, which, 0, f)),
            pl.BlockSpec((None, None, D_MODEL, FFN_TF), lambda i, f: (layer, which, 0, f + nf)),
            pl.BlockSpec((None, None, FFN_TF, D_MODEL), lambda i, f: (layer, which, f, 0)),
            pl.BlockSpec((1, D_MODEL), lambda i, f: (0, 0)),
            pl.BlockSpec((1, D_MODEL), lambda i, f: (0, 0)),
        ],
        out_specs=pl.BlockSpec((FFN_TM, D_MODEL), lambda i, f: (i, 0)),
        out_shape=jax.ShapeDtypeStruct((m, D_MODEL), F32),
        scratch_shapes=[pltpu.VMEM((FFN_TM, D_MODEL), BF16)],
        compiler_params=_cparams(("parallel", "arbitrary"), VMEM_LIMIT_FFN),
        name="ffn_ln",
    )(x, w_in, w_in, w_out, g.reshape(1, D_MODEL), b.reshape(1, D_MODEL))


PROJ_TM = 1040
PROJ_TN = 1024


NT_DIMS = (((1,), (1,)), ((), ()))


def _proj_kernel(x_ref, wt_ref, o_ref, xb_scr):
    @pl.when(pl.program_id(1) == 0)
    def _():
        xb_scr[...] = x_ref[...].astype(BF16)

    o_ref[...] = lax.dot_general(xb_scr[...], wt_ref[...], NT_DIMS, preferred_element_type=F32)


def branch_proj(x, wt):
    m = x.shape[0]
    return pl.pallas_call(
        _proj_kernel,
        grid=(m // PROJ_TM, PROJ_W // PROJ_TN),
        in_specs=[pl.BlockSpec((PROJ_TM, D_MODEL), lambda i, j: (i, 0)),
                  pl.BlockSpec((PROJ_TN, D_MODEL), lambda i, j: (j, 0))],
        out_specs=pl.BlockSpec((PROJ_TM, PROJ_TN), lambda i, j: (i, j)),
        out_shape=jax.ShapeDtypeStruct((m, PROJ_W), F32),
        scratch_shapes=[pltpu.VMEM((PROJ_TM, D_MODEL), BF16)],
        compiler_params=_cparams(("parallel", "arbitrary")),
        name="branch_proj",
    )(x, wt)


MRG_TM = 640
MRG_TN = 512


def _gate_up_kernel(x_ref, hm_ref, yc_ref, oa_ref, wg0_ref, wg1_ref, wg2_ref,
                    wum_ref, wuc_ref, wua_ref, o_ref, xb_scr):
    @pl.when(pl.program_id(1) == 0)
    def _():
        xb_scr[...] = x_ref[...].astype(BF16)

    xb = xb_scr[...]

    def gated(wg_ref, br_ref, wu_ref):
        gate = _sigmoid(lax.dot_general(xb, wg_ref[...], NT_DIMS, preferred_element_type=F32))
        return gate * jnp.dot(br_ref[...], wu_ref[...], preferred_element_type=F32)

    merged = (gated(wg0_ref, hm_ref, wum_ref) + gated(wg1_ref, yc_ref, wuc_ref)
              + gated(wg2_ref, oa_ref, wua_ref))
    o_ref[...] = merged.astype(BF16)


def _out_ln_kernel(x_ref, mg_ref, wo_ref, g_ref, b_ref, y_ref):
    z = ALPHA * x_ref[...] + jnp.dot(mg_ref[...], wo_ref[...], preferred_element_type=F32)
    y_ref[...] = _layer_norm(z, g_ref[...], b_ref[...])


def merge_ln(x, hm, yc, oa, w_all, wum, wuc, wua, wo, g, b):
    m = x.shape[0]
    nn = D_MODEL // MRG_TN
    g0 = PROJ_W // MRG_TN
    row = lambda w: pl.BlockSpec((MRG_TM, w), lambda i, n: (i, 0))
    merged = pl.pallas_call(
        _gate_up_kernel,
        grid=(m // MRG_TM, nn),
        in_specs=[
            row(D_MODEL), row(M_HEADS * DVP), row(CONV_WIDTH), row(A_GW),
            pl.BlockSpec((MRG_TN, D_MODEL), lambda i, n: (g0 + n, 0)),
            pl.BlockSpec((MRG_TN, D_MODEL), lambda i, n: (g0 + n + nn, 0)),
            pl.BlockSpec((MRG_TN, D_MODEL), lambda i, n: (g0 + n + 2 * nn, 0)),
            pl.BlockSpec((M_HEADS * DVP, MRG_TN), lambda i, n: (0, n)),
            pl.BlockSpec((CONV_WIDTH, MRG_TN), lambda i, n: (0, n)),
            pl.BlockSpec((A_GW, MRG_TN), lambda i, n: (0, n)),
        ],
        out_specs=pl.BlockSpec((MRG_TM, MRG_TN), lambda i, n: (i, n)),
        out_shape=jax.ShapeDtypeStruct((m, D_MODEL), BF16),
        scratch_shapes=[pltpu.VMEM((MRG_TM, D_MODEL), BF16)],
        compiler_params=_cparams(("parallel", "arbitrary")),
        name="gate_up",
    )(x, hm, yc, oa, w_all, w_all, w_all, wum, wuc, wua)
    rows = pl.BlockSpec((MRG_TM, D_MODEL), lambda i: (i, 0))
    vec = pl.BlockSpec((1, D_MODEL), lambda i: (0, 0))
    return pl.pallas_call(
        _out_ln_kernel,
        grid=(m // MRG_TM,),
        in_specs=[rows, rows, pl.BlockSpec((D_MODEL, D_MODEL), lambda i: (0, 0)), vec, vec],
        out_specs=rows,
        out_shape=jax.ShapeDtypeStruct((m, D_MODEL), F32),
        compiler_params=_cparams(("parallel",)),
        name="out_ln",
    )(x, merged, wo, g.reshape(1, D_MODEL), b.reshape(1, D_MODEL))


def _conv_prompt_kernel(cb_ref, cc_ref, ch_ref, w_ref, y_dst_ref, y_ref, st_ref, u_scr):
    del y_dst_ref
    u = cc_ref[...] * ch_ref[...]
    u_scr[pl.ds(0, 8), :] = jnp.zeros((8, CONV_WIDTH), F32)
    u_scr[pl.ds(8, SEQ), :] = u
    w = w_ref[...]
    acc = (w[0:1, :] * u_scr[pl.ds(6, SEQ), :] + w[1:2, :] * u_scr[pl.ds(7, SEQ), :]
           + w[2:3, :] * u)
    y_ref[...] = (cb_ref[...] * acc).astype(BF16)
    st_ref[...] = u_scr[pl.ds(8 + SEQ - (CONV_K - 1), CONV_K - 1), :]


def conv_prompt(proj, conv_w, y_dst):
    blk = lambda c: pl.BlockSpec((SEQ, CONV_WIDTH), lambda b, c=c: (b, c // CONV_WIDTH))
    return pl.pallas_call(
        _conv_prompt_kernel,
        grid=(BATCH,),
        in_specs=[blk(C_CB), blk(C_CC), blk(C_CH),
                  pl.BlockSpec((CONV_K, CONV_WIDTH), lambda b: (0, 0)),
                  pl.BlockSpec(memory_space=pl.ANY)],
        out_specs=[pl.BlockSpec((SEQ, CONV_WIDTH), lambda b: (b, 0)),
                   pl.BlockSpec((None, CONV_K - 1, CONV_WIDTH), lambda b: (b, 0, 0))],
        out_shape=[jax.ShapeDtypeStruct((M_ROWS, CONV_WIDTH), BF16),
                   jax.ShapeDtypeStruct((BATCH, CONV_K - 1, CONV_WIDTH), F32)],
        input_output_aliases={4: 0},
        scratch_shapes=[pltpu.VMEM((SEQ + 8, CONV_WIDTH), F32)],
        compiler_params=_cparams(("parallel",)),
        name="conv_prompt",
    )(proj, proj, proj, conv_w, y_dst)


def _conv_sample_kernel(cb_ref, cc_ref, ch_ref, prev_ref, w_ref, y_dst_ref, y_ref, st_ref):
    del y_dst_ref
    u = cc_ref[...] * ch_ref[...]
    w = w_ref[...]
    p0 = prev_ref[:, 0, :]
    p1 = prev_ref[:, 1, :]
    acc = w[0:1, :] * p0 + w[1:2, :] * p1 + w[2:3, :] * u
    y_ref[...] = (cb_ref[...] * acc).astype(BF16)
    st_ref[:, 0, :] = p1
    st_ref[:, 1, :] = u


def conv_sample(proj, prev, conv_w, y_dst):
    rb = M_PROMPT // DEC_BATCH
    blk = lambda c: pl.BlockSpec((DEC_BATCH, CONV_WIDTH), lambda i, c=c: (rb, c // CONV_WIDTH))
    full3 = pl.BlockSpec((DEC_BATCH, CONV_K - 1, CONV_WIDTH), lambda i: (0, 0, 0))
    return pl.pallas_call(
        _conv_sample_kernel,
        grid=(1,),
        in_specs=[blk(C_CB), blk(C_CC), blk(C_CH), full3,
                  pl.BlockSpec((CONV_K, CONV_WIDTH), lambda i: (0, 0)),
                  pl.BlockSpec(memory_space=pl.ANY)],
        out_specs=[pl.BlockSpec((DEC_BATCH, CONV_WIDTH), lambda i: (rb, 0)), full3],
        out_shape=[jax.ShapeDtypeStruct((M_ROWS, CONV_WIDTH), BF16),
                   jax.ShapeDtypeStruct((DEC_BATCH, CONV_K - 1, CONV_WIDTH), F32)],
        input_output_aliases={5: 0},
        compiler_params=_cparams(("arbitrary",)),
        name="conv_sample",
    )(proj, proj, proj, prev, conv_w, y_dst)


M_L = 128


def _log_sigmoid(x):
    return jnp.minimum(x, 0.0) - jnp.log1p(jnp.exp(-jnp.abs(x)))


def _head_norm_gate(h, o_pre, gain):
    lane = lax.broadcasted_iota(jnp.int32, h.shape, 1)
    real = lane < M_DV
    mu = jnp.sum(h, axis=-1, keepdims=True) * (1.0 / M_DV)
    hc = jnp.where(real, h - mu, 0.0)
    var = jnp.sum(hc * hc, axis=-1, keepdims=True) * (1.0 / M_DV)
    return _sigmoid(o_pre) * (hc * lax.rsqrt(var + LN_EPS) * gain)


M_TS = 512


M_NROW = M_DV


def _mlstm_prompt_kernel(q_ref, k_ref, v_ref, o_ref, if_ref, bias_ref, gain_ref, hm_dst_ref,
                         hm_ref, c_out_ref, n_out_ref, m_out_ref, ct_scr, m_scr):
    del hm_dst_ref
    step = pl.program_id(1)

    @pl.when(step == 0)
    def _():
        ct_scr[...] = jnp.zeros_like(ct_scr)
        m_scr[...] = jnp.zeros_like(m_scr)

    row = lax.broadcasted_iota(jnp.int32, (M_L, M_L), 0)
    col = lax.broadcasted_iota(jnp.int32, (M_L, M_L), 1)
    causal_t = row <= col
    tri = (col <= row).astype(F32)
    bias = bias_ref[...]
    ones_lane = lax.broadcasted_iota(jnp.int32, (M_L, DVP), 1) == M_NROW
    real_rows = lax.broadcasted_iota(jnp.int32, (DVP, M_L), 0) < M_DV
    tn = (((0,), (0,)), ((), ()))

    def chunk(c, carry):
        r0 = pl.multiple_of(c * M_L, M_L)
        x_if = if_ref[pl.ds(r0, M_L), :] + bias
        log_f = _log_sigmoid(x_if)
        cs = jnp.dot(tri, log_f, preferred_element_type=F32, precision=lax.Precision.HIGHEST)
        zt = jnp.where(col < M_HEADS, x_if, cs).T
        for hd in range(M_HEADS):
            b_row = zt[M_HEADS + hd:M_HEADS + hd + 1, :]
            c_col = x_if[:, hd:hd + 1] - cs[:, M_HEADS + hd:M_HEADS + hd + 1]
            c_rep = jnp.broadcast_to(c_col, (M_L, M_L))
            m_prev = m_scr[hd]
            b_last = b_row[:, M_L - 1:M_L]

            d_t = jnp.where(causal_t, b_row + c_rep, NEG)
            inter = b_row + m_prev
            m_t = jnp.maximum(jnp.max(d_t, axis=0, keepdims=True), inter)
            q = q_ref[pl.ds(r0, M_L), pl.ds(hd * DKP, DKP)].astype(BF16)
            k = k_ref[pl.ds(r0, M_L), pl.ds(hd * DKP, DKP)] * (M_DK ** -0.5)
            v1 = jnp.where(ones_lane, 1.0, v_ref[pl.ds(r0, M_L), pl.ds(hd * DVP, DVP)]).astype(BF16)
            s_t = lax.dot_general(k.astype(BF16), q, NT_DIMS, preferred_element_type=F32) * jnp.exp(d_t - m_t)
            w_inter = jnp.exp(inter - m_t)
            ct_prev = ct_scr[hd]
            num_t = (lax.dot_general(v1, s_t.astype(BF16), tn, preferred_element_type=F32)
                     + w_inter * lax.dot_general(ct_prev.astype(BF16), q, NT_DIMS,
                                                 preferred_element_type=F32))
            den = num_t[M_NROW:M_NROW + 1, :]
            h_t = jnp.where(real_rows, num_t / jnp.maximum(jnp.abs(den), jnp.exp(-m_t)), 0.0)
            mu = jnp.sum(h_t, axis=0, keepdims=True) * (1.0 / M_DV)
            hc = jnp.where(real_rows, h_t - mu, 0.0)
            var = jnp.sum(hc * hc, axis=0, keepdims=True) * (1.0 / M_DV)
            hn = (hc * lax.rsqrt(var + LN_EPS)).T * gain_ref[hd]
            o_pre = o_ref[pl.ds(r0, M_L), pl.ds(hd * DVP, DVP)]
            hm_ref[pl.ds(r0, M_L), pl.ds(hd * DVP, DVP)] = (_sigmoid(o_pre) * hn).astype(BF16)

            m_new = jnp.maximum(b_last + m_prev, b_last + jnp.max(c_rep, axis=0, keepdims=True)[:, 0:1])
            kw = k * jnp.exp(c_rep + (b_last - m_new))
            ct_scr[hd] = (jnp.exp(b_last + m_prev - m_new) * ct_prev
                          + lax.dot_general(v1, kw.astype(BF16), tn, preferred_element_type=F32))
            m_scr[hd] = m_new
        return carry

    lax.fori_loop(0, M_TS // M_L, chunk, 0)

    @pl.when(step == pl.num_programs(1) - 1)
    def _():
        for hd in range(M_HEADS):
            ct = ct_scr[hd]
            c_out_ref[hd] = ct.T[0:M_DK, 0:M_DV]
            n_out_ref[hd] = ct[M_NROW:M_NROW + 1, 0:M_DK]
            m_out_ref[hd] = m_scr[hd]


def mlstm_prompt(proj, bias, gain, hm_dst):
    ns = SEQ // M_TS
    def cblk(c0, w):
        return pl.BlockSpec((M_TS, w), lambda b, s: (b * ns + s, c0 // w))
    return pl.pallas_call(
        _mlstm_prompt_kernel,
        grid=(BATCH, ns),
        in_specs=[cblk(C_MQ, M_HEADS * DKP), cblk(C_MK, M_HEADS * DKP),
                  cblk(C_MV, M_HEADS * DVP), cblk(C_MO, M_HEADS * DVP),
                  cblk(C_IF, LANES),
                  pl.BlockSpec((1, LANES), lambda b, s: (0, 0)),
                  pl.BlockSpec((M_HEADS, 1, DVP), lambda b, s: (0, 0, 0)),
                  pl.BlockSpec(memory_space=pl.ANY)],
        out_specs=[pl.BlockSpec((M_TS, M_HEADS * DVP), lambda b, s: (b * ns + s, 0)),
                   pl.BlockSpec((None, M_HEADS, M_DK, M_DV), lambda b, s: (b, 0, 0, 0)),
                   pl.BlockSpec((None, M_HEADS, 1, M_DK), lambda b, s: (b, 0, 0, 0)),
                   pl.BlockSpec((None, M_HEADS, 1, 1), lambda b, s: (b, 0, 0, 0))],
        out_shape=[jax.ShapeDtypeStruct((M_ROWS, M_HEADS * DVP), BF16),
                   jax.ShapeDtypeStruct((BATCH, M_HEADS, M_DK, M_DV), F32),
                   jax.ShapeDtypeStruct((BATCH, M_HEADS, 1, M_DK), F32),
                   jax.ShapeDtypeStruct((BATCH, M_HEADS, 1, 1), F32)],
        scratch_shapes=[pltpu.VMEM((M_HEADS, DVP, DKP), F32), pltpu.VMEM((M_HEADS, 1, 1), F32)],
        compiler_params=_cparams(("parallel", "arbitrary")),
        input_output_aliases={7: 0},
        name="mlstm_prompt",
    )(proj, proj, proj, proj, proj, bias, gain, hm_dst)


MS_DC = 48


def _pick_row(x8, j):
    rows = lax.broadcasted_iota(jnp.int32, x8.shape, 0)
    return jnp.sum(jnp.where(rows == j, x8, 0.0), axis=0, keepdims=True)


def _mlstm_sample_kernel(q_ref, k_ref, v_ref, o_ref, if_ref, bias_ref, gain_ref,
                         c0_ref, n0_ref, m0_ref, hm_dst_ref, c_dst_ref,
                         hm_ref, c_out_ref, n_out_ref, m_out_ref,
                         qt_scr, kw_scr, vt_scr, acc_scr, st_scr):
    del hm_dst_ref, c_dst_ref
    hd = pl.program_id(0)
    c = pl.program_id(1)

    @pl.when(c == 0)
    def _():
        qt = q_ref[...].T
        kt = (k_ref[...] * (M_DK ** -0.5)).T
        vt_scr[...] = v_ref[...].T
        x_if = if_ref[...].T[0:2 * M_HEADS, :] + bias_ref[...]
        i_pre = _pick_row(x_if, hd)
        log_f = _log_sigmoid(_pick_row(x_if, hd + M_HEADS))
        inter = log_f + m0_ref[...]
        m_new = jnp.maximum(i_pre, inter)
        w_k = jnp.exp(i_pre - m_new)
        decay = jnp.exp(inter - m_new)
        n_prev = n0_ref[...]
        s = jnp.sum(qt * kt, axis=0, keepdims=True) * w_k
        den = s + decay * jnp.sum(qt[:M_DK] * n_prev, axis=0, keepdims=True)
        kw = kt * w_k
        qt_scr[...] = qt
        kw_scr[...] = kw
        n_out_ref[...] = decay * n_prev + kw[:M_DK]
        m_out_ref[...] = m_new
        st_scr[0:1, :] = s
        st_scr[1:2, :] = decay
        st_scr[2:3, :] = den
        st_scr[3:4, :] = m_new
        acc_scr[...] = jnp.zeros_like(acc_scr)

    decay = st_scr[1:2, :]
    vt = vt_scr[pl.ds(0, M_DV), :]

    def tile(t, acc):
        r8 = pl.multiple_of(c * MS_DC + t * 8, 8)
        q8 = qt_scr[pl.ds(r8, 8), :]
        kw8 = kw_scr[pl.ds(r8, 8), :]
        for r in range(8):
            c_row = c0_ref[t * 8 + r]
            c_out_ref[t * 8 + r] = decay * c_row + kw8[r:r + 1, :] * vt
            acc = acc + q8[r:r + 1, :] * c_row
        return acc

    acc = lax.fori_loop(0, MS_DC // 8, tile, acc_scr[...])
    acc_scr[...] = acc

    @pl.when(c == pl.num_programs(1) - 1)
    def _():
        s = st_scr[0:1, :]
        den = st_scr[2:3, :]
        m_t = st_scr[3:4, :]
        h = (s * vt + decay * acc) / jnp.maximum(jnp.abs(den), jnp.exp(-m_t))
        mu = jnp.mean(h, axis=0, keepdims=True)
        hc = h - mu
        var = jnp.mean(hc * hc, axis=0, keepdims=True)
        o_pre = o_ref[...].T[:M_DV, :]
        out = _sigmoid(o_pre) * (hc * lax.rsqrt(var + LN_EPS) * gain_ref[...])
        out = jnp.concatenate([out, jnp.zeros((DVP - M_DV, DEC_BATCH), F32)], axis=0)
        hm_ref[...] = out.T.astype(BF16)


def mlstm_sample(proj, bias_col, gain_col, c0t, n0t, m0t, hm_dst, c_dst, layer):
    any_spec = pl.BlockSpec(memory_space=pl.ANY)
    rb = M_PROMPT // DEC_BATCH
    nc = M_DK // MS_DC
    def cblk(c0_, w):
        return pl.BlockSpec((DEC_BATCH, w), lambda h, c: (rb, c0_ // w + h))
    return pl.pallas_call(
        _mlstm_sample_kernel,
        grid=(M_HEADS, nc),
        in_specs=[cblk(C_MQ, DKP), cblk(C_MK, DKP), cblk(C_MV, DVP), cblk(C_MO, DVP),
                  pl.BlockSpec((DEC_BATCH, LANES), lambda h, c: (rb, C_IF // LANES)),
                  pl.BlockSpec((2 * M_HEADS, DEC_BATCH), lambda h, c: (0, 0)),
                  pl.BlockSpec((None, M_DV, 1), lambda h, c: (h, 0, 0)),
                  pl.BlockSpec((None, None, MS_DC, M_DV, DEC_BATCH), lambda h, c: (layer, h, c, 0, 0)),
                  pl.BlockSpec((None, None, M_DK, DEC_BATCH), lambda h, c: (layer, h, 0, 0)),
                  pl.BlockSpec((None, None, 1, DEC_BATCH), lambda h, c: (layer, h, 0, 0)),
                  any_spec, any_spec],
        out_specs=[pl.BlockSpec((DEC_BATCH, DVP), lambda h, c: (rb, h)),
                   pl.BlockSpec((None, None, MS_DC, M_DV, DEC_BATCH), lambda h, c: (layer, h, c, 0, 0)),
                   pl.BlockSpec((None, M_DK, DEC_BATCH), lambda h, c: (h, 0, 0)),
                   pl.BlockSpec((None, 1, DEC_BATCH), lambda h, c: (h, 0, 0))],
        out_shape=[jax.ShapeDtypeStruct((M_ROWS, M_HEADS * DVP), BF16),
                   jax.ShapeDtypeStruct((DEPTH, M_HEADS, M_DK, M_DV, DEC_BATCH), F32),
                   jax.ShapeDtypeStruct((M_HEADS, M_DK, DEC_BATCH), F32),
                   jax.ShapeDtypeStruct((M_HEADS, 1, DEC_BATCH), F32)],
        scratch_shapes=[pltpu.VMEM((DKP, DEC_BATCH), F32), pltpu.VMEM((DKP, DEC_BATCH), F32),
                        pltpu.VMEM((DVP, DEC_BATCH), F32), pltpu.VMEM((M_DV, DEC_BATCH), F32),
                        pltpu.VMEM((8, DEC_BATCH), F32)],
        input_output_aliases={10: 0, 11: 1},
        compiler_params=_cparams(("arbitrary", "arbitrary")),
        name="mlstm_sample",
    )(proj, proj, proj, proj, proj, bias_col, gain_col, c0t, n0t, m0t, hm_dst, c_dst)


A_Q = 128
A_LT = A_GW // LANES


def _rope(x, cos, sin_signed):
    lane = lax.broadcasted_iota(jnp.int32, x.shape, 1)
    first_half = (lane % A_HEAD_DIM) < (A_HEAD_DIM // 2)
    partner = jnp.where(first_half, pltpu.roll(x, x.shape[1] - A_HEAD_DIM // 2, 1),
                        pltpu.roll(x, A_HEAD_DIM // 2, 1))
    return x * cos + partner * sin_signed


def _head_masks(shape):
    lane = lax.broadcasted_iota(jnp.int32, shape, 1)
    return [(lane // A_HEAD_DIM) == h for h in range(A_HPG)]


def _attn_group_prompt(dil, gi, qs_scr, ks_scr, vs_scr, o_scr, l_scr):
    length = SEQ // dil
    nb = length // A_Q
    row = lax.broadcasted_iota(jnp.int32, (A_Q, A_Q), 0)
    col = lax.broadcasted_iota(jnp.int32, (A_Q, A_Q), 1)
    cur_ok = col <= row
    prev_ok = col >= row
    masks = _head_masks((A_Q, A_GW))
    nt = (((1,), (1,)), ((), ()))

    def window(start):
        if dil == 1:
            return pl.ds(pl.multiple_of(start, A_Q), A_Q)
        return pl.ds(start, A_Q, stride=dil)

    def rows(scr, start):
        w = window(start)
        return jnp.concatenate([scr[t, w, :] for t in range(A_LT)], axis=1).astype(BF16)

    def block(idx, carry):
        r = idx % dil
        n = idx // dil
        base = r + (dil * A_Q) * n
        qb = rows(qs_scr, base)
        kc = rows(ks_scr, base)
        vc = rows(vs_scr, base)
        if nb > 1:
            pbase = jnp.maximum(base - dil * A_Q, r)
            kp = rows(ks_scr, pbase)
            vp = rows(vs_scr, pbase)
            has_prev = n > 0
        o_acc = jnp.zeros((A_Q, A_GW), F32)
        l_acc = jnp.zeros((A_Q, A_GW), F32)
        for h in range(A_HPG):
            qh = jnp.where(masks[h], qb, jnp.zeros_like(qb))
            s_c = jnp.where(cur_ok, lax.dot_general(qh, kc, nt, preferred_element_type=F32), NEG)
            m = jnp.max(s_c, axis=1, keepdims=True)
            if nb > 1:
                s_p = jnp.where(jnp.logical_and(prev_ok, has_prev),
                                lax.dot_general(qh, kp, nt, preferred_element_type=F32), NEG)
                m = jnp.maximum(m, jnp.max(s_p, axis=1, keepdims=True))
            p_c = jnp.exp(s_c - m)
            l = jnp.sum(p_c, axis=1, keepdims=True)
            o_h = jnp.dot(p_c.astype(BF16), vc, preferred_element_type=F32)
            if nb > 1:
                p_p = jnp.exp(s_p - m)
                l = l + jnp.sum(p_p, axis=1, keepdims=True)
                o_h = o_h + jnp.dot(p_p.astype(BF16), vp, preferred_element_type=F32)
            o_acc = o_acc + jnp.where(masks[h], o_h / l, 0.0)
            l_acc = l_acc + jnp.where(masks[h], m + jnp.log(l), 0.0)
        w = window(base)
        for t in range(A_LT):
            o_scr[gi, t, w, :] = o_acc[:, t * LANES:(t + 1) * LANES]
            l_scr[gi, t, w, :] = l_acc[:, t * LANES:(t + 1) * LANES]
        return carry

    lax.fori_loop(0, dil * nb, block, 0)


def _attn_prompt_kernel(q_ref, k_ref, v_ref, cos_ref, sin_ref, oa_dst_ref, kvd0_ref, kvd1_ref, kvd2_ref,
                        oa_ref, kv0_ref, kv1_ref, kv2_ref, qs_scr, ks_scr, vs_scr, o_scr, l_scr):
    del oa_dst_ref, kvd0_ref, kvd1_ref, kvd2_ref
    g = pl.program_id(1)
    rc = 256

    def rope_rows(c, carry):
        sl = pl.ds(pl.multiple_of(c * rc, rc), rc)
        cos = cos_ref[sl, :]
        sin = sin_ref[sl, :]
        for t in range(A_LT):
            lanes = pl.ds(t * LANES, LANES)
            qs_scr[t, sl, :] = _rope(q_ref[sl, lanes], cos, sin) * (A_HEAD_DIM ** -0.5)
            ks_scr[t, sl, :] = _rope(k_ref[sl, lanes], cos, sin)
            vs_scr[t, sl, :] = v_ref[sl, lanes]
        return carry

    lax.fori_loop(0, SEQ // rc, rope_rows, 0)

    for gi, ((win, dil), kv_ref) in enumerate(zip(A_GROUPS, (kv0_ref, kv1_ref, kv2_ref))):
        @pl.when(g == gi)
        def _(gi=gi, dil=dil, win=win, kv_ref=kv_ref):
            _attn_group_prompt(dil, gi, qs_scr, ks_scr, vs_scr, o_scr, l_scr)
            keep = min(win, SEQ)
            for c in range(keep // LANES):
                rows = pl.ds(SEQ - keep + c * LANES, LANES)
                for t in range(A_LT):
                    kv_ref[0, pl.ds(t * LANES, LANES), pl.ds(c * LANES, LANES)] = ks_scr[t, rows, :].T
                    kv_ref[1, pl.ds(t * LANES, LANES), pl.ds(c * LANES, LANES)] = vs_scr[t, rows, :].T

    @pl.when(g == len(A_GROUPS) - 1)
    def _():
        def comb(c, carry):
            sl = pl.ds(pl.multiple_of(c * rc, rc), rc)
            for t in range(A_LT):
                l0, l1, l2 = l_scr[0, t, sl, :], l_scr[1, t, sl, :], l_scr[2, t, sl, :]
                mx = jnp.maximum(jnp.maximum(l0, l1), l2)
                e0, e1, e2 = jnp.exp(l0 - mx), jnp.exp(l1 - mx), jnp.exp(l2 - mx)
                tot = e0 * o_scr[0, t, sl, :] + e1 * o_scr[1, t, sl, :] + e2 * o_scr[2, t, sl, :]
                oa_ref[sl, pl.ds(t * LANES, LANES)] = (tot / (e0 + e1 + e2)).astype(BF16)
            return carry
        lax.fori_loop(0, SEQ // rc, comb, 0)


def attn_prompt(proj, cos, sin, oa_dst, kv_dst, layer):
    def gblk(c0):
        return pl.BlockSpec((SEQ, A_GW), lambda b, g: (b, c0 // A_GW + g))
    tab = pl.BlockSpec((SEQ, LANES), lambda b, g: (0, 0))
    keeps = [min(win, SEQ) for win, _ in A_GROUPS]
    return pl.pallas_call(
        _attn_prompt_kernel,
        grid=(BATCH, len(A_GROUPS)),
        in_specs=[gblk(C_AQ), gblk(C_AK), gblk(C_AV), tab, tab] + [pl.BlockSpec(memory_space=pl.ANY)] * 4,
        out_specs=[pl.BlockSpec((SEQ, A_GW), lambda b, g: (b, 0))]
                  + [pl.BlockSpec((None, None, 2, A_GW, kp), lambda b, g: (layer, b, 0, 0, 0)) for kp in keeps],
        out_shape=[jax.ShapeDtypeStruct((M_ROWS, A_GW), BF16)]
                  + [jax.ShapeDtypeStruct((DEPTH, BATCH, 2, A_GW, kp), F32) for kp in keeps],
        input_output_aliases={5: 0, 6: 1, 7: 2, 8: 3},
        scratch_shapes=[pltpu.VMEM((A_LT, SEQ, LANES), F32),
                        pltpu.VMEM((A_LT, SEQ, LANES), F32),
                        pltpu.VMEM((A_LT, SEQ, LANES), F32),
                        pltpu.VMEM((len(A_GROUPS), A_LT, SEQ, LANES), F32),
                        pltpu.VMEM((len(A_GROUPS), A_LT, SEQ, LANES), F32)],
        compiler_params=_cparams(("parallel", "arbitrary")),
        name="attn_prompt",
    )(proj, proj, proj, cos, sin, oa_dst, *kv_dst)


AS_BB = 2


def _attn_sample_kernel(q_ref, k_ref, v_ref, cos_ref, sin_ref, c0_ref, c1_ref, c2_ref, oa_dst_ref,
                        oa_ref, kt_ref, vt_ref, qt_scr, s0_scr, ot_scr, lt_scr):
    del oa_dst_ref
    i = pl.program_id(0)
    lane_b = lax.broadcasted_iota(jnp.int32, (1, DEC_BATCH), 1)
    sub8 = lax.broadcasted_iota(jnp.int32, (8, DEC_BATCH), 0)

    @pl.when(i == 0)
    def _():
        cos = cos_ref[...]
        sin = sin_ref[...]
        for gi in range(len(A_GROUPS)):
            gs = pl.ds(gi * A_GW, A_GW)
            qt = (_rope(q_ref[:, gs], cos, sin) * (A_HEAD_DIM ** -0.5)).T
            kt = _rope(k_ref[:, gs], cos, sin).T
            qt_scr[gi] = qt
            kt_ref[gi] = kt
            vt_ref[gi] = v_ref[:, gs].T
            prod = qt * kt
            s0 = jnp.zeros((8, DEC_BATCH), F32)
            for h in range(A_HPG):
                part = jnp.sum(prod[h * A_HEAD_DIM:(h + 1) * A_HEAD_DIM], axis=0, keepdims=True)
                s0 = jnp.where(sub8 == h, part, s0)
            s0_scr[gi] = s0
        ot_scr[...] = jnp.zeros_like(ot_scr)
        lt_scr[...] = jnp.zeros_like(lt_scr)

    for bl in range(AS_BB):
        pick = lane_b == i * AS_BB + bl
        for gi, (cache_ref, (_, dil)) in enumerate(zip((c0_ref, c1_ref, c2_ref), A_GROUPS)):
            wb = cache_ref.shape[-1]
            if dil > 1:
                keep = (lax.broadcasted_iota(jnp.int32, (1, wb), 1) & (dil - 1)) == 0
            head_row = lax.broadcasted_iota(jnp.int32, (8, wb), 0)
            q_col = jnp.sum(jnp.where(pick, qt_scr[gi], 0.0), axis=1, keepdims=True)
            v_col = jnp.sum(jnp.where(pick, vt_ref[gi], 0.0), axis=1, keepdims=True)
            s0 = jnp.sum(jnp.where(pick, s0_scr[gi], 0.0), axis=1, keepdims=True)
            s = jnp.zeros((8, wb), F32)
            for h in range(A_HPG):
                hs = slice(h * A_HEAD_DIM, (h + 1) * A_HEAD_DIM)
                part = jnp.sum(q_col[hs] * cache_ref[bl, 0, h], axis=0, keepdims=True)
                s = jnp.where(head_row == h, part, s)
            if dil > 1:
                s = jnp.where(keep, s, NEG)
            m = jnp.maximum(jnp.max(s, axis=1, keepdims=True), s0)
            p = jnp.exp(s - m)
            p0 = jnp.exp(s0 - m)
            l = jnp.sum(p, axis=1, keepdims=True) + p0
            lse = m + jnp.log(l)
            o_parts = []
            for h in range(A_HPG):
                hs = slice(h * A_HEAD_DIM, (h + 1) * A_HEAD_DIM)
                pv = jnp.sum(p[h:h + 1, :] * cache_ref[bl, 1, h], axis=1, keepdims=True)
                o_parts.append((pv + p0[h:h + 1, :] * v_col[hs]) / l[h:h + 1, :])
            o = jnp.concatenate(o_parts, axis=0)
            ot_scr[gi] = jnp.where(pick, o, ot_scr[gi])
            lt_scr[gi] = jnp.where(pick, lse, lt_scr[gi])

    @pl.when(i == pl.num_programs(0) - 1)
    def _():
        for h in range(A_HPG):
            hs = pl.ds(h * A_HEAD_DIM, A_HEAD_DIM)
            l0, l1, l2 = (lt_scr[gi, pl.ds(h, 1), :] for gi in range(3))
            mx = jnp.maximum(jnp.maximum(l0, l1), l2)
            e0, e1, e2 = jnp.exp(l0 - mx), jnp.exp(l1 - mx), jnp.exp(l2 - mx)
            tot = e0 * ot_scr[0, hs, :] + e1 * ot_scr[1, hs, :] + e2 * ot_scr[2, hs, :]
            ot_scr[0, hs, :] = tot / (e0 + e1 + e2)
        oa_ref[...] = ot_scr[0].T.astype(BF16)


def attn_sample(proj, cos, sin, caches_t, oa_dst, layer):
    rb = M_PROMPT // DEC_BATCH
    def pblk(c0):
        return pl.BlockSpec((DEC_BATCH, A_WIDTH), lambda i: (rb, c0 // A_WIDTH))
    tab = pl.BlockSpec((1, A_GW), lambda i: (0, 0))
    cache_specs = [pl.BlockSpec((None, AS_BB, 2, A_HPG, A_HEAD_DIM, ct.shape[-1]),
                                lambda i: (layer, i, 0, 0, 0, 0)) for ct in caches_t]
    ng = len(A_GROUPS)
    full3 = pl.BlockSpec((ng, A_GW, DEC_BATCH), lambda i: (0, 0, 0))
    return pl.pallas_call(
        _attn_sample_kernel,
        grid=(DEC_BATCH // AS_BB,),
        in_specs=[pblk(C_AQ), pblk(C_AK), pblk(C_AV), tab, tab] + cache_specs
                 + [pl.BlockSpec(memory_space=pl.ANY)],
        out_specs=[pl.BlockSpec((DEC_BATCH, A_GW), lambda i: (rb, 0)), full3, full3],
        out_shape=[jax.ShapeDtypeStruct((M_ROWS, A_GW), BF16),
                   jax.ShapeDtypeStruct((ng, A_GW, DEC_BATCH), F32),
                   jax.ShapeDtypeStruct((ng, A_GW, DEC_BATCH), F32)],
        scratch_shapes=[pltpu.VMEM((ng, A_GW, DEC_BATCH), F32), pltpu.VMEM((ng, 8, DEC_BATCH), F32),
                        pltpu.VMEM((ng, A_GW, DEC_BATCH), F32), pltpu.VMEM((ng, 8, DEC_BATCH), F32)],
        compiler_params=_cparams(("arbitrary",)),
        input_output_aliases={8: 0},
        name="attn_sample",
    )(proj, proj, proj, cos, sin, *caches_t, oa_dst)


def _pad_heads(wt, d, dp):
    c = wt.shape[1]
    wt = wt.reshape(M_HEADS, d, c)
    return jnp.pad(wt, ((0, 0), (0, dp - d), (0, 0))).reshape(M_HEADS * dp, c)


def _layer_weights(w_in_l, w_up_m, w_up_c, w_up_a, w_o_l):
    w_in_t = jnp.transpose(w_in_l)
    o = IN_OFFSETS
    piece = lambda i: w_in_t[o[i]:o[i + 1]]
    mq, mk, mv, mi, mf, mo, cb, cc, ch, aq, ak, av, gt = [piece(i) for i in range(13)]
    w_all = jnp.concatenate(
        [_pad_heads(mv, M_DV, DVP), _pad_heads(mo, M_DV, DVP),
         _pad_heads(mq, M_DK, DKP), _pad_heads(mk, M_DK, DKP),
         cb, cc, ch, aq, ak, av,
         mi, mf, jnp.zeros((PROJ_W - C_IF - 2 * M_HEADS, D_MODEL), F32), gt], axis=0).astype(BF16)
    w_um = jnp.pad(w_up_m.reshape(M_HEADS, M_DV, D_MODEL),
                   ((0, 0), (0, DVP - M_DV), (0, 0))).reshape(M_HEADS * DVP, D_MODEL)
    return (w_all, w_um.astype(BF16), w_up_c.astype(BF16), w_up_a.astype(BF16), w_o_l.astype(BF16))


def _rope_tables(pos):
    half = A_HEAD_DIM // 2
    inv = ROPE_THETA ** (-(2.0 * jnp.arange(half, dtype=F32)) / A_HEAD_DIM)
    ang = pos.astype(F32)[:, None] * inv[None, :]
    cos = jnp.cos(ang)
    sin = jnp.sin(ang)
    cos = jnp.tile(jnp.concatenate([cos, cos], axis=-1), (1, A_HPG))
    sin = jnp.tile(jnp.concatenate([-sin, sin], axis=-1), (1, A_HPG))
    return cos, sin


def kernel(x_prompt, x_sample, state_mlstm_C, state_mlstm_n, state_mlstm_m, state_conv,
           cache_attn_kv_w128, cache_attn_kv_w512, cache_attn_kv_w2048,
           w_in, b_gate_if, mlstm_norm_g, conv_w, w_up_mlstm, w_up_conv, w_up_attn, w_o,
           w_ffn_in, w_ffn_out, ln_g, ln_b):
    x = jnp.concatenate([x_prompt.reshape(M_PROMPT, D_MODEL),
                         x_sample.reshape(DEC_BATCH, D_MODEL)], axis=0)
    cos_p, sin_p = _rope_tables(jnp.arange(SEQ))
    cos_s, sin_s = _rope_tables(PAST_LEN + jnp.arange(1))
    w_ffn_in_b = w_ffn_in.astype(BF16)
    w_ffn_out_b = w_ffn_out.astype(BF16)

    c0t = jnp.transpose(state_mlstm_C, (0, 2, 3, 4, 1))
    n0t = jnp.transpose(state_mlstm_n, (0, 2, 3, 1))
    m0t = jnp.transpose(state_mlstm_m, (0, 2, 1)).reshape(DEPTH, M_HEADS, 1, DEC_BATCH)
    caches_t = [jnp.transpose(c, (0, 1, 3, 4, 5, 2))
                for c in (cache_attn_kv_w128, cache_attn_kv_w512, cache_attn_kv_w2048)]

    keeps = [min(win, SEQ) for win, _ in A_GROUPS]
    kv_all = [jnp.zeros((DEPTH, BATCH, 2, A_GW, kp), F32) for kp in keeps]
    sct_all = jnp.zeros((DEPTH, M_HEADS, M_DK, M_DV, DEC_BATCH), F32)
    hm = jnp.zeros((M_ROWS, M_HEADS * DVP), BF16)
    yc = jnp.zeros((M_ROWS, CONV_WIDTH), BF16)
    oa = jnp.zeros((M_ROWS, A_GW), BF16)

    p_states, s_states = [], []
    for l in range(DEPTH):
        w_all, w_um, w_uc, w_ua, w_ol = _layer_weights(
            w_in[l], w_up_mlstm[l], w_up_conv[l], w_up_attn[l], w_o[l])
        bias = jnp.pad(b_gate_if[l], (0, LANES - 2 * M_HEADS)).reshape(1, LANES)
        bias_col = jnp.broadcast_to(b_gate_if[l][:, None], (2 * M_HEADS, DEC_BATCH))
        gain = jnp.pad(mlstm_norm_g[l].reshape(M_HEADS, M_DV), ((0, 0), (0, DVP - M_DV)))
        gain_col = mlstm_norm_g[l].reshape(M_HEADS, M_DV, 1)

        x = ffn_ln(x, w_ffn_in_b, w_ffn_out_b, ln_g[l, 0], ln_b[l, 0], l, 0)
        proj = branch_proj(x, w_all)

        hm, pc, pn, pm = mlstm_prompt(proj, bias, gain.reshape(M_HEADS, 1, DVP), hm)
        hm, sct_all, snt, smt = mlstm_sample(proj, bias_col, gain_col, c0t, n0t, m0t, hm, sct_all, l)
        yc, pconv = conv_prompt(proj, conv_w[l], yc)
        yc, sconv = conv_sample(proj, state_conv[l], conv_w[l], yc)
        oa, *kv_all = attn_prompt(proj, cos_p[:, :LANES], sin_p[:, :LANES], oa, kv_all, l)
        oa, kt_s, vt_s = attn_sample(proj, cos_s, sin_s, caches_t, oa, l)

        x = merge_ln(x, hm, yc, oa, w_all, w_um, w_uc, w_ua, w_ol, ln_g[l, 1], ln_b[l, 1])
        x = ffn_ln(x, w_ffn_in_b, w_ffn_out_b, ln_g[l, 2], ln_b[l, 2], l, 1)

        kt_s = kt_s.reshape(3, A_HPG, A_HEAD_DIM, DEC_BATCH)
        vt_s = vt_s.reshape(3, A_HPG, A_HEAD_DIM, DEC_BATCH)
        kv_s = [jnp.stack([kt_s[gi], vt_s[gi]], axis=0) for gi in range(3)]
        p_states.append((pc, pn.reshape(BATCH, M_HEADS, M_DK), pm.reshape(BATCH, M_HEADS), pconv))
        s_states.append((snt, smt, sconv, kv_s[0], kv_s[1], kv_s[2]))

    y_prompt = x[:M_PROMPT].reshape(BATCH, SEQ, D_MODEL)
    y_sample = x[M_PROMPT:].reshape(DEC_BATCH, 1, D_MODEL)
    p_out = [jnp.stack(z) for z in zip(*p_states)]
    p_out += [jnp.transpose(kvt.reshape(DEPTH, BATCH, 2, A_HPG, A_HEAD_DIM, kvt.shape[-1]),
                            (0, 1, 5, 2, 3, 4))
              for kvt in kv_all]
    snt, smt, sconv, kv0, kv1, kv2 = [jnp.stack(z) for z in zip(*s_states)]
    s_out = [jnp.transpose(sct_all, (0, 4, 1, 2, 3)),
             jnp.transpose(snt, (0, 3, 1, 2)),
             jnp.transpose(smt.reshape(DEPTH, M_HEADS, DEC_BATCH), (0, 2, 1)),
             sconv]
    s_out += [jnp.transpose(kv, (0, 4, 1, 2, 3)).reshape(DEPTH, DEC_BATCH, 1, 2, A_HPG, A_HEAD_DIM)
              for kv in (kv0, kv1, kv2)]
    return (y_prompt, y_sample, *p_out, *s_out)
```

```python
import functools
import math

import jax
import jax.numpy as jnp
import numpy as np
from jax import lax
from jax.experimental import pallas as pl
from jax.experimental.pallas import tpu as pltpu

F32 = jnp.float32
BF16 = jnp.bfloat16

D_MODEL = 2048
BATCH = 4
SEQ = 2048
DEPTH = 2
DEC_BATCH = 128
PAST_LEN = 2048
M_HEADS = 4
M_DV = 192
M_DK = 96
M_QK = M_HEADS * M_DK
M_WIDTH = M_HEADS * M_DV
CONV_WIDTH = 512
CONV_K = 3
A_GROUPS = ((128, 1), (512, 4), (2048, 16))
A_HPG = 4
A_HEAD_DIM = 64
A_GW = A_HPG * A_HEAD_DIM
A_WIDTH = 3 * A_GW
ROPE_THETA = 10000.0
N_BRANCH = 3
D_FF = 5632
LN_EPS = 1e-5
ALPHA = (2 * DEPTH) ** 0.25
IN_SIZES = (M_QK, M_QK, M_WIDTH, M_HEADS, M_HEADS, M_WIDTH,
            CONV_WIDTH, CONV_WIDTH, CONV_WIDTH,
            A_WIDTH, A_WIDTH, A_WIDTH, N_BRANCH * D_MODEL)
IN_OFFSETS = tuple(int(o) for o in np.cumsum((0,) + IN_SIZES))

M_PROMPT = BATCH * SEQ
M_ROWS = M_PROMPT + DEC_BATCH

LANES = 128
DKP = 128
DVP = 256
VMEM_LIMIT = 52 * 1024 * 1024
VMEM_LIMIT_FFN = 56 * 1024 * 1024

C_MV, C_MO = 0, 1024
C_MQ, C_MK = 2048, 2560
C_CB, C_CC, C_CH = 3072, 3584, 4096
C_AQ, C_AK, C_AV = 4608, 5376, 6144
C_IF = 6912
PROJ_W = 7168

NEG = -1e30


def _sigmoid(x):
    return 1.0 / (1.0 + jnp.exp(-x))


def _layer_norm(z, g, b):
    mu = jnp.mean(z, axis=-1, keepdims=True)
    zc = z - mu
    var = jnp.mean(zc * zc, axis=-1, keepdims=True)
    return zc * lax.rsqrt(var + LN_EPS) * g + b


def _cparams(sem, vmem_limit=VMEM_LIMIT):
    return pltpu.CompilerParams(dimension_semantics=sem, vmem_limit_bytes=vmem_limit)


FFN_TM = 640
FFN_TF = 512


def _ffn_kernel(x_ref, wa_ref, wb_ref, wo_ref, g_ref, b_ref, y_ref, xb_scr):
    f = pl.program_id(1)

    @pl.when(f == 0)
    def _():
        xb_scr[...] = x_ref[...].astype(BF16)
        y_ref[...] = jnp.zeros_like(y_ref)

    xb = xb_scr[...]
    a = jnp.dot(xb, wa_ref[...].astype(BF16), preferred_element_type=F32)
    b = jnp.dot(xb, wb_ref[...].astype(BF16), preferred_element_type=F32)
    h = (a * _sigmoid(a)) * b
    y_ref[...] += jnp.dot(h.astype(BF16), wo_ref[...].astype(BF16), preferred_element_type=F32)

    @pl.when(f == pl.num_programs(1) - 1)
    def _():
        z = ALPHA * x_ref[...] + 0.5 * y_ref[...]
        y_ref[...] = _layer_norm(z, g_ref[...], b_ref[...])


def ffn_ln(x, w_in, w_out, g, b, layer, which):
    m = x.shape[0]
    nf = D_FF // FFN_TF
    return pl.pallas_call(
        _ffn_kernel,
        grid=(m // FFN_TM, nf),
        in_specs=[
            pl.BlockSpec((FFN_TM, D_MODEL), lambda i, f: (i, 0), pipeline_mode=pl.Buffered(1)),
            pl.BlockSpec((None, None, D_MODEL, FFN_TF), lambda i, f: (layer, which, 0, f)),
            pl.BlockSpec((None, None, D_MODEL, FFN_TF), lambda i, f: (layer, which, 0, f + nf)),
            pl.BlockSpec((None, None, FFN_TF, D_MODEL), lambda i, f: (layer, which, f, 0)),
            pl.BlockSpec((1, D_MODEL), lambda i, f: (0, 0)),
            pl.BlockSpec((1, D_MODEL), lambda i, f: (0, 0)),
        ],
        out_specs=pl.BlockSpec((FFN_TM, D_MODEL), lambda i, f: (i, 0)),
        out_shape=jax.ShapeDtypeStruct((m, D_MODEL), F32),
        scratch_shapes=[pltpu.VMEM((FFN_TM, D_MODEL), BF16)],
        compiler_params=_cparams(("parallel", "arbitrary"), VMEM_LIMIT_FFN),
        name="ffn_ln",
    )(x, w_in, w_in, w_out, g.reshape(1, D_MODEL), b.reshape(1, D_MODEL))


PROJ_TM = 1040
PROJ_TN = 1024


NT_DIMS = (((1,), (1,)), ((), ()))


def _proj_kernel(x_ref, wt_ref, o_ref, xb_scr):
    @pl.when(pl.program_id(1) == 0)
    def _():
        xb_scr[...] = x_ref[...].astype(BF16)

    o_ref[...] = lax.dot_general(xb_scr[...], wt_ref[...], NT_DIMS, preferred_element_type=F32)


def branch_proj(x, wt):
    m = x.shape[0]
    return pl.pallas_call(
        _proj_kernel,
        grid=(m // PROJ_TM, PROJ_W // PROJ_TN),
        in_specs=[pl.BlockSpec((PROJ_TM, D_MODEL), lambda i, j: (i, 0)),
                  pl.BlockSpec((PROJ_TN, D_MODEL), lambda i, j: (j, 0))],
        out_specs=pl.BlockSpec((PROJ_TM, PROJ_TN), lambda i, j: (i, j)),
        out_shape=jax.ShapeDtypeStruct((m, PROJ_W), F32),
        scratch_shapes=[pltpu.VMEM((PROJ_TM, D_MODEL), BF16)],
        compiler_params=_cparams(("parallel", "arbitrary")),
        name="branch_proj",
    )(x, wt)


MRG_TM = 640
MRG_TN = 512


def _gate_up_kernel(x_ref, hm_ref, yc_ref, oa_ref, wg0_ref, wg1_ref, wg2_ref,
                    wum_ref, wuc_ref, wua_ref, o_ref, xb_scr):
    @pl.when(pl.program_id(1) == 0)
    def _():
        xb_scr[...] = x_ref[...].astype(BF16)

    xb = xb_scr[...]

    def gated(wg_ref, br_ref, wu_ref):
        gate = _sigmoid(lax.dot_general(xb, wg_ref[...], NT_DIMS, preferred_element_type=F32))
        return gate * jnp.dot(br_ref[...], wu_ref[...], preferred_element_type=F32)

    merged = (gated(wg0_ref, hm_ref, wum_ref) + gated(wg1_ref, yc_ref, wuc_ref)
              + gated(wg2_ref, oa_ref, wua_ref))
    o_ref[...] = merged.astype(BF16)


def _out_ln_kernel(x_ref, mg_ref, wo_ref, g_ref, b_ref, y_ref):
    z = ALPHA * x_ref[...] + jnp.dot(mg_ref[...], wo_ref[...], preferred_element_type=F32)
    y_ref[...] = _layer_norm(z, g_ref[...], b_ref[...])


def merge_ln(x, hm, yc, oa, w_all, wum, wuc, wua, wo, g, b):
    m = x.shape[0]
    nn = D_MODEL // MRG_TN
    g0 = PROJ_W // MRG_TN
    row = lambda w: pl.BlockSpec((MRG_TM, w), lambda i, n: (i, 0))
    merged = pl.pallas_call(
        _gate_up_kernel,
        grid=(m // MRG_TM, nn),
        in_specs=[
            row(D_MODEL), row(M_HEADS * DVP), row(CONV_WIDTH), row(A_GW),
            pl.BlockSpec((MRG_TN, D_MODEL), lambda i, n: (g0 + n, 0)),
            pl.BlockSpec((MRG_TN, D_MODEL), lambda i, n: (g0 + n + nn, 0)),
            pl.BlockSpec((MRG_TN, D_MODEL), lambda i, n: (g0 + n + 2 * nn, 0)),
            pl.BlockSpec((M_HEADS * DVP, MRG_TN), lambda i, n: (0, n)),
            pl.BlockSpec((CONV_WIDTH, MRG_TN), lambda i, n: (0, n)),
            pl.BlockSpec((A_GW, MRG_TN), lambda i, n: (0, n)),
        ],
        out_specs=pl.BlockSpec((MRG_TM, MRG_TN), lambda i, n: (i, n)),
        out_shape=jax.ShapeDtypeStruct((m, D_MODEL), BF16),
        scratch_shapes=[pltpu.VMEM((MRG_TM, D_MODEL), BF16)],
        compiler_params=_cparams(("parallel", "arbitrary")),
        name="gate_up",
    )(x, hm, yc, oa, w_all, w_all, w_all, wum, wuc, wua)
    rows = pl.BlockSpec((MRG_TM, D_MODEL), lambda i: (i, 0))
    vec = pl.BlockSpec((1, D_MODEL), lambda i: (0, 0))
    return pl.pallas_call(
        _out_ln_kernel,
        grid=(m // MRG_TM,),
        in_specs=[rows, rows, pl.BlockSpec((D_MODEL, D_MODEL), lambda i: (0, 0)), vec, vec],
        out_specs=rows,
        out_shape=jax.ShapeDtypeStruct((m, D_MODEL), F32),
        compiler_params=_cparams(("parallel",)),
        name="out_ln",
    )(x, merged, wo, g.reshape(1, D_MODEL), b.reshape(1, D_MODEL))


def _conv_prompt_kernel(cb_ref, cc_ref, ch_ref, w_ref, y_dst_ref, y_ref, st_ref, u_scr):
    del y_dst_ref
    u = cc_ref[...] * ch_ref[...]
    u_scr[pl.ds(0, 8), :] = jnp.zeros((8, CONV_WIDTH), F32)
    u_scr[pl.ds(8, SEQ), :] = u
    w = w_ref[...]
    acc = (w[0:1, :] * u_scr[pl.ds(6, SEQ), :] + w[1:2, :] * u_scr[pl.ds(7, SEQ), :]
           + w[2:3, :] * u)
    y_ref[...] = (cb_ref[...] * acc).astype(BF16)
    st_ref[...] = u_scr[pl.ds(8 + SEQ - (CONV_K - 1), CONV_K - 1), :]


def conv_prompt(proj, conv_w, y_dst):
    blk = lambda c: pl.BlockSpec((SEQ, CONV_WIDTH), lambda b, c=c: (b, c // CONV_WIDTH))
    return pl.pallas_call(
        _conv_prompt_kernel,
        grid=(BATCH,),
        in_specs=[blk(C_CB), blk(C_CC), blk(C_CH),
                  pl.BlockSpec((CONV_K, CONV_WIDTH), lambda b: (0, 0)),
                  pl.BlockSpec(memory_space=pl.ANY)],
        out_specs=[pl.BlockSpec((SEQ, CONV_WIDTH), lambda b: (b, 0)),
                   pl.BlockSpec((None, CONV_K - 1, CONV_WIDTH), lambda b: (b, 0, 0))],
        out_shape=[jax.ShapeDtypeStruct((M_ROWS, CONV_WIDTH), BF16),
                   jax.ShapeDtypeStruct((BATCH, CONV_K - 1, CONV_WIDTH), F32)],
        input_output_aliases={4: 0},
        scratch_shapes=[pltpu.VMEM((SEQ + 8, CONV_WIDTH), F32)],
        compiler_params=_cparams(("parallel",)),
        name="conv_prompt",
    )(proj, proj, proj, conv_w, y_dst)


def _conv_sample_kernel(cb_ref, cc_ref, ch_ref, prev_ref, w_ref, y_dst_ref, y_ref, st_ref):
    del y_dst_ref
    u = cc_ref[...] * ch_ref[...]
    w = w_ref[...]
    p0 = prev_ref[:, 0, :]
    p1 = prev_ref[:, 1, :]
    acc = w[0:1, :] * p0 + w[1:2, :] * p1 + w[2:3, :] * u
    y_ref[...] = (cb_ref[...] * acc).astype(BF16)
    st_ref[:, 0, :] = p1
    st_ref[:, 1, :] = u


def conv_sample(proj, prev, conv_w, y_dst):
    rb = M_PROMPT // DEC_BATCH
    blk = lambda c: pl.BlockSpec((DEC_BATCH, CONV_WIDTH), lambda i, c=c: (rb, c // CONV_WIDTH))
    full3 = pl.BlockSpec((DEC_BATCH, CONV_K - 1, CONV_WIDTH), lambda i: (0, 0, 0))
    return pl.pallas_call(
        _conv_sample_kernel,
        grid=(1,),
        in_specs=[blk(C_CB), blk(C_CC), blk(C_CH), full3,
                  pl.BlockSpec((CONV_K, CONV_WIDTH), lambda i: (0, 0)),
                  pl.BlockSpec(memory_space=pl.ANY)],
        out_specs=[pl.BlockSpec((DEC_BATCH, CONV_WIDTH), lambda i: (rb, 0)), full3],
        out_shape=[jax.ShapeDtypeStruct((M_ROWS, CONV_WIDTH), BF16),
                   jax.ShapeDtypeStruct((DEC_BATCH, CONV_K - 1, CONV_WIDTH), F32)],
        input_output_aliases={5: 0},
        compiler_params=_cparams(("arbitrary",)),
        name="conv_sample",
    )(proj, proj, proj, prev, conv_w, y_dst)


M_L = 128


def _log_sigmoid(x):
    return jnp.minimum(x, 0.0) - jnp.log1p(jnp.exp(-jnp.abs(x)))


def _head_norm_gate(h, o_pre, gain):
    lane = lax.broadcasted_iota(jnp.int32, h.shape, 1)
    real = lane < M_DV
    mu = jnp.sum(h, axis=-1, keepdims=True) * (1.0 / M_DV)
    hc = jnp.where(real, h - mu, 0.0)
    var = jnp.sum(hc * hc, axis=-1, keepdims=True) * (1.0 / M_DV)
    return _sigmoid(o_pre) * (hc * lax.rsqrt(var + LN_EPS) * gain)


M_TS = 512


M_NROW = M_DV


def _mlstm_prompt_kernel(q_ref, k_ref, v_ref, o_ref, if_ref, bias_ref, gain_ref, hm_dst_ref,
                         hm_ref, c_out_ref, n_out_ref, m_out_ref, ct_scr, m_scr):
    del hm_dst_ref
    step = pl.program_id(1)

    @pl.when(step == 0)
    def _():
        ct_scr[...] = jnp.zeros_like(ct_scr)
        m_scr[...] = jnp.zeros_like(m_scr)

    row = lax.broadcasted_iota(jnp.int32, (M_L, M_L), 0)
    col = lax.broadcasted_iota(jnp.int32, (M_L, M_L), 1)
    causal_t = row <= col
    tri = (col <= row).astype(F32)
    bias = bias_ref[...]
    ones_lane = lax.broadcasted_iota(jnp.int32, (M_L, DVP), 1) == M_NROW
    real_rows = lax.broadcasted_iota(jnp.int32, (DVP, M_L), 0) < M_DV
    tn = (((0,), (0,)), ((), ()))

    def chunk(c, carry):
        r0 = pl.multiple_of(c * M_L, M_L)
        x_if = if_ref[pl.ds(r0, M_L), :] + bias
        log_f = _log_sigmoid(x_if)
        cs = jnp.dot(tri, log_f, preferred_element_type=F32, precision=lax.Precision.HIGHEST)
        zt = jnp.where(col < M_HEADS, x_if, cs).T
        for hd in range(M_HEADS):
            b_row = zt[M_HEADS + hd:M_HEADS + hd + 1, :]
            c_col = x_if[:, hd:hd + 1] - cs[:, M_HEADS + hd:M_HEADS + hd + 1]
            c_rep = jnp.broadcast_to(c_col, (M_L, M_L))
            m_prev = m_scr[hd]
            b_last = b_row[:, M_L - 1:M_L]

            d_t = jnp.where(causal_t, b_row + c_rep, NEG)
            inter = b_row + m_prev
            m_t = jnp.maximum(jnp.max(d_t, axis=0, keepdims=True), inter)
            q = q_ref[pl.ds(r0, M_L), pl.ds(hd * DKP, DKP)].astype(BF16)
            k = k_ref[pl.ds(r0, M_L), pl.ds(hd * DKP, DKP)] * (M_DK ** -0.5)
            v1 = jnp.where(ones_lane, 1.0, v_ref[pl.ds(r0, M_L), pl.ds(hd * DVP, DVP)]).astype(BF16)
            s_t = lax.dot_general(k.astype(BF16), q, NT_DIMS, preferred_element_type=F32) * jnp.exp(d_t - m_t)
            w_inter = jnp.exp(inter - m_t)
            ct_prev = ct_scr[hd]
            num_t = (lax.dot_general(v1, s_t.astype(BF16), tn, preferred_element_type=F32)
                     + w_inter * lax.dot_general(ct_prev.astype(BF16), q, NT_DIMS,
                                                 preferred_element_type=F32))
            den = num_t[M_NROW:M_NROW + 1, :]
            h_t = jnp.where(real_rows, num_t / jnp.maximum(jnp.abs(den), jnp.exp(-m_t)), 0.0)
            mu = jnp.sum(h_t, axis=0, keepdims=True) * (1.0 / M_DV)
            hc = jnp.where(real_rows, h_t - mu, 0.0)
            var = jnp.sum(hc * hc, axis=0, keepdims=True) * (1.0 / M_DV)
            hn = (hc * lax.rsqrt(var + LN_EPS)).T * gain_ref[hd]
            o_pre = o_ref[pl.ds(r0, M_L), pl.ds(hd * DVP, DVP)]
            hm_ref[pl.ds(r0, M_L), pl.ds(hd * DVP, DVP)] = (_sigmoid(o_pre) * hn).astype(BF16)

            m_new = jnp.maximum(b_last + m_prev, b_last + jnp.max(c_rep, axis=0, keepdims=True)[:, 0:1])
            kw = k * jnp.exp(c_rep + (b_last - m_new))
            ct_scr[hd] = (jnp.exp(b_last + m_prev - m_new) * ct_prev
                          + lax.dot_general(v1, kw.astype(BF16), tn, preferred_element_type=F32))
            m_scr[hd] = m_new
        return carry

    lax.fori_loop(0, M_TS // M_L, chunk, 0)

    @pl.when(step == pl.num_programs(1) - 1)
    def _():
        for hd in range(M_HEADS):
            ct = ct_scr[hd]
            c_out_ref[hd] = ct.T[0:M_DK, 0:M_DV]
            n_out_ref[hd] = ct[M_NROW:M_NROW + 1, 0:M_DK]
            m_out_ref[hd] = m_scr[hd]


def mlstm_prompt(proj, bias, gain, hm_dst):
    ns = SEQ // M_TS
    def cblk(c0, w):
        return pl.BlockSpec((M_TS, w), lambda b, s: (b * ns + s, c0 // w))
    return pl.pallas_call(
        _mlstm_prompt_kernel,
        grid=(BATCH, ns),
        in_specs=[cblk(C_MQ, M_HEADS * DKP), cblk(C_MK, M_HEADS * DKP),
                  cblk(C_MV, M_HEADS * DVP), cblk(C_MO, M_HEADS * DVP),
                  cblk(C_IF, LANES),
                  pl.BlockSpec((1, LANES), lambda b, s: (0, 0)),
                  pl.BlockSpec((M_HEADS, 1, DVP), lambda b, s: (0, 0, 0)),
                  pl.BlockSpec(memory_space=pl.ANY)],
        out_specs=[pl.BlockSpec((M_TS, M_HEADS * DVP), lambda b, s: (b * ns + s, 0)),
                   pl.BlockSpec((None, M_HEADS, M_DK, M_DV), lambda b, s: (b, 0, 0, 0)),
                   pl.BlockSpec((None, M_HEADS, 1, M_DK), lambda b, s: (b, 0, 0, 0)),
                   pl.BlockSpec((None, M_HEADS, 1, 1), lambda b, s: (b, 0, 0, 0))],
        out_shape=[jax.ShapeDtypeStruct((M_ROWS, M_HEADS * DVP), BF16),
                   jax.ShapeDtypeStruct((BATCH, M_HEADS, M_DK, M_DV), F32),
                   jax.ShapeDtypeStruct((BATCH, M_HEADS, 1, M_DK), F32),
                   jax.ShapeDtypeStruct((BATCH, M_HEADS, 1, 1), F32)],
        scratch_shapes=[pltpu.VMEM((M_HEADS, DVP, DKP), F32), pltpu.VMEM((M_HEADS, 1, 1), F32)],
        compiler_params=_cparams(("parallel", "arbitrary")),
        input_output_aliases={7: 0},
        name="mlstm_prompt",
    )(proj, proj, proj, proj, proj, bias, gain, hm_dst)


MS_DC = 48


def _pick_row(x8, j):
    rows = lax.broadcasted_iota(jnp.int32, x8.shape, 0)
    return jnp.sum(jnp.where(rows == j, x8, 0.0), axis=0, keepdims=True)


def _mlstm_sample_kernel(q_ref, k_ref, v_ref, o_ref, if_ref, bias_ref, gain_ref,
                         c0_ref, n0_ref, m0_ref, hm_dst_ref, c_dst_ref,
                         hm_ref, c_out_ref, n_out_ref, m_out_ref,
                         qt_scr, kw_scr, vt_scr, acc_scr, st_scr):
    del hm_dst_ref, c_dst_ref
    hd = pl.program_id(0)
    c = pl.program_id(1)

    @pl.when(c == 0)
    def _():
        qt = q_ref[...].T
        kt = (k_ref[...] * (M_DK ** -0.5)).T
        vt_scr[...] = v_ref[...].T
        x_if = if_ref[...].T[0:2 * M_HEADS, :] + bias_ref[...]
        i_pre = _pick_row(x_if, hd)
        log_f = _log_sigmoid(_pick_row(x_if, hd + M_HEADS))
        inter = log_f + m0_ref[...]
        m_new = jnp.maximum(i_pre, inter)
        w_k = jnp.exp(i_pre - m_new)
        decay = jnp.exp(inter - m_new)
        n_prev = n0_ref[...]
        s = jnp.sum(qt * kt, axis=0, keepdims=True) * w_k
        den = s + decay * jnp.sum(qt[:M_DK] * n_prev, axis=0, keepdims=True)
        kw = kt * w_k
        qt_scr[...] = qt
        kw_scr[...] = kw
        n_out_ref[...] = decay * n_prev + kw[:M_DK]
        m_out_ref[...] = m_new
        st_scr[0:1, :] = s
        st_scr[1:2, :] = decay
        st_scr[2:3, :] = den
        st_scr[3:4, :] = m_new
        acc_scr[...] = jnp.zeros_like(acc_scr)

    decay = st_scr[1:2, :]
    vt = vt_scr[pl.ds(0, M_DV), :]

    def tile(t, acc):
        r8 = pl.multiple_of(c * MS_DC + t * 8, 8)
        q8 = qt_scr[pl.ds(r8, 8), :]
        kw8 = kw_scr[pl.ds(r8, 8), :]
        for r in range(8):
            c_row = c0_ref[t * 8 + r]
            c_out_ref[t * 8 + r] = decay * c_row + kw8[r:r + 1, :] * vt
            acc = acc + q8[r:r + 1, :] * c_row
        return acc

    acc = lax.fori_loop(0, MS_DC // 8, tile, acc_scr[...])
    acc_scr[...] = acc

    @pl.when(c == pl.num_programs(1) - 1)
    def _():
        s = st_scr[0:1, :]
        den = st_scr[2:3, :]
        m_t = st_scr[3:4, :]
        h = (s * vt + decay * acc) / jnp.maximum(jnp.abs(den), jnp.exp(-m_t))
        mu = jnp.mean(h, axis=0, keepdims=True)
        hc = h - mu
        var = jnp.mean(hc * hc, axis=0, keepdims=True)
        o_pre = o_ref[...].T[:M_DV, :]
        out = _sigmoid(o_pre) * (hc * lax.rsqrt(var + LN_EPS) * gain_ref[...])
        out = jnp.concatenate([out, jnp.zeros((DVP - M_DV, DEC_BATCH), F32)], axis=0)
        hm_ref[...] = out.T.astype(BF16)


def mlstm_sample(proj, bias_col, gain_col, c0t, n0t, m0t, hm_dst, c_dst, layer):
    any_spec = pl.BlockSpec(memory_space=pl.ANY)
    rb = M_PROMPT // DEC_BATCH
    nc = M_DK // MS_DC
    def cblk(c0_, w):
        return pl.BlockSpec((DEC_BATCH, w), lambda h, c: (rb, c0_ // w + h))
    return pl.pallas_call(
        _mlstm_sample_kernel,
        grid=(M_HEADS, nc),
        in_specs=[cblk(C_MQ, DKP), cblk(C_MK, DKP), cblk(C_MV, DVP), cblk(C_MO, DVP),
                  pl.BlockSpec((DEC_BATCH, LANES), lambda h, c: (rb, C_IF // LANES)),
                  pl.BlockSpec((2 * M_HEADS, DEC_BATCH), lambda h, c: (0, 0)),
                  pl.BlockSpec((None, M_DV, 1), lambda h, c: (h, 0, 0)),
                  pl.BlockSpec((None, None, MS_DC, M_DV, DEC_BATCH), lambda h, c: (layer, h, c, 0, 0)),
                  pl.BlockSpec((None, None, M_DK, DEC_BATCH), lambda h, c: (layer, h, 0, 0)),
                  pl.BlockSpec((None, None, 1, DEC_BATCH), lambda h, c: (layer, h, 0, 0)),
                  any_spec, any_spec],
        out_specs=[pl.BlockSpec((DEC_BATCH, DVP), lambda h, c: (rb, h)),
                   pl.BlockSpec((None, None, MS_DC, M_DV, DEC_BATCH), lambda h, c: (layer, h, c, 0, 0)),
                   pl.BlockSpec((None, M_DK, DEC_BATCH), lambda h, c: (h, 0, 0)),
                   pl.BlockSpec((None, 1, DEC_BATCH), lambda h, c: (h, 0, 0))],
        out_shape=[jax.ShapeDtypeStruct((M_ROWS, M_HEADS * DVP), BF16),
                   jax.ShapeDtypeStruct((DEPTH, M_HEADS, M_DK, M_DV, DEC_BATCH), F32),
                   jax.ShapeDtypeStruct((M_HEADS, M_DK, DEC_BATCH), F32),
                   jax.ShapeDtypeStruct((M_HEADS, 1, DEC_BATCH), F32)],
        scratch_shapes=[pltpu.VMEM((DKP, DEC_BATCH), F32), pltpu.VMEM((DKP, DEC_BATCH), F32),
                        pltpu.VMEM((DVP, DEC_BATCH), F32), pltpu.VMEM((M_DV, DEC_BATCH), F32),
                        pltpu.VMEM((8, DEC_BATCH), F32)],
        input_output_aliases={10: 0, 11: 1},
        compiler_params=_cparams(("arbitrary", "arbitrary")),
        name="mlstm_sample",
    )(proj, proj, proj, proj, proj, bias_col, gain_col, c0t, n0t, m0t, hm_dst, c_dst)


A_Q = 128
A_LT = A_GW // LANES


def _rope(x, cos, sin_signed):
    lane = lax.broadcasted_iota(jnp.int32, x.shape, 1)
    first_half = (lane % A_HEAD_DIM) < (A_HEAD_DIM // 2)
    partner = jnp.where(first_half, pltpu.roll(x, x.shape[1] - A_HEAD_DIM // 2, 1),
                        pltpu.roll(x, A_HEAD_DIM // 2, 1))
    return x * cos + partner * sin_signed


def _head_masks(shape):
    lane = lax.broadcasted_iota(jnp.int32, shape, 1)
    return [(lane // A_HEAD_DIM) == h for h in range(A_HPG)]


def _attn_group_prompt(dil, gi, qs_scr, ks_scr, vs_scr, o_scr, l_scr):
    length = SEQ // dil
    nb = length // A_Q
    row = lax.broadcasted_iota(jnp.int32, (A_Q, A_Q), 0)
    col = lax.broadcasted_iota(jnp.int32, (A_Q, A_Q), 1)
    cur_ok = col <= row
    prev_ok = col >= row
    masks = _head_masks((A_Q, A_GW))
    nt = (((1,), (1,)), ((), ()))

    def window(start):
        if dil == 1:
            return pl.ds(pl.multiple_of(start, A_Q), A_Q)
        return pl.ds(start, A_Q, stride=dil)

    def rows(scr, start):
        w = window(start)
        return jnp.concatenate([scr[t, w, :] for t in range(A_LT)], axis=1).astype(BF16)

    def block(idx, carry):
        r = idx % dil
        n = idx // dil
        base = r + (dil * A_Q) * n
        qb = rows(qs_scr, base)
        kc = rows(ks_scr, base)
        vc = rows(vs_scr, base)
        if nb > 1:
            pbase = jnp.maximum(base - dil * A_Q, r)
            kp = rows(ks_scr, pbase)
            vp = rows(vs_scr, pbase)
            has_prev = n > 0
        o_acc = jnp.zeros((A_Q, A_GW), F32)
        l_acc = jnp.zeros((A_Q, A_GW), F32)
        for h in range(A_HPG):
            qh = jnp.where(masks[h], qb, jnp.zeros_like(qb))
            s_c = jnp.where(cur_ok, lax.dot_general(qh, kc, nt, preferred_element_type=F32), NEG)
            m = jnp.max(s_c, axis=1, keepdims=True)
            if nb > 1:
                s_p = jnp.where(jnp.logical_and(prev_ok, has_prev),
                                lax.dot_general(qh, kp, nt, preferred_element_type=F32), NEG)
                m = jnp.maximum(m, jnp.max(s_p, axis=1, keepdims=True))
            p_c = jnp.exp(s_c - m)
            l = jnp.sum(p_c, axis=1, keepdims=True)
            o_h = jnp.dot(p_c.astype(BF16), vc, preferred_element_type=F32)
            if nb > 1:
                p_p = jnp.exp(s_p - m)
                l = l + jnp.sum(p_p, axis=1, keepdims=True)
                o_h = o_h + jnp.dot(p_p.astype(BF16), vp, preferred_element_type=F32)
            o_acc = o_acc + jnp.where(masks[h], o_h / l, 0.0)
            l_acc = l_acc + jnp.where(masks[h], m + jnp.log(l), 0.0)
        w = window(base)
        for t in range(A_LT):
            o_scr[gi, t, w, :] = o_acc[:, t * LANES:(t + 1) * LANES]
            l_scr[gi, t, w, :] = l_acc[:, t * LANES:(t + 1) * LANES]
        return carry

    lax.fori_loop(0, dil * nb, block, 0)


def _attn_prompt_kernel(q_ref, k_ref, v_ref, cos_ref, sin_ref, oa_dst_ref, kvd0_ref, kvd1_ref, kvd2_ref,
                        oa_ref, kv0_ref, kv1_ref, kv2_ref, qs_scr, ks_scr, vs_scr, o_scr, l_scr):
    del oa_dst_ref, kvd0_ref, kvd1_ref, kvd2_ref
    g = pl.program_id(1)
    rc = 256

    def rope_rows(c, carry):
        sl = pl.ds(pl.multiple_of(c * rc, rc), rc)
        cos = cos_ref[sl, :]
        sin = sin_ref[sl, :]
        for t in range(A_LT):
            lanes = pl.ds(t * LANES, LANES)
            qs_scr[t, sl, :] = _rope(q_ref[sl, lanes], cos, sin) * (A_HEAD_DIM ** -0.5)
            ks_scr[t, sl, :] = _rope(k_ref[sl, lanes], cos, sin)
            vs_scr[t, sl, :] = v_ref[sl, lanes]
        return carry

    lax.fori_loop(0, SEQ // rc, rope_rows, 0)

    for gi, ((win, dil), kv_ref) in enumerate(zip(A_GROUPS, (kv0_ref, kv1_ref, kv2_ref))):
        @pl.when(g == gi)
        def _(gi=gi, dil=dil, win=win, kv_ref=kv_ref):
            _attn_group_prompt(dil, gi, qs_scr, ks_scr, vs_scr, o_scr, l_scr)
            keep = min(win, SEQ)
            for c in range(keep // LANES):
                rows = pl.ds(SEQ - keep + c * LANES, LANES)
                for t in range(A_LT):
                    kv_ref[0, pl.ds(t * LANES, LANES), pl.ds(c * LANES, LANES)] = ks_scr[t, rows, :].T
                    kv_ref[1, pl.ds(t * LANES, LANES), pl.ds(c * LANES, LANES)] = vs_scr[t, rows, :].T

    @pl.when(g == len(A_GROUPS) - 1)
    def _():
        def comb(c, carry):
            sl = pl.ds(pl.multiple_of(c * rc, rc), rc)
            for t in range(A_LT):
                l0, l1, l2 = l_scr[0, t, sl, :], l_scr[1, t, sl, :], l_scr[2, t, sl, :]
                mx = jnp.maximum(jnp.maximum(l0, l1), l2)
                e0, e1, e2 = jnp.exp(l0 - mx), jnp.exp(l1 - mx), jnp.exp(l2 - mx)
                tot = e0 * o_scr[0, t, sl, :] + e1 * o_scr[1, t, sl, :] + e2 * o_scr[2, t, sl, :]
                oa_ref[sl, pl.ds(t * LANES, LANES)] = (tot / (e0 + e1 + e2)).astype(BF16)
            return carry
        lax.fori_loop(0, SEQ // rc, comb, 0)


def attn_prompt(proj, cos, sin, oa_dst, kv_dst, layer):
    def gblk(c0):
        return pl.BlockSpec((SEQ, A_GW), lambda b, g: (b, c0 // A_GW + g))
    tab = pl.BlockSpec((SEQ, LANES), lambda b, g: (0, 0))
    keeps = [min(win, SEQ) for win, _ in A_GROUPS]
    return pl.pallas_call(
        _attn_prompt_kernel,
        grid=(BATCH, len(A_GROUPS)),
        in_specs=[gblk(C_AQ), gblk(C_AK), gblk(C_AV), tab, tab] + [pl.BlockSpec(memory_space=pl.ANY)] * 4,
        out_specs=[pl.BlockSpec((SEQ, A_GW), lambda b, g: (b, 0))]
                  + [pl.BlockSpec((None, None, 2, A_GW, kp), lambda b, g: (layer, b, 0, 0, 0)) for kp in keeps],
        out_shape=[jax.ShapeDtypeStruct((M_ROWS, A_GW), BF16)]
                  + [jax.ShapeDtypeStruct((DEPTH, BATCH, 2, A_GW, kp), F32) for kp in keeps],
        input_output_aliases={5: 0, 6: 1, 7: 2, 8: 3},
        scratch_shapes=[pltpu.VMEM((A_LT, SEQ, LANES), F32),
                        pltpu.VMEM((A_LT, SEQ, LANES), F32),
                        pltpu.VMEM((A_LT, SEQ, LANES), F32),
                        pltpu.VMEM((len(A_GROUPS), A_LT, SEQ, LANES), F32),
                        pltpu.VMEM((len(A_GROUPS), A_LT, SEQ, LANES), F32)],
        compiler_params=_cparams(("parallel", "arbitrary")),
        name="attn_prompt",
    )(proj, proj, proj, cos, sin, oa_dst, *kv_dst)


AS_BB = 2


def _attn_sample_kernel(q_ref, k_ref, v_ref, cos_ref, sin_ref, c0_ref, c1_ref, c2_ref, oa_dst_ref,
                        oa_ref, kt_ref, vt_ref, qt_scr, s0_scr, ot_scr, lt_scr):
    del oa_dst_ref
    i = pl.program_id(0)
    lane_b = lax.broadcasted_iota(jnp.int32, (1, DEC_BATCH), 1)
    sub8 = lax.broadcasted_iota(jnp.int32, (8, DEC_BATCH), 0)

    @pl.when(i == 0)
    def _():
        cos = cos_ref[...]
        sin = sin_ref[...]
        for gi in range(len(A_GROUPS)):
            gs = pl.ds(gi * A_GW, A_GW)
            qt = (_rope(q_ref[:, gs], cos, sin) * (A_HEAD_DIM ** -0.5)).T
            kt = _rope(k_ref[:, gs], cos, sin).T
            qt_scr[gi] = qt
            kt_ref[gi] = kt
            vt_ref[gi] = v_ref[:, gs].T
            prod = qt * kt
            s0 = jnp.zeros((8, DEC_BATCH), F32)
            for h in range(A_HPG):
                part = jnp.sum(prod[h * A_HEAD_DIM:(h + 1) * A_HEAD_DIM], axis=0, keepdims=True)
                s0 = jnp.where(sub8 == h, part, s0)
            s0_scr[gi] = s0
        ot_scr[...] = jnp.zeros_like(ot_scr)
        lt_scr[...] = jnp.zeros_like(lt_scr)

    for bl in range(AS_BB):
        pick = lane_b == i * AS_BB + bl
        for gi, (cache_ref, (_, dil)) in enumerate(zip((c0_ref, c1_ref, c2_ref), A_GROUPS)):
            wb = cache_ref.shape[-1]
            if dil > 1:
                keep = (lax.broadcasted_iota(jnp.int32, (1, wb), 1) & (dil - 1)) == 0
            head_row = lax.broadcasted_iota(jnp.int32, (8, wb), 0)
            q_col = jnp.sum(jnp.where(pick, qt_scr[gi], 0.0), axis=1, keepdims=True)
            v_col = jnp.sum(jnp.where(pick, vt_ref[gi], 0.0), axis=1, keepdims=True)
            s0 = jnp.sum(jnp.where(pick, s0_scr[gi], 0.0), axis=1, keepdims=True)
            s = jnp.zeros((8, wb), F32)
            for h in range(A_HPG):
                hs = slice(h * A_HEAD_DIM, (h + 1) * A_HEAD_DIM)
                part = jnp.sum(q_col[hs] * cache_ref[bl, 0, h], axis=0, keepdims=True)
                s = jnp.where(head_row == h, part, s)
            if dil > 1:
                s = jnp.where(keep, s, NEG)
            m = jnp.maximum(jnp.max(s, axis=1, keepdims=True), s0)
            p = jnp.exp(s - m)
            p0 = jnp.exp(s0 - m)
            l = jnp.sum(p, axis=1, keepdims=True) + p0
            lse = m + jnp.log(l)
            o_parts = []
            for h in range(A_HPG):
                hs = slice(h * A_HEAD_DIM, (h + 1) * A_HEAD_DIM)
                pv = jnp.sum(p[h:h + 1, :] * cache_ref[bl, 1, h], axis=1, keepdims=True)
                o_parts.append((pv + p0[h:h + 1, :] * v_col[hs]) / l[h:h + 1, :])
            o = jnp.concatenate(o_parts, axis=0)
            ot_scr[gi] = jnp.where(pick, o, ot_scr[gi])
            lt_scr[gi] = jnp.where(pick, lse, lt_scr[gi])

    @pl.when(i == pl.num_programs(0) - 1)
    def _():
        for h in range(A_HPG):
            hs = pl.ds(h * A_HEAD_DIM, A_HEAD_DIM)
            l0, l1, l2 = (lt_scr[gi, pl.ds(h, 1), :] for gi in range(3))
            mx = jnp.maximum(jnp.maximum(l0, l1), l2)
            e0, e1, e2 = jnp.exp(l0 - mx), jnp.exp(l1 - mx), jnp.exp(l2 - mx)
            tot = e0 * ot_scr[0, hs, :] + e1 * ot_scr[1, hs, :] + e2 * ot_scr[2, hs, :]
            ot_scr[0, hs, :] = tot / (e0 + e1 + e2)
        oa_ref[...] = ot_scr[0].T.astype(BF16)


def attn_sample(proj, cos, sin, caches_t, oa_dst, layer):
    rb = M_PROMPT // DEC_BATCH
    def pblk(c0):
        return pl.BlockSpec((DEC_BATCH, A_WIDTH), lambda i: (rb, c0 // A_WIDTH))
    tab = pl.BlockSpec((1, A_GW), lambda i: (0, 0))
    cache_specs = [pl.BlockSpec((None, AS_BB, 2, A_HPG, A_HEAD_DIM, ct.shape[-1]),
                                lambda i: (layer, i, 0, 0, 0, 0)) for ct in caches_t]
    ng = len(A_GROUPS)
    full3 = pl.BlockSpec((ng, A_GW, DEC_BATCH), lambda i: (0, 0, 0))
    return pl.pallas_call(
        _attn_sample_kernel,
        grid=(DEC_BATCH // AS_BB,),
        in_specs=[pblk(C_AQ), pblk(C_AK), pblk(C_AV), tab, tab] + cache_specs
                 + [pl.BlockSpec(memory_space=pl.ANY)],
        out_specs=[pl.BlockSpec((DEC_BATCH, A_GW), lambda i: (rb, 0)), full3, full3],
        out_shape=[jax.ShapeDtypeStruct((M_ROWS, A_GW), BF16),
                   jax.ShapeDtypeStruct((ng, A_GW, DEC_BATCH), F32),
                   jax.ShapeDtypeStruct((ng, A_GW, DEC_BATCH), F32)],
        scratch_shapes=[pltpu.VMEM((ng, A_GW, DEC_BATCH), F32), pltpu.VMEM((ng, 8, DEC_BATCH), F32),
                        pltpu.VMEM((ng, A_GW, DEC_BATCH), F32), pltpu.VMEM((ng, 8, DEC_BATCH), F32)],
        compiler_params=_cparams(("arbitrary",)),
        input_output_aliases={8: 0},
        name="attn_sample",
    )(proj, proj, proj, cos, sin, *caches_t, oa_dst)


def _pad_heads(wt, d, dp):
    c = wt.shape[1]
    wt = wt.reshape(M_HEADS, d, c)
    return jnp.pad(wt, ((0, 0), (0, dp - d), (0, 0))).reshape(M_HEADS * dp, c)


def _layer_weights(w_in_l, w_up_m, w_up_c, w_up_a, w_o_l):
    w_in_t = jnp.transpose(w_in_l)
    o = IN_OFFSETS
    piece = lambda i: w_in_t[o[i]:o[i + 1]]
    mq, mk, mv, mi, mf, mo, cb, cc, ch, aq, ak, av, gt = [piece(i) for i in range(13)]
    w_all = jnp.concatenate(
        [_pad_heads(mv, M_DV, DVP), _pad_heads(mo, M_DV, DVP),
         _pad_heads(mq, M_DK, DKP), _pad_heads(mk, M_DK, DKP),
         cb, cc, ch, aq, ak, av,
         mi, mf, jnp.zeros((PROJ_W - C_IF - 2 * M_HEADS, D_MODEL), F32), gt], axis=0).astype(BF16)
    w_um = jnp.pad(w_up_m.reshape(M_HEADS, M_DV, D_MODEL),
                   ((0, 0), (0, DVP - M_DV), (0, 0))).reshape(M_HEADS * DVP, D_MODEL)
    return (w_all, w_um.astype(BF16), w_up_c.astype(BF16), w_up_a.astype(BF16), w_o_l.astype(BF16))


def _rope_tables(pos):
    half = A_HEAD_DIM // 2
    inv = ROPE_THETA ** (-(2.0 * jnp.arange(half, dtype=F32)) / A_HEAD_DIM)
    ang = pos.astype(F32)[:, None] * inv[None, :]
    cos = jnp.cos(ang)
    sin = jnp.sin(ang)
    cos = jnp.tile(jnp.concatenate([cos, cos], axis=-1), (1, A_HPG))
    sin = jnp.tile(jnp.concatenate([-sin, sin], axis=-1), (1, A_HPG))
    return cos, sin


def kernel(x_prompt, x_sample, state_mlstm_C, state_mlstm_n, state_mlstm_m, state_conv,
           cache_attn_kv_w128, cache_attn_kv_w512, cache_attn_kv_w2048,
           w_in, b_gate_if, mlstm_norm_g, conv_w, w_up_mlstm, w_up_conv, w_up_attn, w_o,
           w_ffn_in, w_ffn_out, ln_g, ln_b):
    x = jnp.concatenate([x_prompt.reshape(M_PROMPT, D_MODEL),
                         x_sample.reshape(DEC_BATCH, D_MODEL)], axis=0)
    cos_p, sin_p = _rope_tables(jnp.arange(SEQ))
    cos_s, sin_s = _rope_tables(PAST_LEN + jnp.arange(1))

    c0t = jnp.transpose(state_mlstm_C, (0, 2, 3, 4, 1))
    n0t = jnp.transpose(state_mlstm_n, (0, 2, 3, 1))
    m0t = jnp.transpose(state_mlstm_m, (0, 2, 1)).reshape(DEPTH, M_HEADS, 1, DEC_BATCH)
    caches_t = [jnp.transpose(c, (0, 1, 3, 4, 5, 2))
                for c in (cache_attn_kv_w128, cache_attn_kv_w512, cache_attn_kv_w2048)]

    keeps = [min(win, SEQ) for win, _ in A_GROUPS]
    kv_all = [jnp.zeros((DEPTH, BATCH, 2, A_GW, kp), F32) for kp in keeps]
    sct_all = jnp.zeros((DEPTH, M_HEADS, M_DK, M_DV, DEC_BATCH), F32)
    hm = jnp.zeros((M_ROWS, M_HEADS * DVP), BF16)
    yc = jnp.zeros((M_ROWS, CONV_WIDTH), BF16)
    oa = jnp.zeros((M_ROWS, A_GW), BF16)

    p_states, s_states = [], []
    for l in range(DEPTH):
        w_all, w_um, w_uc, w_ua, w_ol = _layer_weights(
            w_in[l], w_up_mlstm[l], w_up_conv[l], w_up_attn[l], w_o[l])
        bias = jnp.pad(b_gate_if[l], (0, LANES - 2 * M_HEADS)).reshape(1, LANES)
        bias_col = jnp.broadcast_to(b_gate_if[l][:, None], (2 * M_HEADS, DEC_BATCH))
        gain = jnp.pad(mlstm_norm_g[l].reshape(M_HEADS, M_DV), ((0, 0), (0, DVP - M_DV)))
        gain_col = mlstm_norm_g[l].reshape(M_HEADS, M_DV, 1)

        x = ffn_ln(x, w_ffn_in, w_ffn_out, ln_g[l, 0], ln_b[l, 0], l, 0)
        proj = branch_proj(x, w_all)

        hm, pc, pn, pm = mlstm_prompt(proj, bias, gain.reshape(M_HEADS, 1, DVP), hm)
        hm, sct_all, snt, smt = mlstm_sample(proj, bias_col, gain_col, c0t, n0t, m0t, hm, sct_all, l)
        yc, pconv = conv_prompt(proj, conv_w[l], yc)
        yc, sconv = conv_sample(proj, state_conv[l], conv_w[l], yc)
        oa, *kv_all = attn_prompt(proj, cos_p[:, :LANES], sin_p[:, :LANES], oa, kv_all, l)
        oa, kt_s, vt_s = attn_sample(proj, cos_s, sin_s, caches_t, oa, l)

        x = merge_ln(x, hm, yc, oa, w_all, w_um, w_uc, w_ua, w_ol, ln_g[l, 1], ln_b[l, 1])
        x = ffn_ln(x, w_ffn_in, w_ffn_out, ln_g[l, 2], ln_b[l, 2], l, 1)

        kt_s = kt_s.reshape(3, A_HPG, A_HEAD_DIM, DEC_BATCH)
        vt_s = vt_s.reshape(3, A_HPG, A_HEAD_DIM, DEC_BATCH)
        kv_s = [jnp.stack([kt_s[gi], vt_s[gi]], axis=0) for gi in range(3)]
        p_states.append((pc, pn.reshape(BATCH, M_HEADS, M_DK), pm.reshape(BATCH, M_HEADS), pconv))
        s_states.append((snt, smt, sconv, kv_s[0], kv_s[1], kv_s[2]))

    y_prompt = x[:M_PROMPT].reshape(BATCH, SEQ, D_MODEL)
    y_sample = x[M_PROMPT:].reshape(DEC_BATCH, 1, D_MODEL)
    p_out = [jnp.stack(z) for z in zip(*p_states)]
    p_out += [jnp.transpose(kvt.reshape(DEPTH, BATCH, 2, A_HPG, A_HEAD_DIM, kvt.shape[-1]),
                            (0, 1, 5, 2, 3, 4))
              for kvt in kv_all]
    snt, smt, sconv, kv0, kv1, kv2 = [jnp.stack(z) for z in zip(*s_states)]
    s_out = [jnp.transpose(sct_all, (0, 4, 1, 2, 3)),
             jnp.transpose(snt, (0, 3, 1, 2)),
             jnp.transpose(smt.reshape(DEPTH, M_HEADS, DEC_BATCH), (0, 2, 1)),
             sconv]
    s_out += [jnp.transpose(kv, (0, 4, 1, 2, 3)).reshape(DEPTH, DEC_BATCH, 1, 2, A_HPG, A_HEAD_DIM)
              for kv in (kv0, kv1, kv2)]
    return (y_prompt, y_sample, *p_out, *s_out)
```

```python
import functools
import math

import jax
import jax.numpy as jnp
import numpy as np
from jax import lax
from jax.experimental import pallas as pl
from jax.experimental.pallas import tpu as pltpu

F32 = jnp.float32
BF16 = jnp.bfloat16

D_MODEL = 2048
BATCH = 4
SEQ = 2048
DEPTH = 2
DEC_BATCH = 128
PAST_LEN = 2048
M_HEADS = 4
M_DV = 192
M_DK = 96
M_QK = M_HEADS * M_DK
M_WIDTH = M_HEADS * M_DV
CONV_WIDTH = 512
CONV_K = 3
A_GROUPS = ((128, 1), (512, 4), (2048, 16))
A_HPG = 4
A_HEAD_DIM = 64
A_GW = A_HPG * A_HEAD_DIM
A_WIDTH = 3 * A_GW
ROPE_THETA = 10000.0
N_BRANCH = 3
D_FF = 5632
LN_EPS = 1e-5
ALPHA = (2 * DEPTH) ** 0.25
IN_SIZES = (M_QK, M_QK, M_WIDTH, M_HEADS, M_HEADS, M_WIDTH,
            CONV_WIDTH, CONV_WIDTH, CONV_WIDTH,
            A_WIDTH, A_WIDTH, A_WIDTH, N_BRANCH * D_MODEL)
IN_OFFSETS = tuple(int(o) for o in np.cumsum((0,) + IN_SIZES))

M_PROMPT = BATCH * SEQ
M_ROWS = M_PROMPT + DEC_BATCH

LANES = 128
DKP = 128
DVP = 256
VMEM_LIMIT = 52 * 1024 * 1024
VMEM_LIMIT_FFN = 56 * 1024 * 1024

C_MV, C_MO = 0, 1024
C_MQ, C_MK = 2048, 2560
C_CB, C_CC, C_CH = 3072, 3584, 4096
C_AQ, C_AK, C_AV = 4608, 5376, 6144
C_IF = 6912
PROJ_W = 7168

NEG = -1e30


def _sigmoid(x):
    return 1.0 / (1.0 + jnp.exp(-x))


def _layer_norm(z, g, b):
    mu = jnp.mean(z, axis=-1, keepdims=True)
    zc = z - mu
    var = jnp.mean(zc * zc, axis=-1, keepdims=True)
    return zc * lax.rsqrt(var + LN_EPS) * g + b


def _cparams(sem, vmem_limit=VMEM_LIMIT):
    return pltpu.CompilerParams(dimension_semantics=sem, vmem_limit_bytes=vmem_limit)


FFN_TM_BF16_OUT = 640
FFN_TM_F32_ONLY = 832
FFN_TF = 512


def _ffn_kernel(x_ref, wa_ref, wb_ref, wo_ref, g_ref, b_ref, y_ref, *rest):
    xb_scr = rest[-1]
    f = pl.program_id(1)

    @pl.when(f == 0)
    def _():
        xb_scr[...] = x_ref[...].astype(BF16)
        y_ref[...] = jnp.zeros_like(y_ref)

    xb = xb_scr[...]
    a = jnp.dot(xb, wa_ref[...], preferred_element_type=F32)
    b = jnp.dot(xb, wb_ref[...], preferred_element_type=F32)
    h = (a * _sigmoid(a)) * b
    y_ref[...] += jnp.dot(h.astype(BF16), wo_ref[...], preferred_element_type=F32)

    @pl.when(f == pl.num_programs(1) - 1)
    def _():
        z = ALPHA * x_ref[...] + 0.5 * y_ref[...]
        out = _layer_norm(z, g_ref[...], b_ref[...])
        y_ref[...] = out
        if len(rest) == 2:
            rest[0][...] = out.astype(BF16)


def ffn_ln(x, w_in, w_out, g, b, layer, which, emit_bf16):
    m = x.shape[0]
    nf = D_FF // FFN_TF
    tm = FFN_TM_BF16_OUT if emit_bf16 else FFN_TM_F32_ONLY
    rows = pl.BlockSpec((tm, D_MODEL), lambda i, f: (i, 0))
    return pl.pallas_call(
        _ffn_kernel,
        grid=(m // tm, nf),
        in_specs=[
            rows,
            pl.BlockSpec((None, None, D_MODEL, FFN_TF), lambda i, f: (layer, which, 0, f)),
            pl.BlockSpec((None, None, D_MODEL, FFN_TF), lambda i, f: (layer, which, 0, f + nf)),
            pl.BlockSpec((None, None, FFN_TF, D_MODEL), lambda i, f: (layer, which, f, 0)),
            pl.BlockSpec((1, D_MODEL), lambda i, f: (0, 0)),
            pl.BlockSpec((1, D_MODEL), lambda i, f: (0, 0)),
        ],
        out_specs=[rows, rows] if emit_bf16 else rows,
        out_shape=([jax.ShapeDtypeStruct((m, D_MODEL), F32), jax.ShapeDtypeStruct((m, D_MODEL), BF16)]
                   if emit_bf16 else jax.ShapeDtypeStruct((m, D_MODEL), F32)),
        scratch_shapes=[pltpu.VMEM((tm, D_MODEL), BF16)],
        compiler_params=_cparams(("parallel", "arbitrary"), VMEM_LIMIT_FFN),
        name="ffn_ln",
    )(x, w_in, w_in, w_out, g.reshape(1, D_MODEL), b.reshape(1, D_MODEL))


PROJ_TM = 1040
PROJ_TN = 1024


NT_DIMS = (((1,), (1,)), ((), ()))


def _proj_kernel(xb_ref, wt_ref, o_ref):
    o_ref[...] = lax.dot_general(xb_ref[...], wt_ref[...], NT_DIMS, preferred_element_type=F32)


def branch_proj(xb, wt):
    m = xb.shape[0]
    return pl.pallas_call(
        _proj_kernel,
        grid=(m // PROJ_TM, PROJ_W // PROJ_TN),
        in_specs=[pl.BlockSpec((PROJ_TM, D_MODEL), lambda i, j: (i, 0)),
                  pl.BlockSpec((PROJ_TN, D_MODEL), lambda i, j: (j, 0))],
        out_specs=pl.BlockSpec((PROJ_TM, PROJ_TN), lambda i, j: (i, j)),
        out_shape=jax.ShapeDtypeStruct((m, PROJ_W), F32),
        compiler_params=_cparams(("parallel", "parallel")),
        name="branch_proj",
    )(xb, wt)


MRG_TM = 640
MRG_TN = 512


def _gate_up_kernel(xb_ref, hm_ref, yc_ref, oa_ref, wg0_ref, wg1_ref, wg2_ref,
                    wum_ref, wuc_ref, wua_ref, o_ref):
    xb = xb_ref[...]

    def gated(wg_ref, br_ref, wu_ref):
        gate = _sigmoid(lax.dot_general(xb, wg_ref[...], NT_DIMS, preferred_element_type=F32))
        return gate * jnp.dot(br_ref[...], wu_ref[...], preferred_element_type=F32)

    merged = (gated(wg0_ref, hm_ref, wum_ref) + gated(wg1_ref, yc_ref, wuc_ref)
              + gated(wg2_ref, oa_ref, wua_ref))
    o_ref[...] = merged.astype(BF16)


def _out_ln_kernel(x_ref, mg_ref, wo_ref, g_ref, b_ref, y_ref):
    z = ALPHA * x_ref[...] + jnp.dot(mg_ref[...], wo_ref[...], preferred_element_type=F32)
    y_ref[...] = _layer_norm(z, g_ref[...], b_ref[...])


def merge_ln(x, xb, hm, yc, oa, w_all, wum, wuc, wua, wo, g, b):
    m = x.shape[0]
    nn = D_MODEL // MRG_TN
    g0 = PROJ_W // MRG_TN
    row = lambda w: pl.BlockSpec((MRG_TM, w), lambda i, n: (i, 0))
    merged = pl.pallas_call(
        _gate_up_kernel,
        grid=(m // MRG_TM, nn),
        in_specs=[
            row(D_MODEL), row(M_HEADS * DVP), row(CONV_WIDTH), row(A_GW),
            pl.BlockSpec((MRG_TN, D_MODEL), lambda i, n: (g0 + n, 0)),
            pl.BlockSpec((MRG_TN, D_MODEL), lambda i, n: (g0 + n + nn, 0)),
            pl.BlockSpec((MRG_TN, D_MODEL), lambda i, n: (g0 + n + 2 * nn, 0)),
            pl.BlockSpec((M_HEADS * DVP, MRG_TN), lambda i, n: (0, n)),
            pl.BlockSpec((CONV_WIDTH, MRG_TN), lambda i, n: (0, n)),
            pl.BlockSpec((A_GW, MRG_TN), lambda i, n: (0, n)),
        ],
        out_specs=pl.BlockSpec((MRG_TM, MRG_TN), lambda i, n: (i, n)),
        out_shape=jax.ShapeDtypeStruct((m, D_MODEL), BF16),
        compiler_params=_cparams(("parallel", "parallel")),
        name="gate_up",
    )(xb, hm, yc, oa, w_all, w_all, w_all, wum, wuc, wua)
    rows = pl.BlockSpec((MRG_TM, D_MODEL), lambda i: (i, 0))
    vec = pl.BlockSpec((1, D_MODEL), lambda i: (0, 0))
    return pl.pallas_call(
        _out_ln_kernel,
        grid=(m // MRG_TM,),
        in_specs=[rows, rows, pl.BlockSpec((D_MODEL, D_MODEL), lambda i: (0, 0)), vec, vec],
        out_specs=rows,
        out_shape=jax.ShapeDtypeStruct((m, D_MODEL), F32),
        compiler_params=_cparams(("parallel",)),
        name="out_ln",
    )(x, merged, wo, g.reshape(1, D_MODEL), b.reshape(1, D_MODEL))


def _conv_prompt_kernel(cb_ref, cc_ref, ch_ref, w_ref, y_dst_ref, y_ref, st_ref, u_scr):
    del y_dst_ref
    u = cc_ref[...] * ch_ref[...]
    u_scr[pl.ds(0, 8), :] = jnp.zeros((8, CONV_WIDTH), F32)
    u_scr[pl.ds(8, SEQ), :] = u
    w = w_ref[...]
    acc = (w[0:1, :] * u_scr[pl.ds(6, SEQ), :] + w[1:2, :] * u_scr[pl.ds(7, SEQ), :]
           + w[2:3, :] * u)
    y_ref[...] = (cb_ref[...] * acc).astype(BF16)
    st_ref[...] = u_scr[pl.ds(8 + SEQ - (CONV_K - 1), CONV_K - 1), :]


def conv_prompt(proj, conv_w, y_dst):
    blk = lambda c: pl.BlockSpec((SEQ, CONV_WIDTH), lambda b, c=c: (b, c // CONV_WIDTH))
    return pl.pallas_call(
        _conv_prompt_kernel,
        grid=(BATCH,),
        in_specs=[blk(C_CB), blk(C_CC), blk(C_CH),
                  pl.BlockSpec((CONV_K, CONV_WIDTH), lambda b: (0, 0)),
                  pl.BlockSpec(memory_space=pl.ANY)],
        out_specs=[pl.BlockSpec((SEQ, CONV_WIDTH), lambda b: (b, 0)),
                   pl.BlockSpec((None, CONV_K - 1, CONV_WIDTH), lambda b: (b, 0, 0))],
        out_shape=[jax.ShapeDtypeStruct((M_ROWS, CONV_WIDTH), BF16),
                   jax.ShapeDtypeStruct((BATCH, CONV_K - 1, CONV_WIDTH), F32)],
        input_output_aliases={4: 0},
        scratch_shapes=[pltpu.VMEM((SEQ + 8, CONV_WIDTH), F32)],
        compiler_params=_cparams(("parallel",)),
        name="conv_prompt",
    )(proj, proj, proj, conv_w, y_dst)


def _conv_sample_kernel(cb_ref, cc_ref, ch_ref, prev_ref, w_ref, y_dst_ref, y_ref, st_ref):
    del y_dst_ref
    u = cc_ref[...] * ch_ref[...]
    w = w_ref[...]
    p0 = prev_ref[:, 0, :]
    p1 = prev_ref[:, 1, :]
    acc = w[0:1, :] * p0 + w[1:2, :] * p1 + w[2:3, :] * u
    y_ref[...] = (cb_ref[...] * acc).astype(BF16)
    st_ref[:, 0, :] = p1
    st_ref[:, 1, :] = u


def conv_sample(proj, prev, conv_w, y_dst):
    rb = M_PROMPT // DEC_BATCH
    blk = lambda c: pl.BlockSpec((DEC_BATCH, CONV_WIDTH), lambda i, c=c: (rb, c // CONV_WIDTH))
    full3 = pl.BlockSpec((DEC_BATCH, CONV_K - 1, CONV_WIDTH), lambda i: (0, 0, 0))
    return pl.pallas_call(
        _conv_sample_kernel,
        grid=(1,),
        in_specs=[blk(C_CB), blk(C_CC), blk(C_CH), full3,
                  pl.BlockSpec((CONV_K, CONV_WIDTH), lambda i: (0, 0)),
                  pl.BlockSpec(memory_space=pl.ANY)],
        out_specs=[pl.BlockSpec((DEC_BATCH, CONV_WIDTH), lambda i: (rb, 0)), full3],
        out_shape=[jax.ShapeDtypeStruct((M_ROWS, CONV_WIDTH), BF16),
                   jax.ShapeDtypeStruct((DEC_BATCH, CONV_K - 1, CONV_WIDTH), F32)],
        input_output_aliases={5: 0},
        compiler_params=_cparams(("arbitrary",)),
        name="conv_sample",
    )(proj, proj, proj, prev, conv_w, y_dst)


M_L = 128


def _log_sigmoid(x):
    return jnp.minimum(x, 0.0) - jnp.log1p(jnp.exp(-jnp.abs(x)))


def _head_norm_gate(h, o_pre, gain):
    lane = lax.broadcasted_iota(jnp.int32, h.shape, 1)
    real = lane < M_DV
    mu = jnp.sum(h, axis=-1, keepdims=True) * (1.0 / M_DV)
    hc = jnp.where(real, h - mu, 0.0)
    var = jnp.sum(hc * hc, axis=-1, keepdims=True) * (1.0 / M_DV)
    return _sigmoid(o_pre) * (hc * lax.rsqrt(var + LN_EPS) * gain)


M_TS = 512


M_NROW = M_DV


def _mlstm_prompt_kernel(q_ref, k_ref, v_ref, o_ref, if_ref, bias_ref, gain_ref, hm_dst_ref,
                         hm_ref, c_out_ref, n_out_ref, m_out_ref, ct_scr, m_scr):
    del hm_dst_ref
    step = pl.program_id(1)

    @pl.when(step == 0)
    def _():
        ct_scr[...] = jnp.zeros_like(ct_scr)
        m_scr[...] = jnp.zeros_like(m_scr)

    row = lax.broadcasted_iota(jnp.int32, (M_L, M_L), 0)
    col = lax.broadcasted_iota(jnp.int32, (M_L, M_L), 1)
    causal_t = row <= col
    tri = (col <= row).astype(F32)
    bias = bias_ref[...]
    ones_lane = lax.broadcasted_iota(jnp.int32, (M_L, DVP), 1) == M_NROW
    real_rows = lax.broadcasted_iota(jnp.int32, (DVP, M_L), 0) < M_DV
    tn = (((0,), (0,)), ((), ()))

    def chunk(c, carry):
        r0 = pl.multiple_of(c * M_L, M_L)
        x_if = if_ref[pl.ds(r0, M_L), :] + bias
        log_f = _log_sigmoid(x_if)
        cs = jnp.dot(tri, log_f, preferred_element_type=F32, precision=lax.Precision.HIGHEST)
        zt = jnp.where(col < M_HEADS, x_if, cs).T
        for hd in range(M_HEADS):
            b_row = zt[M_HEADS + hd:M_HEADS + hd + 1, :]
            c_col = x_if[:, hd:hd + 1] - cs[:, M_HEADS + hd:M_HEADS + hd + 1]
            c_rep = jnp.broadcast_to(c_col, (M_L, M_L))
            m_prev = m_scr[hd]
            b_last = b_row[:, M_L - 1:M_L]

            d_t = jnp.where(causal_t, b_row + c_rep, NEG)
            inter = b_row + m_prev
            m_t = jnp.maximum(jnp.max(d_t, axis=0, keepdims=True), inter)
            q = q_ref[pl.ds(r0, M_L), pl.ds(hd * DKP, DKP)].astype(BF16)
            k = k_ref[pl.ds(r0, M_L), pl.ds(hd * DKP, DKP)] * (M_DK ** -0.5)
            v1 = jnp.where(ones_lane, 1.0, v_ref[pl.ds(r0, M_L), pl.ds(hd * DVP, DVP)]).astype(BF16)
            s_t = lax.dot_general(k.astype(BF16), q, NT_DIMS, preferred_element_type=F32) * jnp.exp(d_t - m_t)
            w_inter = jnp.exp(inter - m_t)
            ct_prev = ct_scr[hd]
            num_t = (lax.dot_general(v1, s_t.astype(BF16), tn, preferred_element_type=F32)
                     + w_inter * lax.dot_general(ct_prev.astype(BF16), q, NT_DIMS,
                                                 preferred_element_type=F32))
            den = num_t[M_NROW:M_NROW + 1, :]
            h_t = jnp.where(real_rows, num_t / jnp.maximum(jnp.abs(den), jnp.exp(-m_t)), 0.0)
            mu = jnp.sum(h_t, axis=0, keepdims=True) * (1.0 / M_DV)
            hc = jnp.where(real_rows, h_t - mu, 0.0)
            var = jnp.sum(hc * hc, axis=0, keepdims=True) * (1.0 / M_DV)
            hn = (hc * lax.rsqrt(var + LN_EPS)).T * gain_ref[hd]
            o_pre = o_ref[pl.ds(r0, M_L), pl.ds(hd * DVP, DVP)]
            hm_ref[pl.ds(r0, M_L), pl.ds(hd * DVP, DVP)] = (_sigmoid(o_pre) * hn).astype(BF16)

            m_new = jnp.maximum(b_last + m_prev, b_last + jnp.max(c_rep, axis=0, keepdims=True)[:, 0:1])
            kw = k * jnp.exp(c_rep + (b_last - m_new))
            ct_scr[hd] = (jnp.exp(b_last + m_prev - m_new) * ct_prev
                          + lax.dot_general(v1, kw.astype(BF16), tn, preferred_element_type=F32))
            m_scr[hd] = m_new
        return carry

    lax.fori_loop(0, M_TS // M_L, chunk, 0)

    @pl.when(step == pl.num_programs(1) - 1)
    def _():
        for hd in range(M_HEADS):
            ct = ct_scr[hd]
            c_out_ref[hd] = ct.T[0:M_DK, 0:M_DV]
            n_out_ref[hd] = ct[M_NROW:M_NROW + 1, 0:M_DK]
            m_out_ref[hd] = m_scr[hd]


def mlstm_prompt(proj, bias, gain, hm_dst):
    ns = SEQ // M_TS
    def cblk(c0, w):
        return pl.BlockSpec((M_TS, w), lambda b, s: (b * ns + s, c0 // w))
    return pl.pallas_call(
        _mlstm_prompt_kernel,
        grid=(BATCH, ns),
        in_specs=[cblk(C_MQ, M_HEADS * DKP), cblk(C_MK, M_HEADS * DKP),
                  cblk(C_MV, M_HEADS * DVP), cblk(C_MO, M_HEADS * DVP),
                  cblk(C_IF, LANES),
                  pl.BlockSpec((1, LANES), lambda b, s: (0, 0)),
                  pl.BlockSpec((M_HEADS, 1, DVP), lambda b, s: (0, 0, 0)),
                  pl.BlockSpec(memory_space=pl.ANY)],
        out_specs=[pl.BlockSpec((M_TS, M_HEADS * DVP), lambda b, s: (b * ns + s, 0)),
                   pl.BlockSpec((None, M_HEADS, M_DK, M_DV), lambda b, s: (b, 0, 0, 0)),
                   pl.BlockSpec((None, M_HEADS, 1, M_DK), lambda b, s: (b, 0, 0, 0)),
                   pl.BlockSpec((None, M_HEADS, 1, 1), lambda b, s: (b, 0, 0, 0))],
        out_shape=[jax.ShapeDtypeStruct((M_ROWS, M_HEADS * DVP), BF16),
                   jax.ShapeDtypeStruct((BATCH, M_HEADS, M_DK, M_DV), F32),
                   jax.ShapeDtypeStruct((BATCH, M_HEADS, 1, M_DK), F32),
                   jax.ShapeDtypeStruct((BATCH, M_HEADS, 1, 1), F32)],
        scratch_shapes=[pltpu.VMEM((M_HEADS, DVP, DKP), F32), pltpu.VMEM((M_HEADS, 1, 1), F32)],
        compiler_params=_cparams(("parallel", "arbitrary")),
        input_output_aliases={7: 0},
        name="mlstm_prompt",
    )(proj, proj, proj, proj, proj, bias, gain, hm_dst)


MS_DC = 48


def _pick_row(x8, j):
    rows = lax.broadcasted_iota(jnp.int32, x8.shape, 0)
    return jnp.sum(jnp.where(rows == j, x8, 0.0), axis=0, keepdims=True)


def _mlstm_sample_kernel(q_ref, k_ref, v_ref, o_ref, if_ref, bias_ref, gain_ref,
                         c0_ref, n0_ref, m0_ref, hm_dst_ref, c_dst_ref,
                         hm_ref, c_out_ref, n_out_ref, m_out_ref,
                         qt_scr, kw_scr, vt_scr, acc_scr, st_scr):
    del hm_dst_ref, c_dst_ref
    hd = pl.program_id(0)
    c = pl.program_id(1)

    @pl.when(c == 0)
    def _():
        qt = q_ref[...].T
        kt = (k_ref[...] * (M_DK ** -0.5)).T
        vt_scr[...] = v_ref[...].T
        x_if = if_ref[...].T[0:2 * M_HEADS, :] + bias_ref[...]
        i_pre = _pick_row(x_if, hd)
        log_f = _log_sigmoid(_pick_row(x_if, hd + M_HEADS))
        inter = log_f + m0_ref[...]
        m_new = jnp.maximum(i_pre, inter)
        w_k = jnp.exp(i_pre - m_new)
        decay = jnp.exp(inter - m_new)
        n_prev = n0_ref[...]
        s = jnp.sum(qt * kt, axis=0, keepdims=True) * w_k
        den = s + decay * jnp.sum(qt[:M_DK] * n_prev, axis=0, keepdims=True)
        kw = kt * w_k
        qt_scr[...] = qt
        kw_scr[...] = kw
        n_out_ref[...] = decay * n_prev + kw[:M_DK]
        m_out_ref[...] = m_new
        st_scr[0:1, :] = s
        st_scr[1:2, :] = decay
        st_scr[2:3, :] = den
        st_scr[3:4, :] = m_new
        acc_scr[...] = jnp.zeros_like(acc_scr)

    decay = st_scr[1:2, :]
    vt = vt_scr[pl.ds(0, M_DV), :]

    def tile(t, acc):
        r8 = pl.multiple_of(c * MS_DC + t * 8, 8)
        q8 = qt_scr[pl.ds(r8, 8), :]
        kw8 = kw_scr[pl.ds(r8, 8), :]
        for r in range(8):
            c_row = c0_ref[t * 8 + r]
            c_out_ref[t * 8 + r] = decay * c_row + kw8[r:r + 1, :] * vt
            acc = acc + q8[r:r + 1, :] * c_row
        return acc

    acc = lax.fori_loop(0, MS_DC // 8, tile, acc_scr[...])
    acc_scr[...] = acc

    @pl.when(c == pl.num_programs(1) - 1)
    def _():
        s = st_scr[0:1, :]
        den = st_scr[2:3, :]
        m_t = st_scr[3:4, :]
        h = (s * vt + decay * acc) / jnp.maximum(jnp.abs(den), jnp.exp(-m_t))
        mu = jnp.mean(h, axis=0, keepdims=True)
        hc = h - mu
        var = jnp.mean(hc * hc, axis=0, keepdims=True)
        o_pre = o_ref[...].T[:M_DV, :]
        out = _sigmoid(o_pre) * (hc * lax.rsqrt(var + LN_EPS) * gain_ref[...])
        out = jnp.concatenate([out, jnp.zeros((DVP - M_DV, DEC_BATCH), F32)], axis=0)
        hm_ref[...] = out.T.astype(BF16)


def mlstm_sample(proj, bias_col, gain_col, c0t, n0t, m0t, hm_dst, c_dst, layer):
    any_spec = pl.BlockSpec(memory_space=pl.ANY)
    rb = M_PROMPT // DEC_BATCH
    nc = M_DK // MS_DC
    def cblk(c0_, w):
        return pl.BlockSpec((DEC_BATCH, w), lambda h, c: (rb, c0_ // w + h))
    return pl.pallas_call(
        _mlstm_sample_kernel,
        grid=(M_HEADS, nc),
        in_specs=[cblk(C_MQ, DKP), cblk(C_MK, DKP), cblk(C_MV, DVP), cblk(C_MO, DVP),
                  pl.BlockSpec((DEC_BATCH, LANES), lambda h, c: (rb, C_IF // LANES)),
                  pl.BlockSpec((2 * M_HEADS, DEC_BATCH), lambda h, c: (0, 0)),
                  pl.BlockSpec((None, M_DV, 1), lambda h, c: (h, 0, 0)),
                  pl.BlockSpec((None, None, MS_DC, M_DV, DEC_BATCH), lambda h, c: (layer, h, c, 0, 0)),
                  pl.BlockSpec((None, None, M_DK, DEC_BATCH), lambda h, c: (layer, h, 0, 0)),
                  pl.BlockSpec((None, None, 1, DEC_BATCH), lambda h, c: (layer, h, 0, 0)),
                  any_spec, any_spec],
        out_specs=[pl.BlockSpec((DEC_BATCH, DVP), lambda h, c: (rb, h)),
                   pl.BlockSpec((None, None, MS_DC, M_DV, DEC_BATCH), lambda h, c: (layer, h, c, 0, 0)),
                   pl.BlockSpec((None, M_DK, DEC_BATCH), lambda h, c: (h, 0, 0)),
                   pl.BlockSpec((None, 1, DEC_BATCH), lambda h, c: (h, 0, 0))],
        out_shape=[jax.ShapeDtypeStruct((M_ROWS, M_HEADS * DVP), BF16),
                   jax.ShapeDtypeStruct((DEPTH, M_HEADS, M_DK, M_DV, DEC_BATCH), F32),
                   jax.ShapeDtypeStruct((M_HEADS, M_DK, DEC_BATCH), F32),
                   jax.ShapeDtypeStruct((M_HEADS, 1, DEC_BATCH), F32)],
        scratch_shapes=[pltpu.VMEM((DKP, DEC_BATCH), F32), pltpu.VMEM((DKP, DEC_BATCH), F32),
                        pltpu.VMEM((DVP, DEC_BATCH), F32), pltpu.VMEM((M_DV, DEC_BATCH), F32),
                        pltpu.VMEM((8, DEC_BATCH), F32)],
        input_output_aliases={10: 0, 11: 1},
        compiler_params=_cparams(("arbitrary", "arbitrary")),
        name="mlstm_sample",
    )(proj, proj, proj, proj, proj, bias_col, gain_col, c0t, n0t, m0t, hm_dst, c_dst)


A_Q = 128
A_LT = A_GW // LANES


def _rope(x, cos, sin_signed):
    lane = lax.broadcasted_iota(jnp.int32, x.shape, 1)
    first_half = (lane % A_HEAD_DIM) < (A_HEAD_DIM // 2)
    partner = jnp.where(first_half, pltpu.roll(x, x.shape[1] - A_HEAD_DIM // 2, 1),
                        pltpu.roll(x, A_HEAD_DIM // 2, 1))
    return x * cos + partner * sin_signed


def _head_masks(shape):
    lane = lax.broadcasted_iota(jnp.int32, shape, 1)
    return [(lane // A_HEAD_DIM) == h for h in range(A_HPG)]


def _attn_group_prompt(dil, gi, qs_scr, ks_scr, vs_scr, o_scr, l_scr):
    length = SEQ // dil
    nb = length // A_Q
    row = lax.broadcasted_iota(jnp.int32, (A_Q, A_Q), 0)
    col = lax.broadcasted_iota(jnp.int32, (A_Q, A_Q), 1)
    cur_ok = col <= row
    prev_ok = col >= row
    masks = _head_masks((A_Q, A_GW))
    nt = (((1,), (1,)), ((), ()))

    def window(start):
        if dil == 1:
            return pl.ds(pl.multiple_of(start, A_Q), A_Q)
        return pl.ds(start, A_Q, stride=dil)

    def rows(scr, start):
        w = window(start)
        return jnp.concatenate([scr[t, w, :] for t in range(A_LT)], axis=1).astype(BF16)

    def block(idx, carry):
        r = idx % dil
        n = idx // dil
        base = r + (dil * A_Q) * n
        qb = rows(qs_scr, base)
        kc = rows(ks_scr, base)
        vc = rows(vs_scr, base)
        if nb > 1:
            pbase = jnp.maximum(base - dil * A_Q, r)
            kp = rows(ks_scr, pbase)
            vp = rows(vs_scr, pbase)
            has_prev = n > 0
        o_acc = jnp.zeros((A_Q, A_GW), F32)
        l_acc = jnp.zeros((A_Q, A_GW), F32)
        for h in range(A_HPG):
            qh = jnp.where(masks[h], qb, jnp.zeros_like(qb))
            s_c = jnp.where(cur_ok, lax.dot_general(qh, kc, nt, preferred_element_type=F32), NEG)
            m = jnp.max(s_c, axis=1, keepdims=True)
            if nb > 1:
                s_p = jnp.where(jnp.logical_and(prev_ok, has_prev),
                                lax.dot_general(qh, kp, nt, preferred_element_type=F32), NEG)
                m = jnp.maximum(m, jnp.max(s_p, axis=1, keepdims=True))
            p_c = jnp.exp(s_c - m)
            l = jnp.sum(p_c, axis=1, keepdims=True)
            o_h = jnp.dot(p_c.astype(BF16), vc, preferred_element_type=F32)
            if nb > 1:
                p_p = jnp.exp(s_p - m)
                l = l + jnp.sum(p_p, axis=1, keepdims=True)
                o_h = o_h + jnp.dot(p_p.astype(BF16), vp, preferred_element_type=F32)
            o_acc = o_acc + jnp.where(masks[h], o_h / l, 0.0)
            l_acc = l_acc + jnp.where(masks[h], m + jnp.log(l), 0.0)
        w = window(base)
        for t in range(A_LT):
            o_scr[gi, t, w, :] = o_acc[:, t * LANES:(t + 1) * LANES]
            l_scr[gi, t, w, :] = l_acc[:, t * LANES:(t + 1) * LANES]
        return carry

    lax.fori_loop(0, dil * nb, block, 0)


def _attn_prompt_kernel(q_ref, k_ref, v_ref, cos_ref, sin_ref, oa_dst_ref, kvd0_ref, kvd1_ref, kvd2_ref,
                        oa_ref, kv0_ref, kv1_ref, kv2_ref, qs_scr, ks_scr, vs_scr, o_scr, l_scr):
    del oa_dst_ref, kvd0_ref, kvd1_ref, kvd2_ref
    g = pl.program_id(1)
    rc = 256

    def rope_rows(c, carry):
        sl = pl.ds(pl.multiple_of(c * rc, rc), rc)
        cos = cos_ref[sl, :]
        sin = sin_ref[sl, :]
        for t in range(A_LT):
            lanes = pl.ds(t * LANES, LANES)
            qs_scr[t, sl, :] = _rope(q_ref[sl, lanes], cos, sin) * (A_HEAD_DIM ** -0.5)
            ks_scr[t, sl, :] = _rope(k_ref[sl, lanes], cos, sin)
            vs_scr[t, sl, :] = v_ref[sl, lanes]
        return carry

    lax.fori_loop(0, SEQ // rc, rope_rows, 0)

    for gi, ((win, dil), kv_ref) in enumerate(zip(A_GROUPS, (kv0_ref, kv1_ref, kv2_ref))):
        @pl.when(g == gi)
        def _(gi=gi, dil=dil, win=win, kv_ref=kv_ref):
            _attn_group_prompt(dil, gi, qs_scr, ks_scr, vs_scr, o_scr, l_scr)
            keep = min(win, SEQ)
            for c in range(keep // LANES):
                rows = pl.ds(SEQ - keep + c * LANES, LANES)
                for t in range(A_LT):
                    kv_ref[0, pl.ds(t * LANES, LANES), pl.ds(c * LANES, LANES)] = ks_scr[t, rows, :].T
                    kv_ref[1, pl.ds(t * LANES, LANES), pl.ds(c * LANES, LANES)] = vs_scr[t, rows, :].T

    @pl.when(g == len(A_GROUPS) - 1)
    def _():
        def comb(c, carry):
            sl = pl.ds(pl.multiple_of(c * rc, rc), rc)
            for t in range(A_LT):
                l0, l1, l2 = l_scr[0, t, sl, :], l_scr[1, t, sl, :], l_scr[2, t, sl, :]
                mx = jnp.maximum(jnp.maximum(l0, l1), l2)
                e0, e1, e2 = jnp.exp(l0 - mx), jnp.exp(l1 - mx), jnp.exp(l2 - mx)
                tot = e0 * o_scr[0, t, sl, :] + e1 * o_scr[1, t, sl, :] + e2 * o_scr[2, t, sl, :]
                oa_ref[sl, pl.ds(t * LANES, LANES)] = (tot / (e0 + e1 + e2)).astype(BF16)
            return carry
        lax.fori_loop(0, SEQ // rc, comb, 0)


def attn_prompt(proj, cos, sin, oa_dst, kv_dst, layer):
    def gblk(c0):
        return pl.BlockSpec((SEQ, A_GW), lambda b, g: (b, c0 // A_GW + g))
    tab = pl.BlockSpec((SEQ, LANES), lambda b, g: (0, 0))
    keeps = [min(win, SEQ) for win, _ in A_GROUPS]
    return pl.pallas_call(
        _attn_prompt_kernel,
        grid=(BATCH, len(A_GROUPS)),
        in_specs=[gblk(C_AQ), gblk(C_AK), gblk(C_AV), tab, tab] + [pl.BlockSpec(memory_space=pl.ANY)] * 4,
        out_specs=[pl.BlockSpec((SEQ, A_GW), lambda b, g: (b, 0))]
                  + [pl.BlockSpec((None, None, 2, A_GW, kp), lambda b, g: (layer, b, 0, 0, 0)) for kp in keeps],
        out_shape=[jax.ShapeDtypeStruct((M_ROWS, A_GW), BF16)]
                  + [jax.ShapeDtypeStruct((DEPTH, BATCH, 2, A_GW, kp), F32) for kp in keeps],
        input_output_aliases={5: 0, 6: 1, 7: 2, 8: 3},
        scratch_shapes=[pltpu.VMEM((A_LT, SEQ, LANES), F32),
                        pltpu.VMEM((A_LT, SEQ, LANES), F32),
                        pltpu.VMEM((A_LT, SEQ, LANES), F32),
                        pltpu.VMEM((len(A_GROUPS), A_LT, SEQ, LANES), F32),
                        pltpu.VMEM((len(A_GROUPS), A_LT, SEQ, LANES), F32)],
        compiler_params=_cparams(("parallel", "arbitrary")),
        name="attn_prompt",
    )(proj, proj, proj, cos, sin, oa_dst, *kv_dst)


AS_BB = 2


def _attn_sample_kernel(q_ref, k_ref, v_ref, cos_ref, sin_ref, c0_ref, c1_ref, c2_ref, oa_dst_ref,
                        oa_ref, kt_ref, vt_ref, qt_scr, s0_scr, ot_scr, lt_scr):
    del oa_dst_ref
    i = pl.program_id(0)
    lane_b = lax.broadcasted_iota(jnp.int32, (1, DEC_BATCH), 1)
    sub8 = lax.broadcasted_iota(jnp.int32, (8, DEC_BATCH), 0)

    @pl.when(i == 0)
    def _():
        cos = cos_ref[...]
        sin = sin_ref[...]
        for gi in range(len(A_GROUPS)):
            gs = pl.ds(gi * A_GW, A_GW)
            qt = (_rope(q_ref[:, gs], cos, sin) * (A_HEAD_DIM ** -0.5)).T
            kt = _rope(k_ref[:, gs], cos, sin).T
            qt_scr[gi] = qt
            kt_ref[gi] = kt
            vt_ref[gi] = v_ref[:, gs].T
            prod = qt * kt
            s0 = jnp.zeros((8, DEC_BATCH), F32)
            for h in range(A_HPG):
                part = jnp.sum(prod[h * A_HEAD_DIM:(h + 1) * A_HEAD_DIM], axis=0, keepdims=True)
                s0 = jnp.where(sub8 == h, part, s0)
            s0_scr[gi] = s0
        ot_scr[...] = jnp.zeros_like(ot_scr)
        lt_scr[...] = jnp.zeros_like(lt_scr)

    for bl in range(AS_BB):
        pick = lane_b == i * AS_BB + bl
        for gi, (cache_ref, (_, dil)) in enumerate(zip((c0_ref, c1_ref, c2_ref), A_GROUPS)):
            wb = cache_ref.shape[-1]
            if dil > 1:
                keep = (lax.broadcasted_iota(jnp.int32, (1, wb), 1) & (dil - 1)) == 0
            head_row = lax.broadcasted_iota(jnp.int32, (8, wb), 0)
            q_col = jnp.sum(jnp.where(pick, qt_scr[gi], 0.0), axis=1, keepdims=True)
            v_col = jnp.sum(jnp.where(pick, vt_ref[gi], 0.0), axis=1, keepdims=True)
            s0 = jnp.sum(jnp.where(pick, s0_scr[gi], 0.0), axis=1, keepdims=True)
            s = jnp.zeros((8, wb), F32)
            for h in range(A_HPG):
                hs = slice(h * A_HEAD_DIM, (h + 1) * A_HEAD_DIM)
                part = jnp.sum(q_col[hs] * cache_ref[bl, 0, h], axis=0, keepdims=True)
                s = jnp.where(head_row == h, part, s)
            if dil > 1:
                s = jnp.where(keep, s, NEG)
            m = jnp.maximum(jnp.max(s, axis=1, keepdims=True), s0)
            p = jnp.exp(s - m)
            p0 = jnp.exp(s0 - m)
            l = jnp.sum(p, axis=1, keepdims=True) + p0
            lse = m + jnp.log(l)
            o_parts = []
            for h in range(A_HPG):
                hs = slice(h * A_HEAD_DIM, (h + 1) * A_HEAD_DIM)
                pv = jnp.sum(p[h:h + 1, :] * cache_ref[bl, 1, h], axis=1, keepdims=True)
                o_parts.append((pv + p0[h:h + 1, :] * v_col[hs]) / l[h:h + 1, :])
            o = jnp.concatenate(o_parts, axis=0)
            ot_scr[gi] = jnp.where(pick, o, ot_scr[gi])
            lt_scr[gi] = jnp.where(pick, lse, lt_scr[gi])

    @pl.when(i == pl.num_programs(0) - 1)
    def _():
        for h in range(A_HPG):
            hs = pl.ds(h * A_HEAD_DIM, A_HEAD_DIM)
            l0, l1, l2 = (lt_scr[gi, pl.ds(h, 1), :] for gi in range(3))
            mx = jnp.maximum(jnp.maximum(l0, l1), l2)
            e0, e1, e2 = jnp.exp(l0 - mx), jnp.exp(l1 - mx), jnp.exp(l2 - mx)
            tot = e0 * ot_scr[0, hs, :] + e1 * ot_scr[1, hs, :] + e2 * ot_scr[2, hs, :]
            ot_scr[0, hs, :] = tot / (e0 + e1 + e2)
        oa_ref[...] = ot_scr[0].T.astype(BF16)


def attn_sample(proj, cos, sin, caches_t, oa_dst, layer):
    rb = M_PROMPT // DEC_BATCH
    def pblk(c0):
        return pl.BlockSpec((DEC_BATCH, A_WIDTH), lambda i: (rb, c0 // A_WIDTH))
    tab = pl.BlockSpec((1, A_GW), lambda i: (0, 0))
    cache_specs = [pl.BlockSpec((None, AS_BB, 2, A_HPG, A_HEAD_DIM, ct.shape[-1]),
                                lambda i: (layer, i, 0, 0, 0, 0)) for ct in caches_t]
    ng = len(A_GROUPS)
    full3 = pl.BlockSpec((ng, A_GW, DEC_BATCH), lambda i: (0, 0, 0))
    return pl.pallas_call(
        _attn_sample_kernel,
        grid=(DEC_BATCH // AS_BB,),
        in_specs=[pblk(C_AQ), pblk(C_AK), pblk(C_AV), tab, tab] + cache_specs
                 + [pl.BlockSpec(memory_space=pl.ANY)],
        out_specs=[pl.BlockSpec((DEC_BATCH, A_GW), lambda i: (rb, 0)), full3, full3],
        out_shape=[jax.ShapeDtypeStruct((M_ROWS, A_GW), BF16),
                   jax.ShapeDtypeStruct((ng, A_GW, DEC_BATCH), F32),
                   jax.ShapeDtypeStruct((ng, A_GW, DEC_BATCH), F32)],
        scratch_shapes=[pltpu.VMEM((ng, A_GW, DEC_BATCH), F32), pltpu.VMEM((ng, 8, DEC_BATCH), F32),
                        pltpu.VMEM((ng, A_GW, DEC_BATCH), F32), pltpu.VMEM((ng, 8, DEC_BATCH), F32)],
        compiler_params=_cparams(("arbitrary",)),
        input_output_aliases={8: 0},
        name="attn_sample",
    )(proj, proj, proj, cos, sin, *caches_t, oa_dst)


def _pad_heads(wt, d, dp):
    c = wt.shape[1]
    wt = wt.reshape(M_HEADS, d, c)
    return jnp.pad(wt, ((0, 0), (0, dp - d), (0, 0))).reshape(M_HEADS * dp, c)


def _layer_weights(w_in_l, w_up_m, w_up_c, w_up_a, w_o_l):
    w_in_t = jnp.transpose(w_in_l)
    o = IN_OFFSETS
    piece = lambda i: w_in_t[o[i]:o[i + 1]]
    mq, mk, mv, mi, mf, mo, cb, cc, ch, aq, ak, av, gt = [piece(i) for i in range(13)]
    w_all = jnp.concatenate(
        [_pad_heads(mv, M_DV, DVP), _pad_heads(mo, M_DV, DVP),
         _pad_heads(mq, M_DK, DKP), _pad_heads(mk, M_DK, DKP),
         cb, cc, ch, aq, ak, av,
         mi, mf, jnp.zeros((PROJ_W - C_IF - 2 * M_HEADS, D_MODEL), F32), gt], axis=0).astype(BF16)
    w_um = jnp.pad(w_up_m.reshape(M_HEADS, M_DV, D_MODEL),
                   ((0, 0), (0, DVP - M_DV), (0, 0))).reshape(M_HEADS * DVP, D_MODEL)
    return (w_all, w_um.astype(BF16), w_up_c.astype(BF16), w_up_a.astype(BF16), w_o_l.astype(BF16))


def _rope_tables(pos):
    half = A_HEAD_DIM // 2
    inv = ROPE_THETA ** (-(2.0 * jnp.arange(half, dtype=F32)) / A_HEAD_DIM)
    ang = pos.astype(F32)[:, None] * inv[None, :]
    cos = jnp.cos(ang)
    sin = jnp.sin(ang)
    cos = jnp.tile(jnp.concatenate([cos, cos], axis=-1), (1, A_HPG))
    sin = jnp.tile(jnp.concatenate([-sin, sin], axis=-1), (1, A_HPG))
    return cos, sin


def kernel(x_prompt, x_sample, state_mlstm_C, state_mlstm_n, state_mlstm_m, state_conv,
           cache_attn_kv_w128, cache_attn_kv_w512, cache_attn_kv_w2048,
           w_in, b_gate_if, mlstm_norm_g, conv_w, w_up_mlstm, w_up_conv, w_up_attn, w_o,
           w_ffn_in, w_ffn_out, ln_g, ln_b):
    x = jnp.concatenate([x_prompt.reshape(M_PROMPT, D_MODEL),
                         x_sample.reshape(DEC_BATCH, D_MODEL)], axis=0)
    cos_p, sin_p = _rope_tables(jnp.arange(SEQ))
    cos_s, sin_s = _rope_tables(PAST_LEN + jnp.arange(1))
    w_ffn_in_b = w_ffn_in.astype(BF16)
    w_ffn_out_b = w_ffn_out.astype(BF16)
    c0t = jnp.transpose(state_mlstm_C, (0, 2, 3, 4, 1))
    n0t = jnp.transpose(state_mlstm_n, (0, 2, 3, 1))
    m0t = jnp.transpose(state_mlstm_m, (0, 2, 1)).reshape(DEPTH, M_HEADS, 1, DEC_BATCH)
    caches_t = [jnp.transpose(c, (0, 1, 3, 4, 5, 2))
                for c in (cache_attn_kv_w128, cache_attn_kv_w512, cache_attn_kv_w2048)]

    keeps = [min(win, SEQ) for win, _ in A_GROUPS]
    kv_all = [jnp.zeros((DEPTH, BATCH, 2, A_GW, kp), F32) for kp in keeps]
    sct_all = jnp.zeros((DEPTH, M_HEADS, M_DK, M_DV, DEC_BATCH), F32)
    hm = jnp.zeros((M_ROWS, M_HEADS * DVP), BF16)
    yc = jnp.zeros((M_ROWS, CONV_WIDTH), BF16)
    oa = jnp.zeros((M_ROWS, A_GW), BF16)

    p_states, s_states = [], []
    for l in range(DEPTH):
        w_all, w_um, w_uc, w_ua, w_ol = _layer_weights(
            w_in[l], w_up_mlstm[l], w_up_conv[l], w_up_attn[l], w_o[l])
        bias = jnp.pad(b_gate_if[l], (0, LANES - 2 * M_HEADS)).reshape(1, LANES)
        bias_col = jnp.broadcast_to(b_gate_if[l][:, None], (2 * M_HEADS, DEC_BATCH))
        gain = jnp.pad(mlstm_norm_g[l].reshape(M_HEADS, M_DV), ((0, 0), (0, DVP - M_DV)))
        gain_col = mlstm_norm_g[l].reshape(M_HEADS, M_DV, 1)

        x, xb = ffn_ln(x, w_ffn_in_b, w_ffn_out_b, ln_g[l, 0], ln_b[l, 0], l, 0, True)
        proj = branch_proj(xb, w_all)

        hm, pc, pn, pm = mlstm_prompt(proj, bias, gain.reshape(M_HEADS, 1, DVP), hm)
        hm, sct_all, snt, smt = mlstm_sample(proj, bias_col, gain_col, c0t, n0t, m0t, hm, sct_all, l)
        yc, pconv = conv_prompt(proj, conv_w[l], yc)
        yc, sconv = conv_sample(proj, state_conv[l], conv_w[l], yc)
        oa, *kv_all = attn_prompt(proj, cos_p[:, :LANES], sin_p[:, :LANES], oa, kv_all, l)
        oa, kt_s, vt_s = attn_sample(proj, cos_s, sin_s, caches_t, oa, l)

        x = merge_ln(x, xb, hm, yc, oa, w_all, w_um, w_uc, w_ua, w_ol, ln_g[l, 1], ln_b[l, 1])
        x = ffn_ln(x, w_ffn_in_b, w_ffn_out_b, ln_g[l, 2], ln_b[l, 2], l, 1, False)

        kt_s = kt_s.reshape(3, A_HPG, A_HEAD_DIM, DEC_BATCH)
        vt_s = vt_s.reshape(3, A_HPG, A_HEAD_DIM, DEC_BATCH)
        kv_s = [jnp.stack([kt_s[gi], vt_s[gi]], axis=0) for gi in range(3)]
        p_states.append((pc, pn.reshape(BATCH, M_HEADS, M_DK), pm.reshape(BATCH, M_HEADS), pconv))
        s_states.append((snt, smt, sconv, kv_s[0], kv_s[1], kv_s[2]))

    y_prompt = x[:M_PROMPT].reshape(BATCH, SEQ, D_MODEL)
    y_sample = x[M_PROMPT:].reshape(DEC_BATCH, 1, D_MODEL)
    p_out = [jnp.stack(z) for z in zip(*p_states)]
    p_out += [jnp.transpose(kvt.reshape(DEPTH, BATCH, 2, A_HPG, A_HEAD_DIM, kvt.shape[-1]),
                            (0, 1, 5, 2, 3, 4))
              for kvt in kv_all]
    snt, smt, sconv, kv0, kv1, kv2 = [jnp.stack(z) for z in zip(*s_states)]
    s_out = [jnp.transpose(sct_all, (0, 4, 1, 2, 3)),
             jnp.transpose(snt, (0, 3, 1, 2)),
             jnp.transpose(smt.reshape(DEPTH, M_HEADS, DEC_BATCH), (0, 2, 1)),
             sconv]
    s_out += [jnp.transpose(kv, (0, 4, 1, 2, 3)).reshape(DEPTH, DEC_BATCH, 1, 2, A_HPG, A_HEAD_DIM)
              for kv in (kv0, kv1, kv2)]
    return (y_prompt, y_sample, *p_out, *s_out)
```

```python
import functools
import math

import jax
import jax.numpy as jnp
import numpy as np
from jax import lax
from jax.experimental import pallas as pl
from jax.experimental.pallas import tpu as pltpu

F32 = jnp.float32
BF16 = jnp.bfloat16

D_MODEL = 2048
BATCH = 4
SEQ = 2048
DEPTH = 2
DEC_BATCH = 128
PAST_LEN = 2048
M_HEADS = 4
M_DV = 192
M_DK = 96
M_QK = M_HEADS * M_DK
M_WIDTH = M_HEADS * M_DV
CONV_WIDTH = 512
CONV_K = 3
A_GROUPS = ((128, 1), (512, 4), (2048, 16))
A_HPG = 4
A_HEAD_DIM = 64
A_GW = A_HPG * A_HEAD_DIM
A_WIDTH = 3 * A_GW
ROPE_THETA = 10000.0
N_BRANCH = 3
D_FF = 5632
LN_EPS = 1e-5
ALPHA = (2 * DEPTH) ** 0.25
IN_SIZES = (M_QK, M_QK, M_WIDTH, M_HEADS, M_HEADS, M_WIDTH,
            CONV_WIDTH, CONV_WIDTH, CONV_WIDTH,
            A_WIDTH, A_WIDTH, A_WIDTH, N_BRANCH * D_MODEL)
IN_OFFSETS = tuple(int(o) for o in np.cumsum((0,) + IN_SIZES))

M_PROMPT = BATCH * SEQ
M_ROWS = M_PROMPT + DEC_BATCH

LANES = 128
DKP = 128
DVP = 256
VMEM_LIMIT = 52 * 1024 * 1024
VMEM_LIMIT_FFN = 56 * 1024 * 1024

C_MV, C_MO = 0, 1024
C_MQ, C_MK = 2048, 2560
C_CB, C_CC, C_CH = 3072, 3584, 4096
C_AQ, C_AK, C_AV = 4608, 5376, 6144
C_IF = 6912
PROJ_W = 7168

NEG = -1e30


def _sigmoid(x):
    return 1.0 / (1.0 + jnp.exp(-x))


def _layer_norm(z, g, b):
    mu = jnp.mean(z, axis=-1, keepdims=True)
    zc = z - mu
    var = jnp.mean(zc * zc, axis=-1, keepdims=True)
    return zc * lax.rsqrt(var + LN_EPS) * g + b


def _cparams(sem, vmem_limit=VMEM_LIMIT):
    return pltpu.CompilerParams(dimension_semantics=sem, vmem_limit_bytes=vmem_limit)


FFN_TM = 832
FFN_TF = 512


def _ffn_kernel(x_ref, wa_ref, wb_ref, wo_ref, g_ref, b_ref, y_ref, xb_scr):
    f = pl.program_id(1)

    @pl.when(f == 0)
    def _():
        xb_scr[...] = x_ref[...].astype(BF16)
        y_ref[...] = jnp.zeros_like(y_ref)

    xb = xb_scr[...]
    a = jnp.dot(xb, wa_ref[...], preferred_element_type=F32)
    b = jnp.dot(xb, wb_ref[...], preferred_element_type=F32)
    h = (a * _sigmoid(a)) * b
    y_ref[...] += jnp.dot(h.astype(BF16), wo_ref[...], preferred_element_type=F32)

    @pl.when(f == pl.num_programs(1) - 1)
    def _():
        z = ALPHA * x_ref[...] + 0.5 * y_ref[...]
        y_ref[...] = _layer_norm(z, g_ref[...], b_ref[...])


def ffn_ln(x, w_in, w_out, g, b, layer, which):
    m = x.shape[0]
    nf = D_FF // FFN_TF
    return pl.pallas_call(
        _ffn_kernel,
        grid=(m // FFN_TM, nf),
        in_specs=[
            pl.BlockSpec((FFN_TM, D_MODEL), lambda i, f: (i, 0)),
            pl.BlockSpec((None, None, D_MODEL, FFN_TF), lambda i, f: (layer, which, 0, f)),
            pl.BlockSpec((None, None, D_MODEL, FFN_TF), lambda i, f: (layer, which, 0, f + nf)),
            pl.BlockSpec((None, None, FFN_TF, D_MODEL), lambda i, f: (layer, which, f, 0)),
            pl.BlockSpec((1, D_MODEL), lambda i, f: (0, 0)),
            pl.BlockSpec((1, D_MODEL), lambda i, f: (0, 0)),
        ],
        out_specs=pl.BlockSpec((FFN_TM, D_MODEL), lambda i, f: (i, 0)),
        out_shape=jax.ShapeDtypeStruct((m, D_MODEL), F32),
        scratch_shapes=[pltpu.VMEM((FFN_TM, D_MODEL), BF16)],
        compiler_params=_cparams(("parallel", "arbitrary"), VMEM_LIMIT_FFN),
        name="ffn_ln",
    )(x, w_in, w_in, w_out, g.reshape(1, D_MODEL), b.reshape(1, D_MODEL))


PROJ_TM = 832
PROJ_TN = 1792


NT_DIMS = (((1,), (1,)), ((), ()))


def _proj_kernel(x_ref, wt_ref, o_ref, xb_scr):
    @pl.when(pl.program_id(1) == 0)
    def _():
        xb_scr[...] = x_ref[...].astype(BF16)

    o_ref[...] = lax.dot_general(xb_scr[...], wt_ref[...], NT_DIMS, preferred_element_type=F32)


def branch_proj(x, wt):
    m = x.shape[0]
    return pl.pallas_call(
        _proj_kernel,
        grid=(m // PROJ_TM, PROJ_W // PROJ_TN),
        in_specs=[pl.BlockSpec((PROJ_TM, D_MODEL), lambda i, j: (i, 0)),
                  pl.BlockSpec((PROJ_TN, D_MODEL), lambda i, j: (j, 0))],
        out_specs=pl.BlockSpec((PROJ_TM, PROJ_TN), lambda i, j: (i, j)),
        out_shape=jax.ShapeDtypeStruct((m, PROJ_W), F32),
        scratch_shapes=[pltpu.VMEM((PROJ_TM, D_MODEL), BF16)],
        compiler_params=_cparams(("parallel", "arbitrary")),
        name="branch_proj",
    )(x, wt)


MRG_TM = 832
MRG_TN = 512
OUT_TM = 640


def _gate_up_kernel(x_ref, hm_ref, yc_ref, oa_ref, wg0_ref, wg1_ref, wg2_ref,
                    wum_ref, wuc_ref, wua_ref, o_ref, xb_scr):
    @pl.when(pl.program_id(1) == 0)
    def _():
        xb_scr[...] = x_ref[...].astype(BF16)

    xb = xb_scr[...]

    def gated(wg_ref, br_ref, wu_ref):
        gate = _sigmoid(lax.dot_general(xb, wg_ref[...], NT_DIMS, preferred_element_type=F32))
        return gate * jnp.dot(br_ref[...], wu_ref[...], preferred_element_type=F32)

    merged = (gated(wg0_ref, hm_ref, wum_ref) + gated(wg1_ref, yc_ref, wuc_ref)
              + gated(wg2_ref, oa_ref, wua_ref))
    o_ref[...] = merged.astype(BF16)


def _out_ln_kernel(x_ref, mg_ref, wo_ref, g_ref, b_ref, y_ref):
    z = ALPHA * x_ref[...] + jnp.dot(mg_ref[...], wo_ref[...], preferred_element_type=F32)
    y_ref[...] = _layer_norm(z, g_ref[...], b_ref[...])


def merge_ln(x, hm, yc, oa, w_all, wum, wuc, wua, wo, g, b):
    m = x.shape[0]
    nn = D_MODEL // MRG_TN
    g0 = PROJ_W // MRG_TN
    row = lambda w: pl.BlockSpec((MRG_TM, w), lambda i, n: (i, 0))
    merged = pl.pallas_call(
        _gate_up_kernel,
        grid=(m // MRG_TM, nn),
        in_specs=[
            row(D_MODEL), row(M_HEADS * DVP), row(CONV_WIDTH), row(A_GW),
            pl.BlockSpec((MRG_TN, D_MODEL), lambda i, n: (g0 + n, 0)),
            pl.BlockSpec((MRG_TN, D_MODEL), lambda i, n: (g0 + n + nn, 0)),
            pl.BlockSpec((MRG_TN, D_MODEL), lambda i, n: (g0 + n + 2 * nn, 0)),
            pl.BlockSpec((M_HEADS * DVP, MRG_TN), lambda i, n: (0, n)),
            pl.BlockSpec((CONV_WIDTH, MRG_TN), lambda i, n: (0, n)),
            pl.BlockSpec((A_GW, MRG_TN), lambda i, n: (0, n)),
        ],
        out_specs=pl.BlockSpec((MRG_TM, MRG_TN), lambda i, n: (i, n)),
        out_shape=jax.ShapeDtypeStruct((m, D_MODEL), BF16),
        scratch_shapes=[pltpu.VMEM((MRG_TM, D_MODEL), BF16)],
        compiler_params=_cparams(("parallel", "arbitrary")),
        name="gate_up",
    )(x, hm, yc, oa, w_all, w_all, w_all, wum, wuc, wua)
    rows = pl.BlockSpec((OUT_TM, D_MODEL), lambda i: (i, 0))
    vec = pl.BlockSpec((1, D_MODEL), lambda i: (0, 0))
    return pl.pallas_call(
        _out_ln_kernel,
        grid=(m // OUT_TM,),
        in_specs=[rows, rows, pl.BlockSpec((D_MODEL, D_MODEL), lambda i: (0, 0)), vec, vec],
        out_specs=rows,
        out_shape=jax.ShapeDtypeStruct((m, D_MODEL), F32),
        compiler_params=_cparams(("parallel",)),
        name="out_ln",
    )(x, merged, wo, g.reshape(1, D_MODEL), b.reshape(1, D_MODEL))


def _conv_prompt_kernel(cb_ref, cc_ref, ch_ref, w_ref, y_dst_ref, y_ref, st_ref, u_scr):
    del y_dst_ref
    u = cc_ref[...] * ch_ref[...]
    u_scr[pl.ds(0, 8), :] = jnp.zeros((8, CONV_WIDTH), F32)
    u_scr[pl.ds(8, SEQ), :] = u
    w = w_ref[...]
    acc = (w[0:1, :] * u_scr[pl.ds(6, SEQ), :] + w[1:2, :] * u_scr[pl.ds(7, SEQ), :]
           + w[2:3, :] * u)
    y_ref[...] = (cb_ref[...] * acc).astype(BF16)
    st_ref[...] = u_scr[pl.ds(8 + SEQ - (CONV_K - 1), CONV_K - 1), :]


def conv_prompt(proj, conv_w, y_dst):
    blk = lambda c: pl.BlockSpec((SEQ, CONV_WIDTH), lambda b, c=c: (b, c // CONV_WIDTH))
    return pl.pallas_call(
        _conv_prompt_kernel,
        grid=(BATCH,),
        in_specs=[blk(C_CB), blk(C_CC), blk(C_CH),
                  pl.BlockSpec((CONV_K, CONV_WIDTH), lambda b: (0, 0)),
                  pl.BlockSpec(memory_space=pl.ANY)],
        out_specs=[pl.BlockSpec((SEQ, CONV_WIDTH), lambda b: (b, 0)),
                   pl.BlockSpec((None, CONV_K - 1, CONV_WIDTH), lambda b: (b, 0, 0))],
        out_shape=[jax.ShapeDtypeStruct((M_ROWS, CONV_WIDTH), BF16),
                   jax.ShapeDtypeStruct((BATCH, CONV_K - 1, CONV_WIDTH), F32)],
        input_output_aliases={4: 0},
        scratch_shapes=[pltpu.VMEM((SEQ + 8, CONV_WIDTH), F32)],
        compiler_params=_cparams(("parallel",)),
        name="conv_prompt",
    )(proj, proj, proj, conv_w, y_dst)


def _conv_sample_kernel(cb_ref, cc_ref, ch_ref, prev_ref, w_ref, y_dst_ref, y_ref, st_ref):
    del y_dst_ref
    u = cc_ref[...] * ch_ref[...]
    w = w_ref[...]
    p0 = prev_ref[:, 0, :]
    p1 = prev_ref[:, 1, :]
    acc = w[0:1, :] * p0 + w[1:2, :] * p1 + w[2:3, :] * u
    y_ref[...] = (cb_ref[...] * acc).astype(BF16)
    st_ref[:, 0, :] = p1
    st_ref[:, 1, :] = u


def conv_sample(proj, prev, conv_w, y_dst):
    rb = M_PROMPT // DEC_BATCH
    blk = lambda c: pl.BlockSpec((DEC_BATCH, CONV_WIDTH), lambda i, c=c: (rb, c // CONV_WIDTH))
    full3 = pl.BlockSpec((DEC_BATCH, CONV_K - 1, CONV_WIDTH), lambda i: (0, 0, 0))
    return pl.pallas_call(
        _conv_sample_kernel,
        grid=(1,),
        in_specs=[blk(C_CB), blk(C_CC), blk(C_CH), full3,
                  pl.BlockSpec((CONV_K, CONV_WIDTH), lambda i: (0, 0)),
                  pl.BlockSpec(memory_space=pl.ANY)],
        out_specs=[pl.BlockSpec((DEC_BATCH, CONV_WIDTH), lambda i: (rb, 0)), full3],
        out_shape=[jax.ShapeDtypeStruct((M_ROWS, CONV_WIDTH), BF16),
                   jax.ShapeDtypeStruct((DEC_BATCH, CONV_K - 1, CONV_WIDTH), F32)],
        input_output_aliases={5: 0},
        compiler_params=_cparams(("arbitrary",)),
        name="conv_sample",
    )(proj, proj, proj, prev, conv_w, y_dst)


M_L = 128


def _log_sigmoid(x):
    return jnp.minimum(x, 0.0) - jnp.log1p(jnp.exp(-jnp.abs(x)))


def _head_norm_gate(h, o_pre, gain):
    lane = lax.broadcasted_iota(jnp.int32, h.shape, 1)
    real = lane < M_DV
    mu = jnp.sum(h, axis=-1, keepdims=True) * (1.0 / M_DV)
    hc = jnp.where(real, h - mu, 0.0)
    var = jnp.sum(hc * hc, axis=-1, keepdims=True) * (1.0 / M_DV)
    return _sigmoid(o_pre) * (hc * lax.rsqrt(var + LN_EPS) * gain)


M_TS = 1024


M_NROW = M_DV


def _mlstm_prompt_kernel(q_ref, k_ref, v_ref, o_ref, if_ref, bias_ref, gain_ref, hm_dst_ref,
                         hm_ref, c_out_ref, n_out_ref, m_out_ref, ct_scr, m_scr):
    del hm_dst_ref
    step = pl.program_id(1)

    @pl.when(step == 0)
    def _():
        ct_scr[...] = jnp.zeros_like(ct_scr)
        m_scr[...] = jnp.zeros_like(m_scr)

    row = lax.broadcasted_iota(jnp.int32, (M_L, M_L), 0)
    col = lax.broadcasted_iota(jnp.int32, (M_L, M_L), 1)
    causal_t = row <= col
    tri = (col <= row).astype(F32)
    bias = bias_ref[...]
    ones_lane = lax.broadcasted_iota(jnp.int32, (M_L, DVP), 1) == M_NROW
    real_rows = lax.broadcasted_iota(jnp.int32, (DVP, M_L), 0) < M_DV
    tn = (((0,), (0,)), ((), ()))

    def chunk(c, carry):
        r0 = pl.multiple_of(c * M_L, M_L)
        x_if = if_ref[pl.ds(r0, M_L), :] + bias
        log_f = _log_sigmoid(x_if)
        cs = jnp.dot(tri, log_f, preferred_element_type=F32, precision=lax.Precision.HIGHEST)
        zt = jnp.where(col < M_HEADS, x_if, cs).T
        for hd in range(M_HEADS):
            b_row = zt[M_HEADS + hd:M_HEADS + hd + 1, :]
            c_col = x_if[:, hd:hd + 1] - cs[:, M_HEADS + hd:M_HEADS + hd + 1]
            c_rep = jnp.broadcast_to(c_col, (M_L, M_L))
            m_prev = m_scr[hd]
            b_last = b_row[:, M_L - 1:M_L]

            d_t = jnp.where(causal_t, b_row + c_rep, NEG)
            inter = b_row + m_prev
            m_t = jnp.maximum(jnp.max(d_t, axis=0, keepdims=True), inter)
            q = q_ref[pl.ds(r0, M_L), pl.ds(hd * DKP, DKP)].astype(BF16)
            k = k_ref[pl.ds(r0, M_L), pl.ds(hd * DKP, DKP)] * (M_DK ** -0.5)
            v1 = jnp.where(ones_lane, 1.0, v_ref[pl.ds(r0, M_L), pl.ds(hd * DVP, DVP)]).astype(BF16)
            s_t = lax.dot_general(k.astype(BF16), q, NT_DIMS, preferred_element_type=F32) * jnp.exp(d_t - m_t)
            w_inter = jnp.exp(inter - m_t)
            ct_prev = ct_scr[hd]
            num_t = (lax.dot_general(v1, s_t.astype(BF16), tn, preferred_element_type=F32)
                     + w_inter * lax.dot_general(ct_prev.astype(BF16), q, NT_DIMS,
                                                 preferred_element_type=F32))
            den = num_t[M_NROW:M_NROW + 1, :]
            h_t = jnp.where(real_rows, num_t / jnp.maximum(jnp.abs(den), jnp.exp(-m_t)), 0.0)
            mu = jnp.sum(h_t, axis=0, keepdims=True) * (1.0 / M_DV)
            hc = jnp.where(real_rows, h_t - mu, 0.0)
            var = jnp.sum(hc * hc, axis=0, keepdims=True) * (1.0 / M_DV)
            hn = (hc * lax.rsqrt(var + LN_EPS)).T * gain_ref[hd]
            o_pre = o_ref[pl.ds(r0, M_L), pl.ds(hd * DVP, DVP)]
            hm_ref[pl.ds(r0, M_L), pl.ds(hd * DVP, DVP)] = (_sigmoid(o_pre) * hn).astype(BF16)

            m_new = jnp.maximum(b_last + m_prev, b_last + jnp.max(c_rep, axis=0, keepdims=True)[:, 0:1])
            kw = k * jnp.exp(c_rep + (b_last - m_new))
            ct_scr[hd] = (jnp.exp(b_last + m_prev - m_new) * ct_prev
                          + lax.dot_general(v1, kw.astype(BF16), tn, preferred_element_type=F32))
            m_scr[hd] = m_new
        return carry

    lax.fori_loop(0, M_TS // M_L, chunk, 0)

    @pl.when(step == pl.num_programs(1) - 1)
    def _():
        for hd in range(M_HEADS):
            ct = ct_scr[hd]
            c_out_ref[hd] = ct.T[0:M_DK, 0:M_DV]
            n_out_ref[hd] = ct[M_NROW:M_NROW + 1, 0:M_DK]
            m_out_ref[hd] = m_scr[hd]


def mlstm_prompt(proj, bias, gain, hm_dst):
    ns = SEQ // M_TS
    def cblk(c0, w):
        return pl.BlockSpec((M_TS, w), lambda b, s: (b * ns + s, c0 // w))
    return pl.pallas_call(
        _mlstm_prompt_kernel,
        grid=(BATCH, ns),
        in_specs=[cblk(C_MQ, M_HEADS * DKP), cblk(C_MK, M_HEADS * DKP),
                  cblk(C_MV, M_HEADS * DVP), cblk(C_MO, M_HEADS * DVP),
                  cblk(C_IF, LANES),
                  pl.BlockSpec((1, LANES), lambda b, s: (0, 0)),
                  pl.BlockSpec((M_HEADS, 1, DVP), lambda b, s: (0, 0, 0)),
                  pl.BlockSpec(memory_space=pl.ANY)],
        out_specs=[pl.BlockSpec((M_TS, M_HEADS * DVP), lambda b, s: (b * ns + s, 0)),
                   pl.BlockSpec((None, M_HEADS, M_DK, M_DV), lambda b, s: (b, 0, 0, 0)),
                   pl.BlockSpec((None, M_HEADS, 1, M_DK), lambda b, s: (b, 0, 0, 0)),
                   pl.BlockSpec((None, M_HEADS, 1, 1), lambda b, s: (b, 0, 0, 0))],
        out_shape=[jax.ShapeDtypeStruct((M_ROWS, M_HEADS * DVP), BF16),
                   jax.ShapeDtypeStruct((BATCH, M_HEADS, M_DK, M_DV), F32),
                   jax.ShapeDtypeStruct((BATCH, M_HEADS, 1, M_DK), F32),
                   jax.ShapeDtypeStruct((BATCH, M_HEADS, 1, 1), F32)],
        scratch_shapes=[pltpu.VMEM((M_HEADS, DVP, DKP), F32), pltpu.VMEM((M_HEADS, 1, 1), F32)],
        compiler_params=_cparams(("parallel", "arbitrary")),
        input_output_aliases={7: 0},
        name="mlstm_prompt",
    )(proj, proj, proj, proj, proj, bias, gain, hm_dst)


MS_DC = 48


def _pick_row(x8, j):
    rows = lax.broadcasted_iota(jnp.int32, x8.shape, 0)
    return jnp.sum(jnp.where(rows == j, x8, 0.0), axis=0, keepdims=True)


def _mlstm_sample_kernel(q_ref, k_ref, v_ref, o_ref, if_ref, bias_ref, gain_ref,
                         c0_ref, n0_ref, m0_ref, hm_dst_ref, c_dst_ref,
                         hm_ref, c_out_ref, n_out_ref, m_out_ref,
                         qt_scr, kw_scr, vt_scr, acc_scr, st_scr):
    del hm_dst_ref, c_dst_ref
    hd = pl.program_id(0)
    c = pl.program_id(1)

    @pl.when(c == 0)
    def _():
        qt = q_ref[...].T
        kt = (k_ref[...] * (M_DK ** -0.5)).T
        vt_scr[...] = v_ref[...].T
        x_if = if_ref[...].T[0:2 * M_HEADS, :] + bias_ref[...]
        i_pre = _pick_row(x_if, hd)
        log_f = _log_sigmoid(_pick_row(x_if, hd + M_HEADS))
        inter = log_f + m0_ref[...]
        m_new = jnp.maximum(i_pre, inter)
        w_k = jnp.exp(i_pre - m_new)
        decay = jnp.exp(inter - m_new)
        n_prev = n0_ref[...]
        s = jnp.sum(qt * kt, axis=0, keepdims=True) * w_k
        den = s + decay * jnp.sum(qt[:M_DK] * n_prev, axis=0, keepdims=True)
        kw = kt * w_k
        qt_scr[...] = qt
        kw_scr[...] = kw
        n_out_ref[...] = decay * n_prev + kw[:M_DK]
        m_out_ref[...] = m_new
        st_scr[0:1, :] = s
        st_scr[1:2, :] = decay
        st_scr[2:3, :] = den
        st_scr[3:4, :] = m_new
        acc_scr[...] = jnp.zeros_like(acc_scr)

    decay = st_scr[1:2, :]
    vt = vt_scr[pl.ds(0, M_DV), :]

    def tile(t, acc):
        r8 = pl.multiple_of(c * MS_DC + t * 8, 8)
        q8 = qt_scr[pl.ds(r8, 8), :]
        kw8 = kw_scr[pl.ds(r8, 8), :]
        for r in range(8):
            c_row = c0_ref[t * 8 + r]
            c_out_ref[t * 8 + r] = decay * c_row + kw8[r:r + 1, :] * vt
            acc = acc + q8[r:r + 1, :] * c_row
        return acc

    acc = lax.fori_loop(0, MS_DC // 8, tile, acc_scr[...])
    acc_scr[...] = acc

    @pl.when(c == pl.num_programs(1) - 1)
    def _():
        s = st_scr[0:1, :]
        den = st_scr[2:3, :]
        m_t = st_scr[3:4, :]
        h = (s * vt + decay * acc) / jnp.maximum(jnp.abs(den), jnp.exp(-m_t))
        mu = jnp.mean(h, axis=0, keepdims=True)
        hc = h - mu
        var = jnp.mean(hc * hc, axis=0, keepdims=True)
        o_pre = o_ref[...].T[:M_DV, :]
        out = _sigmoid(o_pre) * (hc * lax.rsqrt(var + LN_EPS) * gain_ref[...])
        out = jnp.concatenate([out, jnp.zeros((DVP - M_DV, DEC_BATCH), F32)], axis=0)
        hm_ref[...] = out.T.astype(BF16)


def mlstm_sample(proj, bias_col, gain_col, c0t, n0t, m0t, hm_dst, c_dst, layer):
    any_spec = pl.BlockSpec(memory_space=pl.ANY)
    rb = M_PROMPT // DEC_BATCH
    nc = M_DK // MS_DC
    def cblk(c0_, w):
        return pl.BlockSpec((DEC_BATCH, w), lambda h, c: (rb, c0_ // w + h))
    return pl.pallas_call(
        _mlstm_sample_kernel,
        grid=(M_HEADS, nc),
        in_specs=[cblk(C_MQ, DKP), cblk(C_MK, DKP), cblk(C_MV, DVP), cblk(C_MO, DVP),
                  pl.BlockSpec((DEC_BATCH, LANES), lambda h, c: (rb, C_IF // LANES)),
                  pl.BlockSpec((2 * M_HEADS, DEC_BATCH), lambda h, c: (0, 0)),
                  pl.BlockSpec((None, M_DV, 1), lambda h, c: (h, 0, 0)),
                  pl.BlockSpec((None, None, MS_DC, M_DV, DEC_BATCH), lambda h, c: (layer, h, c, 0, 0)),
                  pl.BlockSpec((None, None, M_DK, DEC_BATCH), lambda h, c: (layer, h, 0, 0)),
                  pl.BlockSpec((None, None, 1, DEC_BATCH), lambda h, c: (layer, h, 0, 0)),
                  any_spec, any_spec],
        out_specs=[pl.BlockSpec((DEC_BATCH, DVP), lambda h, c: (rb, h)),
                   pl.BlockSpec((None, None, MS_DC, M_DV, DEC_BATCH), lambda h, c: (layer, h, c, 0, 0)),
                   pl.BlockSpec((None, M_DK, DEC_BATCH), lambda h, c: (h, 0, 0)),
                   pl.BlockSpec((None, 1, DEC_BATCH), lambda h, c: (h, 0, 0))],
        out_shape=[jax.ShapeDtypeStruct((M_ROWS, M_HEADS * DVP), BF16),
                   jax.ShapeDtypeStruct((DEPTH, M_HEADS, M_DK, M_DV, DEC_BATCH), F32),
                   jax.ShapeDtypeStruct((M_HEADS, M_DK, DEC_BATCH), F32),
                   jax.ShapeDtypeStruct((M_HEADS, 1, DEC_BATCH), F32)],
        scratch_shapes=[pltpu.VMEM((DKP, DEC_BATCH), F32), pltpu.VMEM((DKP, DEC_BATCH), F32),
                        pltpu.VMEM((DVP, DEC_BATCH), F32), pltpu.VMEM((M_DV, DEC_BATCH), F32),
                        pltpu.VMEM((8, DEC_BATCH), F32)],
        input_output_aliases={10: 0, 11: 1},
        compiler_params=_cparams(("arbitrary", "arbitrary")),
        name="mlstm_sample",
    )(proj, proj, proj, proj, proj, bias_col, gain_col, c0t, n0t, m0t, hm_dst, c_dst)


A_Q = 128
A_LT = A_GW // LANES


def _rope(x, cos, sin_signed):
    lane = lax.broadcasted_iota(jnp.int32, x.shape, 1)
    first_half = (lane % A_HEAD_DIM) < (A_HEAD_DIM // 2)
    partner = jnp.where(first_half, pltpu.roll(x, x.shape[1] - A_HEAD_DIM // 2, 1),
                        pltpu.roll(x, A_HEAD_DIM // 2, 1))
    return x * cos + partner * sin_signed


def _head_masks(shape):
    lane = lax.broadcasted_iota(jnp.int32, shape, 1)
    return [(lane // A_HEAD_DIM) == h for h in range(A_HPG)]


def _attn_group_prompt(dil, gi, qs_scr, ks_scr, vs_scr, o_scr, l_scr):
    length = SEQ // dil
    nb = length // A_Q
    row = lax.broadcasted_iota(jnp.int32, (A_Q, A_Q), 0)
    col = lax.broadcasted_iota(jnp.int32, (A_Q, A_Q), 1)
    cur_ok = col <= row
    prev_ok = col >= row
    masks = _head_masks((A_Q, A_GW))
    nt = (((1,), (1,)), ((), ()))

    def window(start):
        if dil == 1:
            return pl.ds(pl.multiple_of(start, A_Q), A_Q)
        return pl.ds(start, A_Q, stride=dil)

    def rows(scr, start):
        w = window(start)
        return jnp.concatenate([scr[t, w, :] for t in range(A_LT)], axis=1).astype(BF16)

    def block(idx, carry):
        r = idx % dil
        n = idx // dil
        base = r + (dil * A_Q) * n
        qb = rows(qs_scr, base)
        kc = rows(ks_scr, base)
        vc = rows(vs_scr, base)
        if nb > 1:
            pbase = jnp.maximum(base - dil * A_Q, r)
            kp = rows(ks_scr, pbase)
            vp = rows(vs_scr, pbase)
            has_prev = n > 0
        o_acc = jnp.zeros((A_Q, A_GW), F32)
        l_acc = jnp.zeros((A_Q, A_GW), F32)
        for h in range(A_HPG):
            qh = jnp.where(masks[h], qb, jnp.zeros_like(qb))
            s_c = jnp.where(cur_ok, lax.dot_general(qh, kc, nt, preferred_element_type=F32), NEG)
            m = jnp.max(s_c, axis=1, keepdims=True)
            if nb > 1:
                s_p = jnp.where(jnp.logical_and(prev_ok, has_prev),
                                lax.dot_general(qh, kp, nt, preferred_element_type=F32), NEG)
                m = jnp.maximum(m, jnp.max(s_p, axis=1, keepdims=True))
            p_c = jnp.exp(s_c - m)
            l = jnp.sum(p_c, axis=1, keepdims=True)
            o_h = jnp.dot(p_c.astype(BF16), vc, preferred_element_type=F32)
            if nb > 1:
                p_p = jnp.exp(s_p - m)
                l = l + jnp.sum(p_p, axis=1, keepdims=True)
                o_h = o_h + jnp.dot(p_p.astype(BF16), vp, preferred_element_type=F32)
            o_acc = o_acc + jnp.where(masks[h], o_h / l, 0.0)
            l_acc = l_acc + jnp.where(masks[h], m + jnp.log(l), 0.0)
        w = window(base)
        for t in range(A_LT):
            o_scr[gi, t, w, :] = o_acc[:, t * LANES:(t + 1) * LANES]
            l_scr[gi, t, w, :] = l_acc[:, t * LANES:(t + 1) * LANES]
        return carry

    lax.fori_loop(0, dil * nb, block, 0)


def _attn_prompt_kernel(q_ref, k_ref, v_ref, cos_ref, sin_ref, oa_dst_ref, kvd0_ref, kvd1_ref, kvd2_ref,
                        oa_ref, kv0_ref, kv1_ref, kv2_ref, qs_scr, ks_scr, vs_scr, o_scr, l_scr):
    del oa_dst_ref, kvd0_ref, kvd1_ref, kvd2_ref
    g = pl.program_id(1)
    rc = 256

    def rope_rows(c, carry):
        sl = pl.ds(pl.multiple_of(c * rc, rc), rc)
        cos = cos_ref[sl, :]
        sin = sin_ref[sl, :]
        for t in range(A_LT):
            lanes = pl.ds(t * LANES, LANES)
            qs_scr[t, sl, :] = _rope(q_ref[sl, lanes], cos, sin) * (A_HEAD_DIM ** -0.5)
            ks_scr[t, sl, :] = _rope(k_ref[sl, lanes], cos, sin)
            vs_scr[t, sl, :] = v_ref[sl, lanes]
        return carry

    lax.fori_loop(0, SEQ // rc, rope_rows, 0)

    for gi, ((win, dil), kv_ref) in enumerate(zip(A_GROUPS, (kv0_ref, kv1_ref, kv2_ref))):
        @pl.when(g == gi)
        def _(gi=gi, dil=dil, win=win, kv_ref=kv_ref):
            _attn_group_prompt(dil, gi, qs_scr, ks_scr, vs_scr, o_scr, l_scr)
            keep = min(win, SEQ)
            for c in range(keep // LANES):
                rows = pl.ds(SEQ - keep + c * LANES, LANES)
                for t in range(A_LT):
                    kv_ref[0, pl.ds(t * LANES, LANES), pl.ds(c * LANES, LANES)] = ks_scr[t, rows, :].T
                    kv_ref[1, pl.ds(t * LANES, LANES), pl.ds(c * LANES, LANES)] = vs_scr[t, rows, :].T

    @pl.when(g == len(A_GROUPS) - 1)
    def _():
        def comb(c, carry):
            sl = pl.ds(pl.multiple_of(c * rc, rc), rc)
            for t in range(A_LT):
                l0, l1, l2 = l_scr[0, t, sl, :], l_scr[1, t, sl, :], l_scr[2, t, sl, :]
                mx = jnp.maximum(jnp.maximum(l0, l1), l2)
                e0, e1, e2 = jnp.exp(l0 - mx), jnp.exp(l1 - mx), jnp.exp(l2 - mx)
                tot = e0 * o_scr[0, t, sl, :] + e1 * o_scr[1, t, sl, :] + e2 * o_scr[2, t, sl, :]
                oa_ref[sl, pl.ds(t * LANES, LANES)] = (tot / (e0 + e1 + e2)).astype(BF16)
            return carry
        lax.fori_loop(0, SEQ // rc, comb, 0)


def attn_prompt(proj, cos, sin, oa_dst, kv_dst, layer):
    def gblk(c0):
        return pl.BlockSpec((SEQ, A_GW), lambda b, g: (b, c0 // A_GW + g))
    tab = pl.BlockSpec((SEQ, LANES), lambda b, g: (0, 0))
    keeps = [min(win, SEQ) for win, _ in A_GROUPS]
    return pl.pallas_call(
        _attn_prompt_kernel,
        grid=(BATCH, len(A_GROUPS)),
        in_specs=[gblk(C_AQ), gblk(C_AK), gblk(C_AV), tab, tab] + [pl.BlockSpec(memory_space=pl.ANY)] * 4,
        out_specs=[pl.BlockSpec((SEQ, A_GW), lambda b, g: (b, 0))]
                  + [pl.BlockSpec((None, None, 2, A_GW, kp), lambda b, g: (layer, b, 0, 0, 0)) for kp in keeps],
        out_shape=[jax.ShapeDtypeStruct((M_ROWS, A_GW), BF16)]
                  + [jax.ShapeDtypeStruct((DEPTH, BATCH, 2, A_GW, kp), F32) for kp in keeps],
        input_output_aliases={5: 0, 6: 1, 7: 2, 8: 3},
        scratch_shapes=[pltpu.VMEM((A_LT, SEQ, LANES), F32),
                        pltpu.VMEM((A_LT, SEQ, LANES), F32),
                        pltpu.VMEM((A_LT, SEQ, LANES), F32),
                        pltpu.VMEM((len(A_GROUPS), A_LT, SEQ, LANES), F32),
                        pltpu.VMEM((len(A_GROUPS), A_LT, SEQ, LANES), F32)],
        compiler_params=_cparams(("parallel", "arbitrary")),
        name="attn_prompt",
    )(proj, proj, proj, cos, sin, oa_dst, *kv_dst)


AS_BB = 2


def _attn_sample_kernel(q_ref, k_ref, v_ref, cos_ref, sin_ref, c0_ref, c1_ref, c2_ref, oa_dst_ref,
                        oa_ref, kt_ref, vt_ref, qt_scr, s0_scr, ot_scr, lt_scr):
    del oa_dst_ref
    i = pl.program_id(0)
    lane_b = lax.broadcasted_iota(jnp.int32, (1, DEC_BATCH), 1)
    sub8 = lax.broadcasted_iota(jnp.int32, (8, DEC_BATCH), 0)

    @pl.when(i == 0)
    def _():
        cos = cos_ref[...]
        sin = sin_ref[...]
        for gi in range(len(A_GROUPS)):
            gs = pl.ds(gi * A_GW, A_GW)
            qt = (_rope(q_ref[:, gs], cos, sin) * (A_HEAD_DIM ** -0.5)).T
            kt = _rope(k_ref[:, gs], cos, sin).T
            qt_scr[gi] = qt
            kt_ref[gi] = kt
            vt_ref[gi] = v_ref[:, gs].T
            prod = qt * kt
            s0 = jnp.zeros((8, DEC_BATCH), F32)
            for h in range(A_HPG):
                part = jnp.sum(prod[h * A_HEAD_DIM:(h + 1) * A_HEAD_DIM], axis=0, keepdims=True)
                s0 = jnp.where(sub8 == h, part, s0)
            s0_scr[gi] = s0
        ot_scr[...] = jnp.zeros_like(ot_scr)
        lt_scr[...] = jnp.zeros_like(lt_scr)

    for bl in range(AS_BB):
        pick = lane_b == i * AS_BB + bl
        for gi, (cache_ref, (_, dil)) in enumerate(zip((c0_ref, c1_ref, c2_ref), A_GROUPS)):
            wb = cache_ref.shape[-1]
            if dil > 1:
                keep = (lax.broadcasted_iota(jnp.int32, (1, wb), 1) & (dil - 1)) == 0
            head_row = lax.broadcasted_iota(jnp.int32, (8, wb), 0)
            q_col = jnp.sum(jnp.where(pick, qt_scr[gi], 0.0), axis=1, keepdims=True)
            v_col = jnp.sum(jnp.where(pick, vt_ref[gi], 0.0), axis=1, keepdims=True)
            s0 = jnp.sum(jnp.where(pick, s0_scr[gi], 0.0), axis=1, keepdims=True)
            s = jnp.zeros((8, wb), F32)
            for h in range(A_HPG):
                hs = slice(h * A_HEAD_DIM, (h + 1) * A_HEAD_DIM)
                part = jnp.sum(q_col[hs] * cache_ref[bl, 0, h], axis=0, keepdims=True)
                s = jnp.where(head_row == h, part, s)
            if dil > 1:
                s = jnp.where(keep, s, NEG)
            m = jnp.maximum(jnp.max(s, axis=1, keepdims=True), s0)
            p = jnp.exp(s - m)
            p0 = jnp.exp(s0 - m)
            l = jnp.sum(p, axis=1, keepdims=True) + p0
            lse = m + jnp.log(l)
            o_parts = []
            for h in range(A_HPG):
                hs = slice(h * A_HEAD_DIM, (h + 1) * A_HEAD_DIM)
                pv = jnp.sum(p[h:h + 1, :] * cache_ref[bl, 1, h], axis=1, keepdims=True)
                o_parts.append((pv + p0[h:h + 1, :] * v_col[hs]) / l[h:h + 1, :])
            o = jnp.concatenate(o_parts, axis=0)
            ot_scr[gi] = jnp.where(pick, o, ot_scr[gi])
            lt_scr[gi] = jnp.where(pick, lse, lt_scr[gi])

    @pl.when(i == pl.num_programs(0) - 1)
    def _():
        for h in range(A_HPG):
            hs = pl.ds(h * A_HEAD_DIM, A_HEAD_DIM)
            l0, l1, l2 = (lt_scr[gi, pl.ds(h, 1), :] for gi in range(3))
            mx = jnp.maximum(jnp.maximum(l0, l1), l2)
            e0, e1, e2 = jnp.exp(l0 - mx), jnp.exp(l1 - mx), jnp.exp(l2 - mx)
            tot = e0 * ot_scr[0, hs, :] + e1 * ot_scr[1, hs, :] + e2 * ot_scr[2, hs, :]
            ot_scr[0, hs, :] = tot / (e0 + e1 + e2)
        oa_ref[...] = ot_scr[0].T.astype(BF16)


def attn_sample(proj, cos, sin, caches_t, oa_dst, layer):
    rb = M_PROMPT // DEC_BATCH
    def pblk(c0):
        return pl.BlockSpec((DEC_BATCH, A_WIDTH), lambda i: (rb, c0 // A_WIDTH))
    tab = pl.BlockSpec((1, A_GW), lambda i: (0, 0))
    cache_specs = [pl.BlockSpec((None, AS_BB, 2, A_HPG, A_HEAD_DIM, ct.shape[-1]),
                                lambda i: (layer, i, 0, 0, 0, 0)) for ct in caches_t]
    ng = len(A_GROUPS)
    full3 = pl.BlockSpec((ng, A_GW, DEC_BATCH), lambda i: (0, 0, 0))
    return pl.pallas_call(
        _attn_sample_kernel,
        grid=(DEC_BATCH // AS_BB,),
        in_specs=[pblk(C_AQ), pblk(C_AK), pblk(C_AV), tab, tab] + cache_specs
                 + [pl.BlockSpec(memory_space=pl.ANY)],
        out_specs=[pl.BlockSpec((DEC_BATCH, A_GW), lambda i: (rb, 0)), full3, full3],
        out_shape=[jax.ShapeDtypeStruct((M_ROWS, A_GW), BF16),
                   jax.ShapeDtypeStruct((ng, A_GW, DEC_BATCH), F32),
                   jax.ShapeDtypeStruct((ng, A_GW, DEC_BATCH), F32)],
        scratch_shapes=[pltpu.VMEM((ng, A_GW, DEC_BATCH), F32), pltpu.VMEM((ng, 8, DEC_BATCH), F32),
                        pltpu.VMEM((ng, A_GW, DEC_BATCH), F32), pltpu.VMEM((ng, 8, DEC_BATCH), F32)],
        compiler_params=_cparams(("arbitrary",)),
        input_output_aliases={8: 0},
        name="attn_sample",
    )(proj, proj, proj, cos, sin, *caches_t, oa_dst)


def _pad_heads(wt, d, dp):
    c = wt.shape[1]
    wt = wt.reshape(M_HEADS, d, c)
    return jnp.pad(wt, ((0, 0), (0, dp - d), (0, 0))).reshape(M_HEADS * dp, c)


def _layer_weights(w_in_l, w_up_m, w_up_c, w_up_a, w_o_l):
    w_in_t = jnp.transpose(w_in_l)
    o = IN_OFFSETS
    piece = lambda i: w_in_t[o[i]:o[i + 1]]
    mq, mk, mv, mi, mf, mo, cb, cc, ch, aq, ak, av, gt = [piece(i) for i in range(13)]
    w_all = jnp.concatenate(
        [_pad_heads(mv, M_DV, DVP), _pad_heads(mo, M_DV, DVP),
         _pad_heads(mq, M_DK, DKP), _pad_heads(mk, M_DK, DKP),
         cb, cc, ch, aq, ak, av,
         mi, mf, jnp.zeros((PROJ_W - C_IF - 2 * M_HEADS, D_MODEL), F32), gt], axis=0).astype(BF16)
    w_um = jnp.pad(w_up_m.reshape(M_HEADS, M_DV, D_MODEL),
                   ((0, 0), (0, DVP - M_DV), (0, 0))).reshape(M_HEADS * DVP, D_MODEL)
    return (w_all, w_um.astype(BF16), w_up_c.astype(BF16), w_up_a.astype(BF16), w_o_l.astype(BF16))


def _rope_tables(pos):
    half = A_HEAD_DIM // 2
    inv = ROPE_THETA ** (-(2.0 * jnp.arange(half, dtype=F32)) / A_HEAD_DIM)
    ang = pos.astype(F32)[:, None] * inv[None, :]
    cos = jnp.cos(ang)
    sin = jnp.sin(ang)
    cos = jnp.tile(jnp.concatenate([cos, cos], axis=-1), (1, A_HPG))
    sin = jnp.tile(jnp.concatenate([-sin, sin], axis=-1), (1, A_HPG))
    return cos, sin


def kernel(x_prompt, x_sample, state_mlstm_C, state_mlstm_n, state_mlstm_m, state_conv,
           cache_attn_kv_w128, cache_attn_kv_w512, cache_attn_kv_w2048,
           w_in, b_gate_if, mlstm_norm_g, conv_w, w_up_mlstm, w_up_conv, w_up_attn, w_o,
           w_ffn_in, w_ffn_out, ln_g, ln_b):
    x = jnp.concatenate([x_prompt.reshape(M_PROMPT, D_MODEL),
                         x_sample.reshape(DEC_BATCH, D_MODEL)], axis=0)
    cos_p, sin_p = _rope_tables(jnp.arange(SEQ))
    cos_s, sin_s = _rope_tables(PAST_LEN + jnp.arange(1))
    w_ffn_in_b = w_ffn_in.astype(BF16)
    w_ffn_out_b = w_ffn_out.astype(BF16)

    c0t = jnp.transpose(state_mlstm_C, (0, 2, 3, 4, 1))
    n0t = jnp.transpose(state_mlstm_n, (0, 2, 3, 1))
    m0t = jnp.transpose(state_mlstm_m, (0, 2, 1)).reshape(DEPTH, M_HEADS, 1, DEC_BATCH)
    caches_t = [jnp.transpose(c, (0, 1, 3, 4, 5, 2))
                for c in (cache_attn_kv_w128, cache_attn_kv_w512, cache_attn_kv_w2048)]

    keeps = [min(win, SEQ) for win, _ in A_GROUPS]
    kv_all = [jnp.zeros((DEPTH, BATCH, 2, A_GW, kp), F32) for kp in keeps]
    sct_all = jnp.zeros((DEPTH, M_HEADS, M_DK, M_DV, DEC_BATCH), F32)
    hm = jnp.zeros((M_ROWS, M_HEADS * DVP), BF16)
    yc = jnp.zeros((M_ROWS, CONV_WIDTH), BF16)
    oa = jnp.zeros((M_ROWS, A_GW), BF16)

    p_states, s_states = [], []
    for l in range(DEPTH):
        w_all, w_um, w_uc, w_ua, w_ol = _layer_weights(
            w_in[l], w_up_mlstm[l], w_up_conv[l], w_up_attn[l], w_o[l])
        bias = jnp.pad(b_gate_if[l], (0, LANES - 2 * M_HEADS)).reshape(1, LANES)
        bias_col = jnp.broadcast_to(b_gate_if[l][:, None], (2 * M_HEADS, DEC_BATCH))
        gain = jnp.pad(mlstm_norm_g[l].reshape(M_HEADS, M_DV), ((0, 0), (0, DVP - M_DV)))
        gain_col = mlstm_norm_g[l].reshape(M_HEADS, M_DV, 1)

        x = ffn_ln(x, w_ffn_in_b, w_ffn_out_b, ln_g[l, 0], ln_b[l, 0], l, 0)
        proj = branch_proj(x, w_all)

        hm, pc, pn, pm = mlstm_prompt(proj, bias, gain.reshape(M_HEADS, 1, DVP), hm)
        hm, sct_all, snt, smt = mlstm_sample(proj, bias_col, gain_col, c0t, n0t, m0t, hm, sct_all, l)
        yc, pconv = conv_prompt(proj, conv_w[l], yc)
        yc, sconv = conv_sample(proj, state_conv[l], conv_w[l], yc)
        oa, *kv_all = attn_prompt(proj, cos_p[:, :LANES], sin_p[:, :LANES], oa, kv_all, l)
        oa, kt_s, vt_s = attn_sample(proj, cos_s, sin_s, caches_t, oa, l)

        x = merge_ln(x, hm, yc, oa, w_all, w_um, w_uc, w_ua, w_ol, ln_g[l, 1], ln_b[l, 1])
        x = ffn_ln(x, w_ffn_in_b, w_ffn_out_b, ln_g[l, 2], ln_b[l, 2], l, 1)

        kt_s = kt_s.reshape(3, A_HPG, A_HEAD_DIM, DEC_BATCH)
        vt_s = vt_s.reshape(3, A_HPG, A_HEAD_DIM, DEC_BATCH)
        kv_s = [jnp.stack([kt_s[gi], vt_s[gi]], axis=0) for gi in range(3)]
        p_states.append((pc, pn.reshape(BATCH, M_HEADS, M_DK), pm.reshape(BATCH, M_HEADS), pconv))
        s_states.append((snt, smt, sconv, kv_s[0], kv_s[1], kv_s[2]))

    y_prompt = x[:M_PROMPT].reshape(BATCH, SEQ, D_MODEL)
    y_sample = x[M_PROMPT:].reshape(DEC_BATCH, 1, D_MODEL)
    p_out = [jnp.stack(z) for z in zip(*p_states)]
    p_out += [jnp.transpose(kvt.reshape(DEPTH, BATCH, 2, A_HPG, A_HEAD_DIM, kvt.shape[-1]),
                            (0, 1, 5, 2, 3, 4))
              for kvt in kv_all]
    snt, smt, sconv, kv0, kv1, kv2 = [jnp.stack(z) for z in zip(*s_states)]
    s_out = [jnp.transpose(sct_all, (0, 4, 1, 2, 3)),
             jnp.transpose(snt, (0, 3, 1, 2)),
             jnp.transpose(smt.reshape(DEPTH, M_HEADS, DEC_BATCH), (0, 2, 1)),
             sconv]
    s_out += [jnp.transpose(kv, (0, 4, 1, 2, 3)).reshape(DEPTH, DEC_BATCH, 1, 2, A_HPG, A_HEAD_DIM)
              for kv in (kv0, kv1, kv2)]
    return (y_prompt, y_sample, *p_out, *s_out)
```

```python
import functools
import math

import jax
import jax.numpy as jnp
import numpy as np
from jax import lax
from jax.experimental import pallas as pl
from jax.experimental.pallas import tpu as pltpu

F32 = jnp.float32
BF16 = jnp.bfloat16

D_MODEL = 2048
BATCH = 4
SEQ = 2048
DEPTH = 2
DEC_BATCH = 128
PAST_LEN = 2048
M_HEADS = 4
M_DV = 192
M_DK = 96
M_QK = M_HEADS * M_DK
M_WIDTH = M_HEADS * M_DV
CONV_WIDTH = 512
CONV_K = 3
A_GROUPS = ((128, 1), (512, 4), (2048, 16))
A_HPG = 4
A_HEAD_DIM = 64
A_GW = A_HPG * A_HEAD_DIM
A_WIDTH = 3 * A_GW
ROPE_THETA = 10000.0
N_BRANCH = 3
D_FF = 5632
LN_EPS = 1e-5
ALPHA = (2 * DEPTH) ** 0.25
IN_SIZES = (M_QK, M_QK, M_WIDTH, M_HEADS, M_HEADS, M_WIDTH,
            CONV_WIDTH, CONV_WIDTH, CONV_WIDTH,
            A_WIDTH, A_WIDTH, A_WIDTH, N_BRANCH * D_MODEL)
IN_OFFSETS = tuple(int(o) for o in np.cumsum((0,) + IN_SIZES))

M_PROMPT = BATCH * SEQ
M_ROWS = M_PROMPT + DEC_BATCH

LANES = 128
DKP = 128
DVP = 256
VMEM_LIMIT = 52 * 1024 * 1024
VMEM_LIMIT_FFN = 56 * 1024 * 1024

C_MV, C_MO = 0, 1024
C_MQ, C_MK = 2048, 2560
C_CB, C_CC, C_CH = 3072, 3584, 4096
C_AQ, C_AK, C_AV = 4608, 5376, 6144
C_IF = 6912
PROJ_W = 7168

NEG = -1e30


def _sigmoid(x):
    return 1.0 / (1.0 + jnp.exp(-x))


def _layer_norm(z, g, b):
    mu = jnp.mean(z, axis=-1, keepdims=True)
    zc = z - mu
    var = jnp.mean(zc * zc, axis=-1, keepdims=True)
    return zc * lax.rsqrt(var + LN_EPS) * g + b


def _cparams(sem, vmem_limit=VMEM_LIMIT):
    return pltpu.CompilerParams(dimension_semantics=sem, vmem_limit_bytes=vmem_limit)


FFN_TM = 832
FFN_TF = 512


def _ffn_kernel(x_ref, wa_ref, wb_ref, wo_ref, g_ref, b_ref, y_ref, xb_scr):
    f = pl.program_id(1)

    @pl.when(f == 0)
    def _():
        x = x_ref[...]
        xb_scr[...] = x.astype(BF16)
        y_ref[...] = ALPHA * x

    xb = xb_scr[...]
    a = jnp.dot(xb, wa_ref[...], preferred_element_type=F32)
    b = jnp.dot(xb, wb_ref[...], preferred_element_type=F32)
    h = (a * _sigmoid(a)) * b
    y_ref[...] += jnp.dot(h.astype(BF16), wo_ref[...], preferred_element_type=F32)

    @pl.when(f == pl.num_programs(1) - 1)
    def _():
        y_ref[...] = _layer_norm(y_ref[...], g_ref[...], b_ref[...])


def ffn_ln(x, w_in, w_out, g, b, layer, which):
    m = x.shape[0]
    nf = D_FF // FFN_TF
    return pl.pallas_call(
        _ffn_kernel,
        grid=(m // FFN_TM, nf),
        in_specs=[
            pl.BlockSpec((FFN_TM, D_MODEL), lambda i, f: (i, 0)),
            pl.BlockSpec((None, None, D_MODEL, FFN_TF), lambda i, f: (layer, which, 0, f)),
            pl.BlockSpec((None, None, D_MODEL, FFN_TF), lambda i, f: (layer, which, 0, f + nf)),
            pl.BlockSpec((None, None, FFN_TF, D_MODEL), lambda i, f: (layer, which, f, 0)),
            pl.BlockSpec((1, D_MODEL), lambda i, f: (0, 0)),
            pl.BlockSpec((1, D_MODEL), lambda i, f: (0, 0)),
        ],
        out_specs=pl.BlockSpec((FFN_TM, D_MODEL), lambda i, f: (i, 0)),
        out_shape=jax.ShapeDtypeStruct((m, D_MODEL), F32),
        scratch_shapes=[pltpu.VMEM((FFN_TM, D_MODEL), BF16)],
        compiler_params=_cparams(("parallel", "arbitrary"), VMEM_LIMIT_FFN),
        name="ffn_ln",
    )(x, w_in, w_in, w_out, g.reshape(1, D_MODEL), b.reshape(1, D_MODEL))


PROJ_TM = 1040
PROJ_TN = 1024


NT_DIMS = (((1,), (1,)), ((), ()))


def _proj_kernel(x_ref, wt_ref, o_ref, xb_scr):
    @pl.when(pl.program_id(1) == 0)
    def _():
        xb_scr[...] = x_ref[...].astype(BF16)

    o_ref[...] = lax.dot_general(xb_scr[...], wt_ref[...], NT_DIMS, preferred_element_type=F32)


def branch_proj(x, wt):
    m = x.shape[0]
    return pl.pallas_call(
        _proj_kernel,
        grid=(m // PROJ_TM, PROJ_W // PROJ_TN),
        in_specs=[pl.BlockSpec((PROJ_TM, D_MODEL), lambda i, j: (i, 0)),
                  pl.BlockSpec((PROJ_TN, D_MODEL), lambda i, j: (j, 0))],
        out_specs=pl.BlockSpec((PROJ_TM, PROJ_TN), lambda i, j: (i, j)),
        out_shape=jax.ShapeDtypeStruct((m, PROJ_W), F32),
        scratch_shapes=[pltpu.VMEM((PROJ_TM, D_MODEL), BF16)],
        compiler_params=_cparams(("parallel", "arbitrary")),
        name="branch_proj",
    )(x, wt)


MRG_TM = 640
MRG_TN = 512


def _gate_up_kernel(x_ref, hm_ref, yc_ref, oa_ref, wg0_ref, wg1_ref, wg2_ref,
                    wum_ref, wuc_ref, wua_ref, o_ref, xb_scr):
    @pl.when(pl.program_id(1) == 0)
    def _():
        xb_scr[...] = x_ref[...].astype(BF16)

    xb = xb_scr[...]

    def gated(wg_ref, br_ref, wu_ref):
        gate = _sigmoid(lax.dot_general(xb, wg_ref[...], NT_DIMS, preferred_element_type=F32))
        return gate * jnp.dot(br_ref[...], wu_ref[...], preferred_element_type=F32)

    merged = (gated(wg0_ref, hm_ref, wum_ref) + gated(wg1_ref, yc_ref, wuc_ref)
              + gated(wg2_ref, oa_ref, wua_ref))
    o_ref[...] = merged.astype(BF16)


def _out_ln_kernel(x_ref, mg_ref, wo_ref, g_ref, b_ref, y_ref):
    z = ALPHA * x_ref[...] + jnp.dot(mg_ref[...], wo_ref[...], preferred_element_type=F32)
    y_ref[...] = _layer_norm(z, g_ref[...], b_ref[...])


def merge_ln(x, hm, yc, oa, w_all, wum, wuc, wua, wo, g, b):
    m = x.shape[0]
    nn = D_MODEL // MRG_TN
    g0 = PROJ_W // MRG_TN
    row = lambda w: pl.BlockSpec((MRG_TM, w), lambda i, n: (i, 0))
    merged = pl.pallas_call(
        _gate_up_kernel,
        grid=(m // MRG_TM, nn),
        in_specs=[
            row(D_MODEL), row(M_HEADS * DVP), row(CONV_WIDTH), row(A_GW),
            pl.BlockSpec((MRG_TN, D_MODEL), lambda i, n: (g0 + n, 0)),
            pl.BlockSpec((MRG_TN, D_MODEL), lambda i, n: (g0 + n + nn, 0)),
            pl.BlockSpec((MRG_TN, D_MODEL), lambda i, n: (g0 + n + 2 * nn, 0)),
            pl.BlockSpec((M_HEADS * DVP, MRG_TN), lambda i, n: (0, n)),
            pl.BlockSpec((CONV_WIDTH, MRG_TN), lambda i, n: (0, n)),
            pl.BlockSpec((A_GW, MRG_TN), lambda i, n: (0, n)),
        ],
        out_specs=pl.BlockSpec((MRG_TM, MRG_TN), lambda i, n: (i, n)),
        out_shape=jax.ShapeDtypeStruct((m, D_MODEL), BF16),
        scratch_shapes=[pltpu.VMEM((MRG_TM, D_MODEL), BF16)],
        compiler_params=_cparams(("parallel", "arbitrary")),
        name="gate_up",
    )(x, hm, yc, oa, w_all, w_all, w_all, wum, wuc, wua)
    rows = pl.BlockSpec((MRG_TM, D_MODEL), lambda i: (i, 0))
    vec = pl.BlockSpec((1, D_MODEL), lambda i: (0, 0))
    return pl.pallas_call(
        _out_ln_kernel,
        grid=(m // MRG_TM,),
        in_specs=[rows, rows, pl.BlockSpec((D_MODEL, D_MODEL), lambda i: (0, 0)), vec, vec],
        out_specs=rows,
        out_shape=jax.ShapeDtypeStruct((m, D_MODEL), F32),
        compiler_params=_cparams(("parallel",)),
        name="out_ln",
    )(x, merged, wo, g.reshape(1, D_MODEL), b.reshape(1, D_MODEL))


def _conv_prompt_kernel(cb_ref, cc_ref, ch_ref, w_ref, y_dst_ref, y_ref, st_ref, u_scr):
    del y_dst_ref
    u = cc_ref[...] * ch_ref[...]
    u_scr[pl.ds(0, 8), :] = jnp.zeros((8, CONV_WIDTH), F32)
    u_scr[pl.ds(8, SEQ), :] = u
    w = w_ref[...]
    acc = (w[0:1, :] * u_scr[pl.ds(6, SEQ), :] + w[1:2, :] * u_scr[pl.ds(7, SEQ), :]
           + w[2:3, :] * u)
    y_ref[...] = (cb_ref[...] * acc).astype(BF16)
    st_ref[...] = u_scr[pl.ds(8 + SEQ - (CONV_K - 1), CONV_K - 1), :]


def conv_prompt(proj, conv_w, y_dst):
    blk = lambda c: pl.BlockSpec((SEQ, CONV_WIDTH), lambda b, c=c: (b, c // CONV_WIDTH))
    return pl.pallas_call(
        _conv_prompt_kernel,
        grid=(BATCH,),
        in_specs=[blk(C_CB), blk(C_CC), blk(C_CH),
                  pl.BlockSpec((CONV_K, CONV_WIDTH), lambda b: (0, 0)),
                  pl.BlockSpec(memory_space=pl.ANY)],
        out_specs=[pl.BlockSpec((SEQ, CONV_WIDTH), lambda b: (b, 0)),
                   pl.BlockSpec((None, CONV_K - 1, CONV_WIDTH), lambda b: (b, 0, 0))],
        out_shape=[jax.ShapeDtypeStruct((M_ROWS, CONV_WIDTH), BF16),
                   jax.ShapeDtypeStruct((BATCH, CONV_K - 1, CONV_WIDTH), F32)],
        input_output_aliases={4: 0},
        scratch_shapes=[pltpu.VMEM((SEQ + 8, CONV_WIDTH), F32)],
        compiler_params=_cparams(("parallel",)),
        name="conv_prompt",
    )(proj, proj, proj, conv_w, y_dst)


def _conv_sample_kernel(cb_ref, cc_ref, ch_ref, prev_ref, w_ref, y_dst_ref, y_ref, st_ref):
    del y_dst_ref
    u = cc_ref[...] * ch_ref[...]
    w = w_ref[...]
    p0 = prev_ref[:, 0, :]
    p1 = prev_ref[:, 1, :]
    acc = w[0:1, :] * p0 + w[1:2, :] * p1 + w[2:3, :] * u
    y_ref[...] = (cb_ref[...] * acc).astype(BF16)
    st_ref[:, 0, :] = p1
    st_ref[:, 1, :] = u


def conv_sample(proj, prev, conv_w, y_dst):
    rb = M_PROMPT // DEC_BATCH
    blk = lambda c: pl.BlockSpec((DEC_BATCH, CONV_WIDTH), lambda i, c=c: (rb, c // CONV_WIDTH))
    full3 = pl.BlockSpec((DEC_BATCH, CONV_K - 1, CONV_WIDTH), lambda i: (0, 0, 0))
    return pl.pallas_call(
        _conv_sample_kernel,
        grid=(1,),
        in_specs=[blk(C_CB), blk(C_CC), blk(C_CH), full3,
                  pl.BlockSpec((CONV_K, CONV_WIDTH), lambda i: (0, 0)),
                  pl.BlockSpec(memory_space=pl.ANY)],
        out_specs=[pl.BlockSpec((DEC_BATCH, CONV_WIDTH), lambda i: (rb, 0)), full3],
        out_shape=[jax.ShapeDtypeStruct((M_ROWS, CONV_WIDTH), BF16),
                   jax.ShapeDtypeStruct((DEC_BATCH, CONV_K - 1, CONV_WIDTH), F32)],
        input_output_aliases={5: 0},
        compiler_params=_cparams(("arbitrary",)),
        name="conv_sample",
    )(proj, proj, proj, prev, conv_w, y_dst)


M_L = 128


def _log_sigmoid(x):
    return jnp.minimum(x, 0.0) - jnp.log1p(jnp.exp(-jnp.abs(x)))


def _head_norm_gate(h, o_pre, gain):
    lane = lax.broadcasted_iota(jnp.int32, h.shape, 1)
    real = lane < M_DV
    mu = jnp.sum(h, axis=-1, keepdims=True) * (1.0 / M_DV)
    hc = jnp.where(real, h - mu, 0.0)
    var = jnp.sum(hc * hc, axis=-1, keepdims=True) * (1.0 / M_DV)
    return _sigmoid(o_pre) * (hc * lax.rsqrt(var + LN_EPS) * gain)


M_TS = 512


M_NROW = M_DV


def _mlstm_prompt_kernel(q_ref, k_ref, v_ref, o_ref, if_ref, bias_ref, gain_ref, hm_dst_ref,
                         hm_ref, c_out_ref, n_out_ref, m_out_ref, ct_scr, m_scr):
    del hm_dst_ref
    step = pl.program_id(1)

    @pl.when(step == 0)
    def _():
        ct_scr[...] = jnp.zeros_like(ct_scr)
        m_scr[...] = jnp.zeros_like(m_scr)

    row = lax.broadcasted_iota(jnp.int32, (M_L, M_L), 0)
    col = lax.broadcasted_iota(jnp.int32, (M_L, M_L), 1)
    causal_t = row <= col
    tri = (col <= row).astype(F32)
    bias = bias_ref[...]
    ones_lane = lax.broadcasted_iota(jnp.int32, (M_L, DVP), 1) == M_NROW
    real_rows = lax.broadcasted_iota(jnp.int32, (DVP, M_L), 0) < M_DV
    tn = (((0,), (0,)), ((), ()))

    def chunk(c, carry):
        r0 = pl.multiple_of(c * M_L, M_L)
        x_if = if_ref[pl.ds(r0, M_L), :] + bias
        log_f = _log_sigmoid(x_if)
        cs = jnp.dot(tri, log_f, preferred_element_type=F32, precision=lax.Precision.HIGHEST)
        zt = jnp.where(col < M_HEADS, x_if, cs).T
        for hd in range(M_HEADS):
            b_row = zt[M_HEADS + hd:M_HEADS + hd + 1, :]
            c_col = x_if[:, hd:hd + 1] - cs[:, M_HEADS + hd:M_HEADS + hd + 1]
            c_rep = jnp.broadcast_to(c_col, (M_L, M_L))
            m_prev = m_scr[hd]
            b_last = b_row[:, M_L - 1:M_L]

            d_t = jnp.where(causal_t, b_row + c_rep, NEG)
            inter = b_row + m_prev
            m_t = jnp.maximum(jnp.max(d_t, axis=0, keepdims=True), inter)
            q = q_ref[pl.ds(r0, M_L), pl.ds(hd * DKP, DKP)].astype(BF16)
            k = k_ref[pl.ds(r0, M_L), pl.ds(hd * DKP, DKP)] * (M_DK ** -0.5)
            v1 = jnp.where(ones_lane, 1.0, v_ref[pl.ds(r0, M_L), pl.ds(hd * DVP, DVP)]).astype(BF16)
            s_t = lax.dot_general(k.astype(BF16), q, NT_DIMS, preferred_element_type=F32) * jnp.exp(d_t - m_t)
            w_inter = jnp.exp(inter - m_t)
            ct_prev = ct_scr[hd]
            num_t = (lax.dot_general(v1, s_t.astype(BF16), tn, preferred_element_type=F32)
                     + w_inter * lax.dot_general(ct_prev.astype(BF16), q, NT_DIMS,
                                                 preferred_element_type=F32))
            den = num_t[M_NROW:M_NROW + 1, :]
            h_t = jnp.where(real_rows, num_t / jnp.maximum(jnp.abs(den), jnp.exp(-m_t)), 0.0)
            mu = jnp.sum(h_t, axis=0, keepdims=True) * (1.0 / M_DV)
            hc = jnp.where(real_rows, h_t - mu, 0.0)
            var = jnp.sum(hc * hc, axis=0, keepdims=True) * (1.0 / M_DV)
            hn = (hc * lax.rsqrt(var + LN_EPS)).T * gain_ref[hd]
            o_pre = o_ref[pl.ds(r0, M_L), pl.ds(hd * DVP, DVP)]
            hm_ref[pl.ds(r0, M_L), pl.ds(hd * DVP, DVP)] = (_sigmoid(o_pre) * hn).astype(BF16)

            m_new = jnp.maximum(b_last + m_prev, b_last + jnp.max(c_rep, axis=0, keepdims=True)[:, 0:1])
            kw = k * jnp.exp(c_rep + (b_last - m_new))
            ct_scr[hd] = (jnp.exp(b_last + m_prev - m_new) * ct_prev
                          + lax.dot_general(v1, kw.astype(BF16), tn, preferred_element_type=F32))
            m_scr[hd] = m_new
        return carry

    lax.fori_loop(0, M_TS // M_L, chunk, 0)

    @pl.when(step == pl.num_programs(1) - 1)
    def _():
        for hd in range(M_HEADS):
            ct = ct_scr[hd]
            c_out_ref[hd] = ct.T[0:M_DK, 0:M_DV]
            n_out_ref[hd] = ct[M_NROW:M_NROW + 1, 0:M_DK]
            m_out_ref[hd] = m_scr[hd]


def mlstm_prompt(proj, bias, gain, hm_dst):
    ns = SEQ // M_TS
    def cblk(c0, w):
        return pl.BlockSpec((M_TS, w), lambda b, s: (b * ns + s, c0 // w))
    return pl.pallas_call(
        _mlstm_prompt_kernel,
        grid=(BATCH, ns),
        in_specs=[cblk(C_MQ, M_HEADS * DKP), cblk(C_MK, M_HEADS * DKP),
                  cblk(C_MV, M_HEADS * DVP), cblk(C_MO, M_HEADS * DVP),
                  cblk(C_IF, LANES),
                  pl.BlockSpec((1, LANES), lambda b, s: (0, 0)),
                  pl.BlockSpec((M_HEADS, 1, DVP), lambda b, s: (0, 0, 0)),
                  pl.BlockSpec(memory_space=pl.ANY)],
        out_specs=[pl.BlockSpec((M_TS, M_HEADS * DVP), lambda b, s: (b * ns + s, 0)),
                   pl.BlockSpec((None, M_HEADS, M_DK, M_DV), lambda b, s: (b, 0, 0, 0)),
                   pl.BlockSpec((None, M_HEADS, 1, M_DK), lambda b, s: (b, 0, 0, 0)),
                   pl.BlockSpec((None, M_HEADS, 1, 1), lambda b, s: (b, 0, 0, 0))],
        out_shape=[jax.ShapeDtypeStruct((M_ROWS, M_HEADS * DVP), BF16),
                   jax.ShapeDtypeStruct((BATCH, M_HEADS, M_DK, M_DV), F32),
                   jax.ShapeDtypeStruct((BATCH, M_HEADS, 1, M_DK), F32),
                   jax.ShapeDtypeStruct((BATCH, M_HEADS, 1, 1), F32)],
        scratch_shapes=[pltpu.VMEM((M_HEADS, DVP, DKP), F32), pltpu.VMEM((M_HEADS, 1, 1), F32)],
        compiler_params=_cparams(("parallel", "arbitrary")),
        input_output_aliases={7: 0},
        name="mlstm_prompt",
    )(proj, proj, proj, proj, proj, bias, gain, hm_dst)


MS_DC = 48


def _pick_row(x8, j):
    rows = lax.broadcasted_iota(jnp.int32, x8.shape, 0)
    return jnp.sum(jnp.where(rows == j, x8, 0.0), axis=0, keepdims=True)


def _mlstm_sample_kernel(q_ref, k_ref, v_ref, o_ref, if_ref, bias_ref, gain_ref,
                         c0_ref, n0_ref, m0_ref, hm_dst_ref, c_dst_ref,
                         hm_ref, c_out_ref, n_out_ref, m_out_ref,
                         qt_scr, kw_scr, vt_scr, acc_scr, st_scr):
    del hm_dst_ref, c_dst_ref
    hd = pl.program_id(0)
    c = pl.program_id(1)

    @pl.when(c == 0)
    def _():
        qt = q_ref[...].T
        kt = (k_ref[...] * (M_DK ** -0.5)).T
        vt_scr[...] = v_ref[...].T
        x_if = if_ref[...].T[0:2 * M_HEADS, :] + bias_ref[...]
        i_pre = _pick_row(x_if, hd)
        log_f = _log_sigmoid(_pick_row(x_if, hd + M_HEADS))
        inter = log_f + m0_ref[...]
        m_new = jnp.maximum(i_pre, inter)
        w_k = jnp.exp(i_pre - m_new)
        decay = jnp.exp(inter - m_new)
        n_prev = n0_ref[...]
        s = jnp.sum(qt * kt, axis=0, keepdims=True) * w_k
        den = s + decay * jnp.sum(qt[:M_DK] * n_prev, axis=0, keepdims=True)
        kw = kt * w_k
        qt_scr[...] = qt
        kw_scr[...] = kw
        n_out_ref[...] = decay * n_prev + kw[:M_DK]
        m_out_ref[...] = m_new
        st_scr[0:1, :] = s
        st_scr[1:2, :] = decay
        st_scr[2:3, :] = den
        st_scr[3:4, :] = m_new
        acc_scr[...] = jnp.zeros_like(acc_scr)

    decay = st_scr[1:2, :]
    vt = vt_scr[pl.ds(0, M_DV), :]

    def tile(t, acc):
        r8 = pl.multiple_of(c * MS_DC + t * 8, 8)
        q8 = qt_scr[pl.ds(r8, 8), :]
        kw8 = kw_scr[pl.ds(r8, 8), :]
        for r in range(8):
            c_row = c0_ref[t * 8 + r]
            c_out_ref[t * 8 + r] = decay * c_row + kw8[r:r + 1, :] * vt
            acc = acc + q8[r:r + 1, :] * c_row
        return acc

    acc = lax.fori_loop(0, MS_DC // 8, tile, acc_scr[...])
    acc_scr[...] = acc

    @pl.when(c == pl.num_programs(1) - 1)
    def _():
        s = st_scr[0:1, :]
        den = st_scr[2:3, :]
        m_t = st_scr[3:4, :]
        h = (s * vt + decay * acc) / jnp.maximum(jnp.abs(den), jnp.exp(-m_t))
        mu = jnp.mean(h, axis=0, keepdims=True)
        hc = h - mu
        var = jnp.mean(hc * hc, axis=0, keepdims=True)
        o_pre = o_ref[...].T[:M_DV, :]
        out = _sigmoid(o_pre) * (hc * lax.rsqrt(var + LN_EPS) * gain_ref[...])
        out = jnp.concatenate([out, jnp.zeros((DVP - M_DV, DEC_BATCH), F32)], axis=0)
        hm_ref[...] = out.T.astype(BF16)


def mlstm_sample(proj, bias_col, gain_col, c0t, n0t, m0t, hm_dst, c_dst, layer):
    any_spec = pl.BlockSpec(memory_space=pl.ANY)
    rb = M_PROMPT // DEC_BATCH
    nc = M_DK // MS_DC
    def cblk(c0_, w):
        return pl.BlockSpec((DEC_BATCH, w), lambda h, c: (rb, c0_ // w + h))
    return pl.pallas_call(
        _mlstm_sample_kernel,
        grid=(M_HEADS, nc),
        in_specs=[cblk(C_MQ, DKP), cblk(C_MK, DKP), cblk(C_MV, DVP), cblk(C_MO, DVP),
                  pl.BlockSpec((DEC_BATCH, LANES), lambda h, c: (rb, C_IF // LANES)),
                  pl.BlockSpec((2 * M_HEADS, DEC_BATCH), lambda h, c: (0, 0)),
                  pl.BlockSpec((None, M_DV, 1), lambda h, c: (h, 0, 0)),
                  pl.BlockSpec((None, None, MS_DC, M_DV, DEC_BATCH), lambda h, c: (layer, h, c, 0, 0)),
                  pl.BlockSpec((None, None, M_DK, DEC_BATCH), lambda h, c: (layer, h, 0, 0)),
                  pl.BlockSpec((None, None, 1, DEC_BATCH), lambda h, c: (layer, h, 0, 0)),
                  any_spec, any_spec],
        out_specs=[pl.BlockSpec((DEC_BATCH, DVP), lambda h, c: (rb, h)),
                   pl.BlockSpec((None, None, MS_DC, M_DV, DEC_BATCH), lambda h, c: (layer, h, c, 0, 0)),
                   pl.BlockSpec((None, M_DK, DEC_BATCH), lambda h, c: (h, 0, 0)),
                   pl.BlockSpec((None, 1, DEC_BATCH), lambda h, c: (h, 0, 0))],
        out_shape=[jax.ShapeDtypeStruct((M_ROWS, M_HEADS * DVP), BF16),
                   jax.ShapeDtypeStruct((DEPTH, M_HEADS, M_DK, M_DV, DEC_BATCH), F32),
                   jax.ShapeDtypeStruct((M_HEADS, M_DK, DEC_BATCH), F32),
                   jax.ShapeDtypeStruct((M_HEADS, 1, DEC_BATCH), F32)],
        scratch_shapes=[pltpu.VMEM((DKP, DEC_BATCH), F32), pltpu.VMEM((DKP, DEC_BATCH), F32),
                        pltpu.VMEM((DVP, DEC_BATCH), F32), pltpu.VMEM((M_DV, DEC_BATCH), F32),
                        pltpu.VMEM((8, DEC_BATCH), F32)],
        input_output_aliases={10: 0, 11: 1},
        compiler_params=_cparams(("arbitrary", "arbitrary")),
        name="mlstm_sample",
    )(proj, proj, proj, proj, proj, bias_col, gain_col, c0t, n0t, m0t, hm_dst, c_dst)


A_Q = 128
A_LT = A_GW // LANES


def _rope(x, cos, sin_signed):
    lane = lax.broadcasted_iota(jnp.int32, x.shape, 1)
    first_half = (lane % A_HEAD_DIM) < (A_HEAD_DIM // 2)
    partner = jnp.where(first_half, pltpu.roll(x, x.shape[1] - A_HEAD_DIM // 2, 1),
                        pltpu.roll(x, A_HEAD_DIM // 2, 1))
    return x * cos + partner * sin_signed


def _head_masks(shape):
    lane = lax.broadcasted_iota(jnp.int32, shape, 1)
    return [(lane // A_HEAD_DIM) == h for h in range(A_HPG)]


def _attn_group_prompt(dil, gi, qs_scr, ks_scr, vs_scr, o_scr, l_scr):
    length = SEQ // dil
    nb = length // A_Q
    row = lax.broadcasted_iota(jnp.int32, (A_Q, A_Q), 0)
    col = lax.broadcasted_iota(jnp.int32, (A_Q, A_Q), 1)
    cur_ok = col <= row
    prev_ok = col >= row
    masks = _head_masks((A_Q, A_GW))
    nt = (((1,), (1,)), ((), ()))

    def window(start):
        if dil == 1:
            return pl.ds(pl.multiple_of(start, A_Q), A_Q)
        return pl.ds(start, A_Q, stride=dil)

    def rows(scr, start):
        w = window(start)
        return jnp.concatenate([scr[t, w, :] for t in range(A_LT)], axis=1).astype(BF16)

    def block(idx, carry):
        r = idx % dil
        n = idx // dil
        base = r + (dil * A_Q) * n
        qb = rows(qs_scr, base)
        kc = rows(ks_scr, base)
        vc = rows(vs_scr, base)
        if nb > 1:
            pbase = jnp.maximum(base - dil * A_Q, r)
            kp = rows(ks_scr, pbase)
            vp = rows(vs_scr, pbase)
            has_prev = n > 0
        o_acc = jnp.zeros((A_Q, A_GW), F32)
        l_acc = jnp.zeros((A_Q, A_GW), F32)
        for h in range(A_HPG):
            qh = jnp.where(masks[h], qb, jnp.zeros_like(qb))
            s_c = jnp.where(cur_ok, lax.dot_general(qh, kc, nt, preferred_element_type=F32), NEG)
            m = jnp.max(s_c, axis=1, keepdims=True)
            if nb > 1:
                s_p = jnp.where(jnp.logical_and(prev_ok, has_prev),
                                lax.dot_general(qh, kp, nt, preferred_element_type=F32), NEG)
                m = jnp.maximum(m, jnp.max(s_p, axis=1, keepdims=True))
            p_c = jnp.exp(s_c - m)
            l = jnp.sum(p_c, axis=1, keepdims=True)
            o_h = jnp.dot(p_c.astype(BF16), vc, preferred_element_type=F32)
            if nb > 1:
                p_p = jnp.exp(s_p - m)
                l = l + jnp.sum(p_p, axis=1, keepdims=True)
                o_h = o_h + jnp.dot(p_p.astype(BF16), vp, preferred_element_type=F32)
            o_acc = o_acc + jnp.where(masks[h], o_h / l, 0.0)
            l_acc = l_acc + jnp.where(masks[h], m + jnp.log(l), 0.0)
        w = window(base)
        for t in range(A_LT):
            o_scr[gi, t, w, :] = o_acc[:, t * LANES:(t + 1) * LANES]
            l_scr[gi, t, w, :] = l_acc[:, t * LANES:(t + 1) * LANES]
        return carry

    lax.fori_loop(0, dil * nb, block, 0)


def _attn_prompt_kernel(q_ref, k_ref, v_ref, cos_ref, sin_ref, oa_dst_ref, kvd0_ref, kvd1_ref, kvd2_ref,
                        oa_ref, kv0_ref, kv1_ref, kv2_ref, qs_scr, ks_scr, vs_scr, o_scr, l_scr):
    del oa_dst_ref, kvd0_ref, kvd1_ref, kvd2_ref
    g = pl.program_id(1)
    rc = 256

    def rope_rows(c, carry):
        sl = pl.ds(pl.multiple_of(c * rc, rc), rc)
        cos = cos_ref[sl, :]
        sin = sin_ref[sl, :]
        for t in range(A_LT):
            lanes = pl.ds(t * LANES, LANES)
            qs_scr[t, sl, :] = _rope(q_ref[sl, lanes], cos, sin) * (A_HEAD_DIM ** -0.5)
            ks_scr[t, sl, :] = _rope(k_ref[sl, lanes], cos, sin)
            vs_scr[t, sl, :] = v_ref[sl, lanes]
        return carry

    lax.fori_loop(0, SEQ // rc, rope_rows, 0)

    for gi, ((win, dil), kv_ref) in enumerate(zip(A_GROUPS, (kv0_ref, kv1_ref, kv2_ref))):
        @pl.when(g == gi)
        def _(gi=gi, dil=dil, win=win, kv_ref=kv_ref):
            _attn_group_prompt(dil, gi, qs_scr, ks_scr, vs_scr, o_scr, l_scr)
            keep = min(win, SEQ)
            for c in range(keep // LANES):
                rows = pl.ds(SEQ - keep + c * LANES, LANES)
                for t in range(A_LT):
                    kv_ref[0, pl.ds(t * LANES, LANES), pl.ds(c * LANES, LANES)] = ks_scr[t, rows, :].T
                    kv_ref[1, pl.ds(t * LANES, LANES), pl.ds(c * LANES, LANES)] = vs_scr[t, rows, :].T

    @pl.when(g == len(A_GROUPS) - 1)
    def _():
        def comb(c, carry):
            sl = pl.ds(pl.multiple_of(c * rc, rc), rc)
            for t in range(A_LT):
                l0, l1, l2 = l_scr[0, t, sl, :], l_scr[1, t, sl, :], l_scr[2, t, sl, :]
                mx = jnp.maximum(jnp.maximum(l0, l1), l2)
                e0, e1, e2 = jnp.exp(l0 - mx), jnp.exp(l1 - mx), jnp.exp(l2 - mx)
                tot = e0 * o_scr[0, t, sl, :] + e1 * o_scr[1, t, sl, :] + e2 * o_scr[2, t, sl, :]
                oa_ref[sl, pl.ds(t * LANES, LANES)] = (tot / (e0 + e1 + e2)).astype(BF16)
            return carry
        lax.fori_loop(0, SEQ // rc, comb, 0)


def attn_prompt(proj, cos, sin, oa_dst, kv_dst, layer):
    def gblk(c0):
        return pl.BlockSpec((SEQ, A_GW), lambda b, g: (b, c0 // A_GW + g))
    tab = pl.BlockSpec((SEQ, LANES), lambda b, g: (0, 0))
    keeps = [min(win, SEQ) for win, _ in A_GROUPS]
    return pl.pallas_call(
        _attn_prompt_kernel,
        grid=(BATCH, len(A_GROUPS)),
        in_specs=[gblk(C_AQ), gblk(C_AK), gblk(C_AV), tab, tab] + [pl.BlockSpec(memory_space=pl.ANY)] * 4,
        out_specs=[pl.BlockSpec((SEQ, A_GW), lambda b, g: (b, 0))]
                  + [pl.BlockSpec((None, None, 2, A_GW, kp), lambda b, g: (layer, b, 0, 0, 0)) for kp in keeps],
        out_shape=[jax.ShapeDtypeStruct((M_ROWS, A_GW), BF16)]
                  + [jax.ShapeDtypeStruct((DEPTH, BATCH, 2, A_GW, kp), F32) for kp in keeps],
        input_output_aliases={5: 0, 6: 1, 7: 2, 8: 3},
        scratch_shapes=[pltpu.VMEM((A_LT, SEQ, LANES), F32),
                        pltpu.VMEM((A_LT, SEQ, LANES), F32),
                        pltpu.VMEM((A_LT, SEQ, LANES), F32),
                        pltpu.VMEM((len(A_GROUPS), A_LT, SEQ, LANES), F32),
                        pltpu.VMEM((len(A_GROUPS), A_LT, SEQ, LANES), F32)],
        compiler_params=_cparams(("parallel", "arbitrary")),
        name="attn_prompt",
    )(proj, proj, proj, cos, sin, oa_dst, *kv_dst)


AS_BB = 2


def _attn_sample_kernel(q_ref, k_ref, v_ref, cos_ref, sin_ref, c0_ref, c1_ref, c2_ref, oa_dst_ref,
                        oa_ref, kt_ref, vt_ref, qt_scr, s0_scr, ot_scr, lt_scr):
    del oa_dst_ref
    i = pl.program_id(0)
    lane_b = lax.broadcasted_iota(jnp.int32, (1, DEC_BATCH), 1)
    sub8 = lax.broadcasted_iota(jnp.int32, (8, DEC_BATCH), 0)

    @pl.when(i == 0)
    def _():
        cos = cos_ref[...]
        sin = sin_ref[...]
        for gi in range(len(A_GROUPS)):
            gs = pl.ds(gi * A_GW, A_GW)
            qt = (_rope(q_ref[:, gs], cos, sin) * (A_HEAD_DIM ** -0.5)).T
            kt = _rope(k_ref[:, gs], cos, sin).T
            qt_scr[gi] = qt
            kt_ref[gi] = kt
            vt_ref[gi] = v_ref[:, gs].T
            prod = qt * kt
            s0 = jnp.zeros((8, DEC_BATCH), F32)
            for h in range(A_HPG):
                part = jnp.sum(prod[h * A_HEAD_DIM:(h + 1) * A_HEAD_DIM], axis=0, keepdims=True)
                s0 = jnp.where(sub8 == h, part, s0)
            s0_scr[gi] = s0
        ot_scr[...] = jnp.zeros_like(ot_scr)
        lt_scr[...] = jnp.zeros_like(lt_scr)

    for bl in range(AS_BB):
        pick = lane_b == i * AS_BB + bl
        for gi, (cache_ref, (_, dil)) in enumerate(zip((c0_ref, c1_ref, c2_ref), A_GROUPS)):
            wb = cache_ref.shape[-1]
            if dil > 1:
                keep = (lax.broadcasted_iota(jnp.int32, (1, wb), 1) & (dil - 1)) == 0
            head_row = lax.broadcasted_iota(jnp.int32, (8, wb), 0)
            q_col = jnp.sum(jnp.where(pick, qt_scr[gi], 0.0), axis=1, keepdims=True)
            v_col = jnp.sum(jnp.where(pick, vt_ref[gi], 0.0), axis=1, keepdims=True)
            s0 = jnp.sum(jnp.where(pick, s0_scr[gi], 0.0), axis=1, keepdims=True)
            s = jnp.zeros((8, wb), F32)
            for h in range(A_HPG):
                hs = slice(h * A_HEAD_DIM, (h + 1) * A_HEAD_DIM)
                part = jnp.sum(q_col[hs] * cache_ref[bl, 0, h], axis=0, keepdims=True)
                s = jnp.where(head_row == h, part, s)
            if dil > 1:
                s = jnp.where(keep, s, NEG)
            m = jnp.maximum(jnp.max(s, axis=1, keepdims=True), s0)
            p = jnp.exp(s - m)
            p0 = jnp.exp(s0 - m)
            l = jnp.sum(p, axis=1, keepdims=True) + p0
            lse = m + jnp.log(l)
            o_parts = []
            for h in range(A_HPG):
                hs = slice(h * A_HEAD_DIM, (h + 1) * A_HEAD_DIM)
                pv = jnp.sum(p[h:h + 1, :] * cache_ref[bl, 1, h], axis=1, keepdims=True)
                o_parts.append((pv + p0[h:h + 1, :] * v_col[hs]) / l[h:h + 1, :])
            o = jnp.concatenate(o_parts, axis=0)
            ot_scr[gi] = jnp.where(pick, o, ot_scr[gi])
            lt_scr[gi] = jnp.where(pick, lse, lt_scr[gi])

    @pl.when(i == pl.num_programs(0) - 1)
    def _():
        for h in range(A_HPG):
            hs = pl.ds(h * A_HEAD_DIM, A_HEAD_DIM)
            l0, l1, l2 = (lt_scr[gi, pl.ds(h, 1), :] for gi in range(3))
            mx = jnp.maximum(jnp.maximum(l0, l1), l2)
            e0, e1, e2 = jnp.exp(l0 - mx), jnp.exp(l1 - mx), jnp.exp(l2 - mx)
            tot = e0 * ot_scr[0, hs, :] + e1 * ot_scr[1, hs, :] + e2 * ot_scr[2, hs, :]
            ot_scr[0, hs, :] = tot / (e0 + e1 + e2)
        oa_ref[...] = ot_scr[0].T.astype(BF16)


def attn_sample(proj, cos, sin, caches_t, oa_dst, layer):
    rb = M_PROMPT // DEC_BATCH
    def pblk(c0):
        return pl.BlockSpec((DEC_BATCH, A_WIDTH), lambda i: (rb, c0 // A_WIDTH))
    tab = pl.BlockSpec((1, A_GW), lambda i: (0, 0))
    cache_specs = [pl.BlockSpec((None, AS_BB, 2, A_HPG, A_HEAD_DIM, ct.shape[-1]),
                                lambda i: (layer, i, 0, 0, 0, 0)) for ct in caches_t]
    ng = len(A_GROUPS)
    full3 = pl.BlockSpec((ng, A_GW, DEC_BATCH), lambda i: (0, 0, 0))
    return pl.pallas_call(
        _attn_sample_kernel,
        grid=(DEC_BATCH // AS_BB,),
        in_specs=[pblk(C_AQ), pblk(C_AK), pblk(C_AV), tab, tab] + cache_specs
                 + [pl.BlockSpec(memory_space=pl.ANY)],
        out_specs=[pl.BlockSpec((DEC_BATCH, A_GW), lambda i: (rb, 0)), full3, full3],
        out_shape=[jax.ShapeDtypeStruct((M_ROWS, A_GW), BF16),
                   jax.ShapeDtypeStruct((ng, A_GW, DEC_BATCH), F32),
                   jax.ShapeDtypeStruct((ng, A_GW, DEC_BATCH), F32)],
        scratch_shapes=[pltpu.VMEM((ng, A_GW, DEC_BATCH), F32), pltpu.VMEM((ng, 8, DEC_BATCH), F32),
                        pltpu.VMEM((ng, A_GW, DEC_BATCH), F32), pltpu.VMEM((ng, 8, DEC_BATCH), F32)],
        compiler_params=_cparams(("arbitrary",)),
        input_output_aliases={8: 0},
        name="attn_sample",
    )(proj, proj, proj, cos, sin, *caches_t, oa_dst)


def _pad_heads(wt, d, dp):
    c = wt.shape[1]
    wt = wt.reshape(M_HEADS, d, c)
    return jnp.pad(wt, ((0, 0), (0, dp - d), (0, 0))).reshape(M_HEADS * dp, c)


def _layer_weights(w_in_l, w_up_m, w_up_c, w_up_a, w_o_l):
    w_in_t = jnp.transpose(w_in_l)
    o = IN_OFFSETS
    piece = lambda i: w_in_t[o[i]:o[i + 1]]
    mq, mk, mv, mi, mf, mo, cb, cc, ch, aq, ak, av, gt = [piece(i) for i in range(13)]
    w_all = jnp.concatenate(
        [_pad_heads(mv, M_DV, DVP), _pad_heads(mo, M_DV, DVP),
         _pad_heads(mq, M_DK, DKP), _pad_heads(mk, M_DK, DKP),
         cb, cc, ch, aq, ak, av,
         mi, mf, jnp.zeros((PROJ_W - C_IF - 2 * M_HEADS, D_MODEL), F32), gt], axis=0).astype(BF16)
    w_um = jnp.pad(w_up_m.reshape(M_HEADS, M_DV, D_MODEL),
                   ((0, 0), (0, DVP - M_DV), (0, 0))).reshape(M_HEADS * DVP, D_MODEL)
    return (w_all, w_um.astype(BF16), w_up_c.astype(BF16), w_up_a.astype(BF16), w_o_l.astype(BF16))


def _rope_tables(pos):
    half = A_HEAD_DIM // 2
    inv = ROPE_THETA ** (-(2.0 * jnp.arange(half, dtype=F32)) / A_HEAD_DIM)
    ang = pos.astype(F32)[:, None] * inv[None, :]
    cos = jnp.cos(ang)
    sin = jnp.sin(ang)
    cos = jnp.tile(jnp.concatenate([cos, cos], axis=-1), (1, A_HPG))
    sin = jnp.tile(jnp.concatenate([-sin, sin], axis=-1), (1, A_HPG))
    return cos, sin


def kernel(x_prompt, x_sample, state_mlstm_C, state_mlstm_n, state_mlstm_m, state_conv,
           cache_attn_kv_w128, cache_attn_kv_w512, cache_attn_kv_w2048,
           w_in, b_gate_if, mlstm_norm_g, conv_w, w_up_mlstm, w_up_conv, w_up_attn, w_o,
           w_ffn_in, w_ffn_out, ln_g, ln_b):
    x = jnp.concatenate([x_prompt.reshape(M_PROMPT, D_MODEL),
                         x_sample.reshape(DEC_BATCH, D_MODEL)], axis=0)
    cos_p, sin_p = _rope_tables(jnp.arange(SEQ))
    cos_s, sin_s = _rope_tables(PAST_LEN + jnp.arange(1))
    w_ffn_in_b = w_ffn_in.astype(BF16)
    w_ffn_out_b = (0.5 * w_ffn_out).astype(BF16)

    c0t = jnp.transpose(state_mlstm_C, (0, 2, 3, 4, 1))
    n0t = jnp.transpose(state_mlstm_n, (0, 2, 3, 1))
    m0t = jnp.transpose(state_mlstm_m, (0, 2, 1)).reshape(DEPTH, M_HEADS, 1, DEC_BATCH)
    caches_t = [jnp.transpose(c, (0, 1, 3, 4, 5, 2))
                for c in (cache_attn_kv_w128, cache_attn_kv_w512, cache_attn_kv_w2048)]

    keeps = [min(win, SEQ) for win, _ in A_GROUPS]
    kv_all = [jnp.zeros((DEPTH, BATCH, 2, A_GW, kp), F32) for kp in keeps]
    sct_all = jnp.zeros((DEPTH, M_HEADS, M_DK, M_DV, DEC_BATCH), F32)
    hm = jnp.zeros((M_ROWS, M_HEADS * DVP), BF16)
    yc = jnp.zeros((M_ROWS, CONV_WIDTH), BF16)
    oa = jnp.zeros((M_ROWS, A_GW), BF16)

    p_states, s_states = [], []
    for l in range(DEPTH):
        w_all, w_um, w_uc, w_ua, w_ol = _layer_weights(
            w_in[l], w_up_mlstm[l], w_up_conv[l], w_up_attn[l], w_o[l])
        bias = jnp.pad(b_gate_if[l], (0, LANES - 2 * M_HEADS)).reshape(1, LANES)
        bias_col = jnp.broadcast_to(b_gate_if[l][:, None], (2 * M_HEADS, DEC_BATCH))
        gain = jnp.pad(mlstm_norm_g[l].reshape(M_HEADS, M_DV), ((0, 0), (0, DVP - M_DV)))
        gain_col = mlstm_norm_g[l].reshape(M_HEADS, M_DV, 1)

        x = ffn_ln(x, w_ffn_in_b, w_ffn_out_b, ln_g[l, 0], ln_b[l, 0], l, 0)
        proj = branch_proj(x, w_all)

        hm, pc, pn, pm = mlstm_prompt(proj, bias, gain.reshape(M_HEADS, 1, DVP), hm)
        hm, sct_all, snt, smt = mlstm_sample(proj, bias_col, gain_col, c0t, n0t, m0t, hm, sct_all, l)
        yc, pconv = conv_prompt(proj, conv_w[l], yc)
        yc, sconv = conv_sample(proj, state_conv[l], conv_w[l], yc)
        oa, *kv_all = attn_prompt(proj, cos_p[:, :LANES], sin_p[:, :LANES], oa, kv_all, l)
        oa, kt_s, vt_s = attn_sample(proj, cos_s, sin_s, caches_t, oa, l)

        x = merge_ln(x, hm, yc, oa, w_all, w_um, w_uc, w_ua, w_ol, ln_g[l, 1], ln_b[l, 1])
        x = ffn_ln(x, w_ffn_in_b, w_ffn_out_b, ln_g[l, 2], ln_b[l, 2], l, 1)

        kt_s = kt_s.reshape(3, A_HPG, A_HEAD_DIM, DEC_BATCH)
        vt_s = vt_s.reshape(3, A_HPG, A_HEAD_DIM, DEC_BATCH)
        kv_s = [jnp.stack([kt_s[gi], vt_s[gi]], axis=0) for gi in range(3)]
        p_states.append((pc, pn.reshape(BATCH, M_HEADS, M_DK), pm.reshape(BATCH, M_HEADS), pconv))
        s_states.append((snt, smt, sconv, kv_s[0], kv_s[1], kv_s[2]))

    y_prompt = x[:M_PROMPT].reshape(BATCH, SEQ, D_MODEL)
    y_sample = x[M_PROMPT:].reshape(DEC_BATCH, 1, D_MODEL)
    p_out = [jnp.stack(z) for z in zip(*p_states)]
    p_out += [jnp.transpose(kvt.reshape(DEPTH, BATCH, 2, A_HPG, A_HEAD_DIM, kvt.shape[-1]),
                            (0, 1, 5, 2, 3, 4))
              for kvt in kv_all]
    snt, smt, sconv, kv0, kv1, kv2 = [jnp.stack(z) for z in zip(*s_states)]
    s_out = [jnp.transpose(sct_all, (0, 4, 1, 2, 3)),
             jnp.transpose(snt, (0, 3, 1, 2)),
             jnp.transpose(smt.reshape(DEPTH, M_HEADS, DEC_BATCH), (0, 2, 1)),
             sconv]
    s_out += [jnp.transpose(kv, (0, 4, 1, 2, 3)).reshape(DEPTH, DEC_BATCH, 1, 2, A_HPG, A_HEAD_DIM)
              for kv in (kv0, kv1, kv2)]
    return (y_prompt, y_sample, *p_out, *s_out)
```
